```python
import jax, jax.numpy as jnp
from jax import lax
import numpy as np

D_MODEL = 4096
BATCH = 2
SEQ = 8192
DEPTH = 1

HEAD_DIM = 128
HGRN_EXPAND = 128
HGRN_WIDTH = D_MODEL // 2
HGRN_HEADS = HGRN_WIDTH // HGRN_EXPAND
HGRN_VDIM = HGRN_WIDTH // HGRN_HEADS
ATTN_WIDTH = D_MODEL // 2
ATTN_HEADS = ATTN_WIDTH // HEAD_DIM
KV_HEADS = ATTN_HEADS // 4
GQA_GROUP = ATTN_HEADS // KV_HEADS
WINDOW = 128
BLOCK = 128
CHUNK = 64
D_FF = 4 * D_MODEL
PLE_DIM = 256
ROPE_THETA = 10000.0
EPS = 1e-6

kernel_name = "hybrid_hgrn2_swa_sink_encoder_layer"


def in_proj_sizes():
    return [HGRN_HEADS * HGRN_EXPAND, HGRN_HEADS * HGRN_EXPAND, HGRN_HEADS * HGRN_EXPAND,
            HGRN_WIDTH, HGRN_WIDTH,
            ATTN_HEADS * HEAD_DIM, KV_HEADS * HEAD_DIM, KV_HEADS * HEAD_DIM,
            D_MODEL, D_MODEL]


def rms_norm(x, w):
    xf = x.astype(jnp.float32)
    y = xf * lax.rsqrt(jnp.mean(xf * xf, axis=-1, keepdims=True) + EPS)
    return (y * w.astype(jnp.float32)).astype(x.dtype)


def layer_lower_bound(lb_param, layer):
    lb = jnp.cumsum(jax.nn.softmax(lb_param.astype(jnp.float32), axis=0), axis=0)
    return lb[layer]


def gla_chunkwise(q, k, v, g):
    B, S, H, K = q.shape
    V = v.shape[-1]
    N = S // CHUNK

    def to_chunks(t):
        return t.reshape(B, N, CHUNK, H, t.shape[-1]).transpose(1, 0, 3, 2, 4)

    q, k, v, g = map(to_chunks, (q, k, v, g))
    b = jnp.cumsum(g, axis=3)
    b_last = b[:, :, :, -1:, :]
    q_hat = q * jnp.exp(b)
    k_hat = k * jnp.exp(-b)
    k_tail = k * jnp.exp(b_last - b)
    causal_in_chunk = jnp.tril(jnp.ones((CHUNK, CHUNK), dtype=bool))
    a = jnp.einsum('nbhck,nbhsk->nbhcs', q_hat, k_hat)
    a = jnp.where(causal_in_chunk, a, 0.0)
    o_intra = jnp.einsum('nbhcs,nbhsv->nbhcv', a, v)
    decay = jnp.exp(b_last[:, :, :, 0, :])

    def step(state, xs):
        q_n, kt_n, v_n, d_n = xs
        o_n = jnp.einsum('bhck,bhkv->bhcv', q_n, state)
        state = d_n[..., None] * state + jnp.einsum('bhck,bhcv->bhkv', kt_n, v_n)
        return state, o_n

    s0 = jnp.zeros((B, H, K, V), jnp.float32)
    _, o_inter = lax.scan(step, s0, (q_hat, k_tail, v, decay))
    o = o_intra + o_inter
    return o.transpose(1, 0, 3, 2, 4).reshape(B, S, H, V)


def hgrn2_direction(q, f_logit, v, lb):
    B, S, H, K = q.shape
    f = lb + (1.0 - lb) * jax.nn.sigmoid(f_logit.astype(jnp.float32))
    k = (1.0 - f).reshape(B, S, H, K)
    g = jnp.log(f).reshape(B, S, H, K)
    return gla_chunkwise(q.astype(jnp.float32), k, v.astype(jnp.float32), g)


def rotary(x, positions):
    d = x.shape[-1]
    inv_freq = ROPE_THETA ** (-jnp.arange(0, d, 2, dtype=jnp.float32) / d)
    ang = positions[:, None] * inv_freq[None, :]
    cos = jnp.cos(ang)[None, :, None, :].astype(x.dtype)
    sin = jnp.sin(ang)[None, :, None, :].astype(x.dtype)
    x1, x2 = jnp.split(x, 2, axis=-1)
    return jnp.concatenate([x1 * cos - x2 * sin, x2 * cos + x1 * sin], axis=-1)


def windowed_gqa_sink(q, k, v, sink):
    B, S, H, D = q.shape
    nb = S // BLOCK
    qb = q.reshape(B, nb, BLOCK, KV_HEADS, GQA_GROUP, D)

    def band(t):
        tp = jnp.pad(t, ((0, 0), (BLOCK, BLOCK), (0, 0), (0, 0)))
        tb = tp.reshape(B, nb + 2, BLOCK, KV_HEADS, D)
        return jnp.concatenate([tb[:, :-2], tb[:, 1:-1], tb[:, 2:]], axis=2)

    kw, vw = band(k), band(v)
    s = jnp.einsum('bnqhgd,bnkhd->bnhgqk', qb, kw,
                   preferred_element_type=jnp.float32) * (D ** -0.5)
    r = jnp.arange(BLOCK)[:, None]
    c = jnp.arange(3 * BLOCK)[None, :]
    kpos = jnp.arange(nb)[:, None, None] * BLOCK - BLOCK + c[None]
    mask = (jnp.abs(c - BLOCK - r)[None] <= WINDOW) & (kpos >= 0) & (kpos < S)
    mask = mask[None, :, None, None]
    s = jnp.where(mask, s, -jnp.inf)
    sk = sink.astype(jnp.float32).reshape(KV_HEADS, GQA_GROUP)[None, None, :, :, None, None]
    m = jnp.maximum(jnp.max(s, axis=-1, keepdims=True), sk)
    pr = jnp.exp(s - m)
    denom = jnp.sum(pr, axis=-1, keepdims=True) + jnp.exp(sk - m)
    pr = (pr / denom).astype(v.dtype)
    o = jnp.einsum('bnhgqk,bnkhd->bnqhgd', pr, vw)
    return o.reshape(B, S, H * D)


def setup_inputs(seed: int = 0) -> dict:
    key = jax.random.key(seed)
    ks = jax.random.split(key, 20)
    f32 = jnp.float32
    in_width = sum(in_proj_sizes())

    def nrm(k, shape, scale):
        return jax.random.normal(k, shape, f32) * scale

    def gain(k, shape):
        return 1.0 + 0.05 * jax.random.normal(k, shape, f32)

    return {
        "x": nrm(ks[0], (BATCH, SEQ, D_MODEL), 1.0),
        "p": nrm(ks[1], (DEPTH, BATCH, SEQ, PLE_DIM), 1.0),
        "norm_mix_pre": gain(ks[2], (DEPTH, D_MODEL)),
        "norm_mix_post": gain(ks[3], (DEPTH, D_MODEL)),
        "w_in": nrm(ks[4], (DEPTH, D_MODEL, in_width), D_MODEL ** -0.5),
        "lb_fwd": nrm(ks[5], (DEPTH + 1, HGRN_HEADS * HGRN_EXPAND), 0.1),
        "lb_bwd": nrm(ks[6], (DEPTH + 1, HGRN_HEADS * HGRN_EXPAND), 0.1),
        "hgrn_norm": gain(ks[7], (DEPTH, HGRN_VDIM)),
        "attn_sink": nrm(ks[8], (DEPTH, ATTN_HEADS), 0.5),
        "w_hgrn_proj": nrm(ks[9], (DEPTH, HGRN_WIDTH, D_MODEL), HGRN_WIDTH ** -0.5),
        "w_attn_proj": nrm(ks[10], (DEPTH, ATTN_WIDTH, D_MODEL), ATTN_WIDTH ** -0.5),
        "w_out": nrm(ks[11], (DEPTH, D_MODEL, D_MODEL), D_MODEL ** -0.5),
        "norm_mlp_pre": gain(ks[12], (DEPTH, D_MODEL)),
        "norm_mlp_post": gain(ks[13], (DEPTH, D_MODEL)),
        "w_mlp_up": nrm(ks[14], (DEPTH, D_MODEL, D_FF), D_MODEL ** -0.5),
        "w_mlp_down": nrm(ks[15], (DEPTH, D_FF, D_MODEL), D_FF ** -0.5),
        "w_ple": nrm(ks[16], (DEPTH, PLE_DIM, D_MODEL), PLE_DIM ** -0.5),
        "w_ple_gate": nrm(ks[17], (DEPTH, D_MODEL, D_MODEL), D_MODEL ** -0.5),
        "norm_ple": gain(ks[18], (DEPTH, D_MODEL)),
    }


def reference(x, p, norm_mix_pre, norm_mix_post, w_in, lb_fwd, lb_bwd, hgrn_norm, attn_sink,
              w_hgrn_proj, w_attn_proj, w_out, norm_mlp_pre, norm_mlp_post, w_mlp_up,
              w_mlp_down, w_ple, w_ple_gate, norm_ple):
    B, S, _ = x.shape
    positions = jnp.arange(S, dtype=jnp.float32)
    split_at = np.cumsum(in_proj_sizes())[:-1].tolist()
    for i in range(DEPTH):
        h = rms_norm(x, norm_mix_pre[i])
        proj = jnp.einsum('bsd,de->bse', h, w_in[i])
        hq, hf_f, hf_b, hi, hg, aq, ak, av, ga, gb = jnp.split(proj, split_at, axis=-1)

        q_h = jax.nn.silu(hq).reshape(B, S, HGRN_HEADS, HGRN_EXPAND) * (HGRN_EXPAND ** -0.5)
        v_h = hi.reshape(B, S, HGRN_HEADS, HGRN_VDIM)
        o_f = hgrn2_direction(q_h, hf_f, v_h, layer_lower_bound(lb_fwd, i))
        o_b = jnp.flip(hgrn2_direction(jnp.flip(q_h, 1), jnp.flip(hf_b, 1), jnp.flip(v_h, 1),
                                       layer_lower_bound(lb_bwd, i)), 1)
        o_h = (o_f + o_b).astype(x.dtype)
        o_h = rms_norm(o_h, hgrn_norm[i]) * jax.nn.silu(hg).reshape(B, S, HGRN_HEADS, HGRN_VDIM)
        y_a = jnp.einsum('bse,ed->bsd', o_h.reshape(B, S, HGRN_WIDTH), w_hgrn_proj[i])

        qa = rotary(aq.reshape(B, S, ATTN_HEADS, HEAD_DIM), positions)
        ka = rotary(ak.reshape(B, S, KV_HEADS, HEAD_DIM), positions)
        va = av.reshape(B, S, KV_HEADS, HEAD_DIM)
        o_a = windowed_gqa_sink(qa, ka, va, attn_sink[i])
        y_b = jnp.einsum('bse,ed->bsd', o_a, w_attn_proj[i])

        y = jax.nn.sigmoid(ga) * y_a + jax.nn.sigmoid(gb) * y_b
        mix = jnp.einsum('bsd,de->bse', y, w_out[i])
        x = x + rms_norm(mix, norm_mix_post[i])

        h2 = rms_norm(x, norm_mlp_pre[i])
        u = jnp.square(jax.nn.relu(jnp.einsum('bsd,df->bsf', h2, w_mlp_up[i])))
        d = jnp.einsum('bsf,fd->bsd', u, w_mlp_down[i])
        x = x + rms_norm(d, norm_mlp_post[i])

        e = jnp.einsum('bsp,pd->bsd', p[i].astype(x.dtype), w_ple[i])
        gate = jax.nn.sigmoid(jnp.einsum('bsd,de->bse', x, w_ple_gate[i]))
        x = x + rms_norm(e * gate, norm_ple[i])
    return x
```

```python
import functools
import math

import jax
import jax.numpy as jnp
from jax import lax
from jax.experimental import pallas as pl
from jax.experimental.pallas import tpu as pltpu

F32 = jnp.float32
BF16 = jnp.bfloat16

EPS = 1e-6
HEAD_DIM = 128
GQA_GROUP = 4
WINDOW = 128
ATTN_BLOCK = 128
CHUNK = 64
HGRN_ROWS = 256
ROPE_THETA = 10000.0
VMEM_LIMIT_BYTES = 56 * 1024 * 1024


def _params(*semantics):
    return pltpu.CompilerParams(dimension_semantics=semantics, vmem_limit_bytes=VMEM_LIMIT_BYTES)


def _pick(n, *cands):
    for c in cands:
        if n % c == 0:
            return c
    raise ValueError(f"no tile in {cands} divides {n}")


def _dot(a, b):
    return jnp.dot(a, b, preferred_element_type=F32)


def _dot_nt(a, b):
    return lax.dot_general(a, b, (((1,), (1,)), ((), ())), preferred_element_type=F32)


def _dot_tn(a, b):
    return lax.dot_general(a, b, (((0,), (0,)), ((), ())), preferred_element_type=F32)


def _sigmoid(x):
    return 1.0 / (1.0 + jnp.exp(-x))


def _rms(x, w):
    return x * lax.rsqrt(jnp.mean(x * x, axis=-1, keepdims=True) + EPS) * w


def _norm_cast_kernel(x_ref, w_ref, o_ref):
    o_ref[...] = _rms(x_ref[...], w_ref[...]).astype(o_ref.dtype)


def _norm_cast(x, w, tm):
    T, D = x.shape
    return pl.pallas_call(
        _norm_cast_kernel,
        grid=(T // tm,),
        in_specs=[pl.BlockSpec((tm, D), lambda i: (i, 0)), pl.BlockSpec((1, D), lambda i: (0, 0))],
        out_specs=pl.BlockSpec((tm, D), lambda i: (i, 0)),
        out_shape=jax.ShapeDtypeStruct((T, D), BF16),
        compiler_params=_params("parallel"),
        name="norm_cast",
    )(x, w)


def _resid_norm_kernel(x_ref, d_ref, w_ref, w2_ref, o_ref, h_ref, *, renorm):
    y = x_ref[...] + _rms(d_ref[...], w_ref[...])
    o_ref[...] = y
    h_ref[...] = (_rms(y, w2_ref[...]) if renorm else y).astype(h_ref.dtype)


def _resid_norm(x, d, w, w2, tm, renorm):
    T, D = x.shape
    row = pl.BlockSpec((tm, D), lambda i: (i, 0))
    vec = pl.BlockSpec((1, D), lambda i: (0, 0))
    return pl.pallas_call(
        functools.partial(_resid_norm_kernel, renorm=renorm),
        grid=(T // tm,),
        in_specs=[row, row, vec, vec],
        out_specs=[row, row],
        out_shape=[jax.ShapeDtypeStruct((T, D), F32), jax.ShapeDtypeStruct((T, D), BF16)],
        compiler_params=_params("parallel"),
        name="resid_norm",
    )(x, d, w, w2)


def _resid_final_kernel(x_ref, d_ref, w_ref, o_ref):
    o_ref[...] = x_ref[...] + _rms(d_ref[...], w_ref[...])


def _resid_final(x, d, w, tm):
    T, D = x.shape
    row = pl.BlockSpec((tm, D), lambda i: (i, 0))
    vec = pl.BlockSpec((1, D), lambda i: (0, 0))
    return pl.pallas_call(
        _resid_final_kernel,
        grid=(T // tm,),
        in_specs=[row, row, vec],
        out_specs=row,
        out_shape=jax.ShapeDtypeStruct((T, D), F32),
        compiler_params=_params("parallel"),
        name="resid_final",
    )(x, d, w)


def _mm_kernel(lhs_ref, w_ref, *rest, epilogue):
    *extra_refs, o_ref = rest
    acc = _dot(lhs_ref[...], w_ref[...])
    o_ref[...] = epilogue(acc, *extra_refs).astype(o_ref.dtype)


def _mm(lhs, w, col_off, n_cols, epilogue, out_dtype, tm, tn, extras=(), extra_specs=(), name="mm"):
    T, K = lhs.shape
    off = col_off // tn
    return pl.pallas_call(
        functools.partial(_mm_kernel, epilogue=epilogue),
        grid=(T // tm, n_cols // tn),
        in_specs=[pl.BlockSpec((tm, K), lambda i, j: (i, 0)),
                  pl.BlockSpec((K, tn), lambda i, j: (0, j + off))] + list(extra_specs),
        out_specs=pl.BlockSpec((tm, tn), lambda i, j: (i, j)),
        out_shape=jax.ShapeDtypeStruct((T, n_cols), out_dtype),
        compiler_params=_params("parallel", "arbitrary"),
        name=name,
    )(lhs, w, *extras)


def _epi_silu_scaled(acc, *, scale):
    return acc * _sigmoid(acc) * scale


def _epi_silu(acc):
    return acc * _sigmoid(acc)


def _epi_sigmoid(acc):
    return _sigmoid(acc)


def _epi_identity(acc):
    return acc


def _epi_log_forget(acc, lbp_ref, *, layer):
    lbp = lbp_ref[...]
    e = jnp.exp(lbp - jnp.max(lbp, axis=0, keepdims=True))
    lb = jnp.sum(e[:layer + 1], axis=0, keepdims=True) / jnp.sum(e, axis=0, keepdims=True)
    return jnp.log(lb + (1.0 - lb) * _sigmoid(acc))


def _epi_rotary(acc, cos_ref, sin_ref):
    cos, sin = cos_ref[...], sin_ref[...]
    heads = []
    for h in range(acc.shape[1] // HEAD_DIM):
        xh = acc[:, h * HEAD_DIM:(h + 1) * HEAD_DIM]
        heads.append(xh * cos + pltpu.roll(xh, HEAD_DIM // 2, 1) * sin)
    return jnp.concatenate(heads, axis=1)


def _hgrn_kernel(q_ref, gf_ref, gb_ref, v_ref, og_ref, nw_ref, o_ref,
                 accf_ref, accb_ref, mf_ref, mb_ref, sf_ref, sb_ref, *, seq):
    R = HGRN_ROWS
    nsb = seq // R
    ncs = R // CHUNK

    row = lax.broadcasted_iota(jnp.int32, (R, R), 0)
    col = lax.broadcasted_iota(jnp.int32, (R, R), 1)
    same = (row // CHUNK) == (col // CHUNK)
    mf_ref[...] = jnp.where(same & (col <= row), 1.0, 0.0).astype(BF16)
    mb_ref[...] = jnp.where(same & (col >= row), 1.0, 0.0).astype(BF16)
    sf_ref[...] = jnp.zeros_like(sf_ref)
    sb_ref[...] = jnp.zeros_like(sb_ref)

    def direction(r0, g_ref, m_ref, s_ref, acc_ref, reverse):
        g = g_ref[pl.ds(r0, R), :]
        q = q_ref[pl.ds(r0, R), :].astype(F32)
        v = v_ref[pl.ds(r0, R), :]
        m = m_ref[...]
        g1 = g.astype(BF16)
        r1 = g - g1.astype(F32)
        g2 = r1.astype(BF16)
        g3 = (r1 - g2.astype(F32)).astype(BF16)
        b = _dot(m, g1) + _dot(m, g2) + _dot(m, g3)
        k = 1.0 - jnp.exp(g)
        b3 = b.reshape(ncs, CHUNK, HEAD_DIM)
        edge = b3[:, 0:1, :] if reverse else b3[:, CHUNK - 1:CHUNK, :]
        b_tail = (edge - b3).reshape(R, HEAD_DIM)
        q_hat = (q * jnp.exp(b)).astype(BF16)
        k_hat = (k * jnp.exp(-b)).astype(BF16)
        k_tail = (k * jnp.exp(b_tail)).astype(BF16)
        decay = jnp.exp(edge)
        a = jnp.where(m > 0, _dot_nt(q_hat, k_hat), 0.0).astype(BF16)
        o = _dot(a, v)
        st = s_ref[...]
        for c in (range(ncs - 1, -1, -1) if reverse else range(ncs)):
            sl = slice(c * CHUNK, (c + 1) * CHUNK)
            o_inter = _dot_nt(q_hat[sl], st.astype(BF16))
            st = st * decay[c] + _dot_tn(v[sl], k_tail[sl])
            acc_ref[pl.ds(r0 + c * CHUNK, CHUNK), :] = o[sl] + o_inter
        s_ref[...] = st

    def body(sb, carry):
        rf = pl.multiple_of(sb * R, R)
        rb = pl.multiple_of((nsb - 1 - sb) * R, R)
        direction(rf, gf_ref, mf_ref, sf_ref, accf_ref, False)
        direction(rb, gb_ref, mb_ref, sb_ref, accb_ref, True)
        return carry

    lax.fori_loop(0, nsb, body, 0)

    def finish(t, carry):
        r0 = pl.multiple_of(t * R, R)
        o = accf_ref[pl.ds(r0, R), :] + accb_ref[pl.ds(r0, R), :]
        y = _rms(o, nw_ref[...]) * og_ref[pl.ds(r0, R), :].astype(F32)
        o_ref[pl.ds(r0, R), :] = y.astype(o_ref.dtype)
        return carry

    lax.fori_loop(0, nsb, finish, 0)


def _hgrn(q, g, v, og, nw, batch, seq):
    T, HW = q.shape
    H = HW // HEAD_DIM
    blk = lambda off: pl.BlockSpec((seq, HEAD_DIM), lambda b, h: (b, h + off))
    return pl.pallas_call(
        functools.partial(_hgrn_kernel, seq=seq),
        grid=(batch, H),
        in_specs=[blk(0), blk(0), blk(H), blk(0), blk(0), pl.BlockSpec((1, HEAD_DIM), lambda b, h: (0, 0))],
        out_specs=blk(0),
        out_shape=jax.ShapeDtypeStruct((T, HW), BF16),
        scratch_shapes=[pltpu.VMEM((seq, HEAD_DIM), F32), pltpu.VMEM((seq, HEAD_DIM), F32),
                        pltpu.VMEM((HGRN_ROWS, HGRN_ROWS), BF16), pltpu.VMEM((HGRN_ROWS, HGRN_ROWS), BF16),
                        pltpu.VMEM((HEAD_DIM, HEAD_DIM), F32), pltpu.VMEM((HEAD_DIM, HEAD_DIM), F32)],
        compiler_params=_params("parallel", "parallel"),
        name="hgrn2",
    )(q, g, g, v, og, nw)


def _attn_kernel(sink_ref, q_ref, kp_ref, kc_ref, kn_ref, vp_ref, vc_ref, vn_ref, o_ref, *, seq):
    G, D, BLK = GQA_GROUP, HEAD_DIM, ATTN_BLOCK
    kvh = pl.program_id(1)
    n = pl.program_id(2)
    k = jnp.concatenate([kp_ref[...], kc_ref[...], kn_ref[...]], axis=0)
    v = jnp.concatenate([vp_ref[...], vc_ref[...], vn_ref[...]], axis=0)
    q = q_ref[...]
    q4 = jnp.concatenate([q[:, h * D:(h + 1) * D] for h in range(G)], axis=0)
    s = _dot_nt(q4, k) * (D ** -0.5)
    r = lax.broadcasted_iota(jnp.int32, (G * BLK, 3 * BLK), 0) % BLK
    c = lax.broadcasted_iota(jnp.int32, (G * BLK, 3 * BLK), 1)
    kpos = n * BLK - BLK + c
    mask = (jnp.abs(c - BLK - r) <= WINDOW) & (kpos >= 0) & (kpos < seq)
    s = jnp.where(mask, s, -jnp.inf)
    sk = jnp.concatenate([jnp.full((BLK, 1), sink_ref[kvh * G + h], F32) for h in range(G)], axis=0)
    m = jnp.maximum(jnp.max(s, axis=-1, keepdims=True), sk)
    p = jnp.exp(s - m)
    denom = jnp.sum(p, axis=-1, keepdims=True) + jnp.exp(sk - m)
    o = _dot((p / denom).astype(BF16), v)
    o_ref[...] = jnp.concatenate([o[h * BLK:(h + 1) * BLK] for h in range(G)], axis=1).astype(o_ref.dtype)


def _attn(qk, v, sink, batch, seq, n_q_cols):
    T = qk.shape[0]
    G, D, BLK = GQA_GROUP, HEAD_DIM, ATTN_BLOCK
    kvh = v.shape[1] // D
    nb = seq // BLK
    k_off = n_q_cols // D
    qspec = pl.BlockSpec((BLK, G * D), lambda b, h, n: (b * nb + n, h))
    prev = lambda n: jnp.maximum(n - 1, 0)
    nxt = lambda n: jnp.minimum(n + 1, nb - 1)
    kspec = lambda f: pl.BlockSpec((BLK, D), lambda b, h, n: (b * nb + f(n), k_off + h))
    vspec = lambda f: pl.BlockSpec((BLK, D), lambda b, h, n: (b * nb + f(n), h))
    ident = lambda n: n
    return pl.pallas_call(
        functools.partial(_attn_kernel, seq=seq),
        grid=(batch, kvh, nb),
        in_specs=[pl.BlockSpec(memory_space=pltpu.SMEM), qspec,
                  kspec(prev), kspec(ident), kspec(nxt), vspec(prev), vspec(ident), vspec(nxt)],
        out_specs=qspec,
        out_shape=jax.ShapeDtypeStruct((T, n_q_cols), BF16),
        compiler_params=_params("parallel", "parallel", "arbitrary"),
        name="swa_sink",
    )(sink, qk, qk, qk, qk, v, v, v)


def _merge_kernel(oh_ref, oa_ref, wh_ref, wa_ref, ga_ref, gb_ref, o_ref):
    ya = _dot(oh_ref[...], wh_ref[...])
    yb = _dot(oa_ref[...], wa_ref[...])
    o_ref[...] = (ga_ref[...].astype(F32) * ya + gb_ref[...].astype(F32) * yb).astype(o_ref.dtype)


def _merge(oh, oa, wh, wa, gates, tm, tn):
    T, KH = oh.shape
    KA = oa.shape[1]
    D = wh.shape[1]
    nj = D // tn
    return pl.pallas_call(
        _merge_kernel,
        grid=(T // tm, nj),
        in_specs=[pl.BlockSpec((tm, KH), lambda i, j: (i, 0)), pl.BlockSpec((tm, KA), lambda i, j: (i, 0)),
                  pl.BlockSpec((KH, tn), lambda i, j: (0, j)), pl.BlockSpec((KA, tn), lambda i, j: (0, j)),
                  pl.BlockSpec((tm, tn), lambda i, j: (i, j)), pl.BlockSpec((tm, tn), lambda i, j: (i, j + nj))],
        out_specs=pl.BlockSpec((tm, tn), lambda i, j: (i, j)),
        out_shape=jax.ShapeDtypeStruct((T, D), BF16),
        compiler_params=_params("parallel", "arbitrary"),
        name="gated_merge",
    )(oh, oa, wh, wa, gates, gates)


def _mlp_kernel(h_ref, wu_ref, wd_ref, o_ref):
    j = pl.program_id(1)
    u = jnp.square(jnp.maximum(_dot(h_ref[...], wu_ref[...]), 0.0)).astype(BF16)
    d = _dot(u, wd_ref[...])

    @pl.when(j == 0)
    def _():
        o_ref[...] = d

    @pl.when(j > 0)
    def _():
        o_ref[...] += d


def _mlp(h, wu, wd, tm, tf):
    T, D = h.shape
    FF = wu.shape[1]
    return pl.pallas_call(
        _mlp_kernel,
        grid=(T // tm, FF // tf),
        in_specs=[pl.BlockSpec((tm, D), lambda i, j: (i, 0)),
                  pl.BlockSpec((D, tf), lambda i, j: (0, j)),
                  pl.BlockSpec((tf, D), lambda i, j: (j, 0))],
        out_specs=pl.BlockSpec((tm, D), lambda i, j: (i, 0)),
        out_shape=jax.ShapeDtypeStruct((T, D), F32),
        compiler_params=_params("parallel", "arbitrary"),
        name="relu2_mlp",
    )(h, wu, wd)


def _ple_kernel(x_ref, p_ref, wg_ref, wp_ref, o_ref):
    e = _dot(p_ref[...].astype(BF16), wp_ref[...])
    o_ref[...] = e * _sigmoid(_dot(x_ref[...], wg_ref[...]))


def _ple(xb, p, wg, wp, tm, tn):
    T, D = xb.shape
    P = p.shape[1]
    return pl.pallas_call(
        _ple_kernel,
        grid=(T // tm, D // tn),
        in_specs=[pl.BlockSpec((tm, D), lambda i, j: (i, 0)), pl.BlockSpec((tm, P), lambda i, j: (i, 0)),
                  pl.BlockSpec((D, tn), lambda i, j: (0, j)), pl.BlockSpec((P, tn), lambda i, j: (0, j))],
        out_specs=pl.BlockSpec((tm, tn), lambda i, j: (i, j)),
        out_shape=jax.ShapeDtypeStruct((T, D), F32),
        compiler_params=_params("parallel", "arbitrary"),
        name="ple_gate",
    )(xb, p, wg, wp)


def _rope_tables(seq):
    half = HEAD_DIM // 2
    inv_freq = ROPE_THETA ** (-jnp.arange(0, HEAD_DIM, 2, dtype=F32) / HEAD_DIM)
    ang = jnp.arange(seq, dtype=F32)[:, None] * inv_freq[None, :]
    cos, sin = jnp.cos(ang), jnp.sin(ang)
    assert cos.shape == (seq, half)
    return jnp.concatenate([cos, cos], axis=1), jnp.concatenate([-sin, sin], axis=1)


def kernel(x, p, norm_mix_pre, norm_mix_post, w_in, lb_fwd, lb_bwd, hgrn_norm, attn_sink, w_hgrn_proj,
           w_attn_proj, w_out, norm_mlp_pre, norm_mlp_post, w_mlp_up, w_mlp_down, w_ple, w_ple_gate, norm_ple):
    B, S, D = x.shape
    T = B * S
    depth = w_in.shape[0]
    HW = w_hgrn_proj.shape[1]
    AW = w_attn_proj.shape[1]
    KVW = AW // GQA_GROUP
    assert S % HGRN_ROWS == 0 and S % ATTN_BLOCK == 0 and hgrn_norm.shape[-1] == HEAD_DIM
    assert w_in.shape[2] == 5 * HW + AW + 2 * KVW + 2 * D

    tm = _pick(T, 512, 256, 128)
    tr = _pick(T, 256, 128)
    cos_t, sin_t = _rope_tables(S)
    vec = lambda a: a.reshape(1, -1)

    xf = x.reshape(T, D)
    for i in range(depth):
        w_in_b = w_in[i].astype(BF16)
        off_q, off_f, off_i, off_g = 0, HW, 3 * HW, 4 * HW
        off_aq, off_av, off_gate = 5 * HW, 5 * HW + AW + KVW, 5 * HW + AW + 2 * KVW

        h = _norm_cast(xf, vec(norm_mix_pre[i]), tr)
        tn = lambda off, n: _pick(math.gcd(off, n), 512, 256, 128)
        mm = functools.partial(_mm, h, w_in_b, tm=tm)
        hq = mm(off_q, HW, functools.partial(_epi_silu_scaled, scale=HEAD_DIM ** -0.5), BF16,
                tn=tn(off_q, HW), name="in_hq")
        lbp = jnp.concatenate([lb_fwd, lb_bwd], axis=1)
        tn_f = tn(off_f, 2 * HW)
        log_f = mm(off_f, 2 * HW, functools.partial(_epi_log_forget, layer=i), F32, tn=tn_f,
                   extras=(lbp,), extra_specs=(pl.BlockSpec((lbp.shape[0], tn_f), lambda r, j: (0, j)),),
                   name="in_logf")
        hv = mm(off_i, HW, _epi_identity, BF16, tn=tn(off_i, HW), name="in_hv")
        hg = mm(off_g, HW, _epi_silu, BF16, tn=tn(off_g, HW), name="in_hg")
        tn_r = tn(off_aq, AW + KVW)
        nrb = S // tm if S % tm == 0 else None
        assert nrb is not None
        rope_spec = pl.BlockSpec((tm, HEAD_DIM), lambda r, j: (r % nrb, 0))
        qk = mm(off_aq, AW + KVW, _epi_rotary, BF16, tn=tn_r, extras=(cos_t, sin_t),
                extra_specs=(rope_spec, rope_spec), name="in_qk")
        av = mm(off_av, KVW, _epi_identity, BF16, tn=tn(off_av, KVW), name="in_v")
        gates = mm(off_gate, 2 * D, _epi_sigmoid, BF16, tn=tn(off_gate, 2 * D), name="in_gates")

        o_h = _hgrn(hq, log_f, hv, hg, vec(hgrn_norm[i]), B, S)
        o_a = _attn(qk, av, attn_sink[i], B, S, AW)
        y = _merge(o_h, o_a, w_hgrn_proj[i].astype(BF16), w_attn_proj[i].astype(BF16), gates,
                   tm, _pick(D, 512, 256, 128))
        mix = _mm(y, w_out[i].astype(BF16), 0, D, _epi_identity, F32, tm, _pick(D, 512, 256, 128), name="w_out")
        xf, h2 = _resid_norm(xf, mix, vec(norm_mix_post[i]), vec(norm_mlp_pre[i]), tr, renorm=True)

        d = _mlp(h2, w_mlp_up[i].astype(BF16), w_mlp_down[i].astype(BF16), tm, _pick(w_mlp_up.shape[2], 512, 256))
        xf, xb = _resid_norm(xf, d, vec(norm_mlp_post[i]), vec(norm_mlp_post[i]), tr, renorm=False)

        eg = _ple(xb, p[i].reshape(T, -1), w_ple_gate[i].astype(BF16), w_ple[i].astype(BF16),
                  tm, _pick(D, 512, 256, 128))
        xf = _resid_final(xf, eg, vec(norm_ple[i]), tr)
    return xf.reshape(B, S, D)
```

```python
import functools
import math

import jax
import jax.numpy as jnp
from jax import lax
from jax.experimental import pallas as pl
from jax.experimental.pallas import tpu as pltpu

F32 = jnp.float32
BF16 = jnp.bfloat16

EPS = 1e-6
HEAD_DIM = 128
GQA_GROUP = 4
WINDOW = 128
ATTN_BLOCK = 128
CHUNK = 64
HGRN_ROWS = 256
ROPE_THETA = 10000.0
VMEM_LIMIT_BYTES = 56 * 1024 * 1024


def _params(*semantics):
    return pltpu.CompilerParams(dimension_semantics=semantics, vmem_limit_bytes=VMEM_LIMIT_BYTES)


def _pick(n, *cands):
    for c in cands:
        if n % c == 0:
            return c
    raise ValueError(f"no tile in {cands} divides {n}")


def _dot(a, b):
    return jnp.dot(a, b, preferred_element_type=F32)


def _dot_nt(a, b):
    return lax.dot_general(a, b, (((1,), (1,)), ((), ())), preferred_element_type=F32)


def _dot_tn(a, b):
    return lax.dot_general(a, b, (((0,), (0,)), ((), ())), preferred_element_type=F32)


def _sigmoid(x):
    return 1.0 / (1.0 + jnp.exp(-x))


def _rms(x, w):
    return x * lax.rsqrt(jnp.mean(x * x, axis=-1, keepdims=True) + EPS) * w


def _norm_cast_kernel(x_ref, w_ref, o_ref):
    o_ref[...] = _rms(x_ref[...], w_ref[...]).astype(o_ref.dtype)


def _norm_cast(x, w, tm):
    T, D = x.shape
    return pl.pallas_call(
        _norm_cast_kernel,
        grid=(T // tm,),
        in_specs=[pl.BlockSpec((tm, D), lambda i: (i, 0)), pl.BlockSpec((1, D), lambda i: (0, 0))],
        out_specs=pl.BlockSpec((tm, D), lambda i: (i, 0)),
        out_shape=jax.ShapeDtypeStruct((T, D), BF16),
        compiler_params=_params("parallel"),
        name="norm_cast",
    )(x, w)


def _resid_norm_kernel(x_ref, d_ref, w_ref, w2_ref, o_ref, h_ref, *, renorm):
    y = x_ref[...] + _rms(d_ref[...], w_ref[...])
    o_ref[...] = y
    h_ref[...] = (_rms(y, w2_ref[...]) if renorm else y).astype(h_ref.dtype)


def _resid_norm(x, d, w, w2, tm, renorm):
    T, D = x.shape
    row = pl.BlockSpec((tm, D), lambda i: (i, 0))
    vec = pl.BlockSpec((1, D), lambda i: (0, 0))
    return pl.pallas_call(
        functools.partial(_resid_norm_kernel, renorm=renorm),
        grid=(T // tm,),
        in_specs=[row, row, vec, vec],
        out_specs=[row, row],
        out_shape=[jax.ShapeDtypeStruct((T, D), F32), jax.ShapeDtypeStruct((T, D), BF16)],
        compiler_params=_params("parallel"),
        name="resid_norm",
    )(x, d, w, w2)


def _resid_final_kernel(x_ref, d_ref, w_ref, o_ref):
    o_ref[...] = x_ref[...] + _rms(d_ref[...], w_ref[...])


def _resid_final(x, d, w, tm):
    T, D = x.shape
    row = pl.BlockSpec((tm, D), lambda i: (i, 0))
    vec = pl.BlockSpec((1, D), lambda i: (0, 0))
    return pl.pallas_call(
        _resid_final_kernel,
        grid=(T // tm,),
        in_specs=[row, row, vec],
        out_specs=row,
        out_shape=jax.ShapeDtypeStruct((T, D), F32),
        compiler_params=_params("parallel"),
        name="resid_final",
    )(x, d, w)


def _mm_kernel(lhs_ref, w_ref, *rest, epilogue):
    *extra_refs, o_ref = rest
    acc = _dot(lhs_ref[...], w_ref[...])
    o_ref[...] = epilogue(acc, *extra_refs).astype(o_ref.dtype)


def _mm(lhs, w, col_off, n_cols, epilogue, out_dtype, tm, tn, extras=(), extra_specs=(), name="mm"):
    T, K = lhs.shape
    off = col_off // tn
    return pl.pallas_call(
        functools.partial(_mm_kernel, epilogue=epilogue),
        grid=(T // tm, n_cols // tn),
        in_specs=[pl.BlockSpec((tm, K), lambda i, j: (i, 0)),
                  pl.BlockSpec((K, tn), lambda i, j: (0, j + off))] + list(extra_specs),
        out_specs=pl.BlockSpec((tm, tn), lambda i, j: (i, j)),
        out_shape=jax.ShapeDtypeStruct((T, n_cols), out_dtype),
        compiler_params=_params("parallel", "arbitrary"),
        name=name,
    )(lhs, w, *extras)


def _epi_silu_scaled(acc, *, scale):
    return acc * _sigmoid(acc) * scale


def _epi_silu(acc):
    return acc * _sigmoid(acc)


def _epi_sigmoid(acc):
    return _sigmoid(acc)


def _epi_identity(acc):
    return acc


def _epi_log_forget(acc, lbp_ref, *, layer):
    lbp = lbp_ref[...]
    e = jnp.exp(lbp - jnp.max(lbp, axis=0, keepdims=True))
    lb = jnp.sum(e[:layer + 1], axis=0, keepdims=True) / jnp.sum(e, axis=0, keepdims=True)
    return jnp.log(lb + (1.0 - lb) * _sigmoid(acc))


def _epi_rotary(acc, cos_ref, sin_ref):
    cos, sin = cos_ref[...], sin_ref[...]
    heads = []
    for h in range(acc.shape[1] // HEAD_DIM):
        xh = acc[:, h * HEAD_DIM:(h + 1) * HEAD_DIM]
        heads.append(xh * cos + pltpu.roll(xh, HEAD_DIM // 2, 1) * sin)
    return jnp.concatenate(heads, axis=1)


def _hgrn_kernel(q_ref, gf_ref, gb_ref, v_ref, og_ref, nw_ref, o_ref,
                 accf_ref, accb_ref, mf_ref, mb_ref, sf_ref, sb_ref, *, seq):
    R = HGRN_ROWS
    nsb = seq // R
    ncs = R // CHUNK

    row = lax.broadcasted_iota(jnp.int32, (R, R), 0)
    col = lax.broadcasted_iota(jnp.int32, (R, R), 1)
    same = (row // CHUNK) == (col // CHUNK)
    mf_ref[...] = jnp.where(same & (col <= row), 1.0, 0.0).astype(BF16)
    mb_ref[...] = jnp.where(same & (col >= row), 1.0, 0.0).astype(BF16)
    sf_ref[...] = jnp.zeros_like(sf_ref)
    sb_ref[...] = jnp.zeros_like(sb_ref)

    def direction(r0, g_ref, m_ref, s_ref, acc_ref, reverse):
        g = g_ref[pl.ds(r0, R), :]
        q = q_ref[pl.ds(r0, R), :].astype(F32)
        v = v_ref[pl.ds(r0, R), :]
        m = m_ref[...]
        g1 = g.astype(BF16)
        r1 = g - g1.astype(F32)
        g2 = r1.astype(BF16)
        g3 = (r1 - g2.astype(F32)).astype(BF16)
        b = _dot(m, g1) + _dot(m, g2) + _dot(m, g3)
        k = 1.0 - jnp.exp(g)
        b3 = b.reshape(ncs, CHUNK, HEAD_DIM)
        edge = b3[:, 0:1, :] if reverse else b3[:, CHUNK - 1:CHUNK, :]
        b_tail = (edge - b3).reshape(R, HEAD_DIM)
        q_hat = (q * jnp.exp(b)).astype(BF16)
        k_hat = (k * jnp.exp(-b)).astype(BF16)
        k_tail = (k * jnp.exp(b_tail)).astype(BF16)
        decay = jnp.exp(edge)
        a = jnp.where(m > 0, _dot_nt(q_hat, k_hat), 0.0).astype(BF16)
        o = _dot(a, v)
        st = s_ref[...]
        for c in (range(ncs - 1, -1, -1) if reverse else range(ncs)):
            sl = slice(c * CHUNK, (c + 1) * CHUNK)
            o_inter = _dot_nt(q_hat[sl], st.astype(BF16))
            st = st * decay[c] + _dot_tn(v[sl], k_tail[sl])
            acc_ref[pl.ds(r0 + c * CHUNK, CHUNK), :] = o[sl] + o_inter
        s_ref[...] = st

    def body(sb, carry):
        rf = pl.multiple_of(sb * R, R)
        rb = pl.multiple_of((nsb - 1 - sb) * R, R)
        direction(rf, gf_ref, mf_ref, sf_ref, accf_ref, False)
        direction(rb, gb_ref, mb_ref, sb_ref, accb_ref, True)
        return carry

    lax.fori_loop(0, nsb, body, 0)

    def finish(t, carry):
        r0 = pl.multiple_of(t * R, R)
        o = accf_ref[pl.ds(r0, R), :] + accb_ref[pl.ds(r0, R), :]
        y = _rms(o, nw_ref[...]) * og_ref[pl.ds(r0, R), :].astype(F32)
        o_ref[pl.ds(r0, R), :] = y.astype(o_ref.dtype)
        return carry

    lax.fori_loop(0, nsb, finish, 0)


def _hgrn(q, g, v, og, nw, batch, seq):
    T, HW = q.shape
    H = HW // HEAD_DIM
    blk = lambda off: pl.BlockSpec((seq, HEAD_DIM), lambda b, h: (b, h + off))
    return pl.pallas_call(
        functools.partial(_hgrn_kernel, seq=seq),
        grid=(batch, H),
        in_specs=[blk(0), blk(0), blk(H), blk(0), blk(0), pl.BlockSpec((1, HEAD_DIM), lambda b, h: (0, 0))],
        out_specs=blk(0),
        out_shape=jax.ShapeDtypeStruct((T, HW), BF16),
        scratch_shapes=[pltpu.VMEM((seq, HEAD_DIM), F32), pltpu.VMEM((seq, HEAD_DIM), F32),
                        pltpu.VMEM((HGRN_ROWS, HGRN_ROWS), BF16), pltpu.VMEM((HGRN_ROWS, HGRN_ROWS), BF16),
                        pltpu.VMEM((HEAD_DIM, HEAD_DIM), F32), pltpu.VMEM((HEAD_DIM, HEAD_DIM), F32)],
        compiler_params=_params("parallel", "parallel"),
        name="hgrn2",
    )(q, g, g, v, og, nw)


def _attn_kernel(sink_ref, q_ref, kp_ref, kc_ref, kn_ref, vp_ref, vc_ref, vn_ref, o_ref, *, seq):
    G, D, BLK = GQA_GROUP, HEAD_DIM, ATTN_BLOCK
    kvh = pl.program_id(1)
    n = pl.program_id(2)
    k = jnp.concatenate([kp_ref[...], kc_ref[...], kn_ref[...]], axis=0)
    v = jnp.concatenate([vp_ref[...], vc_ref[...], vn_ref[...]], axis=0)
    q = q_ref[...]
    q4 = jnp.concatenate([q[:, h * D:(h + 1) * D] for h in range(G)], axis=0)
    s = _dot_nt(q4, k) * (D ** -0.5)
    r = lax.broadcasted_iota(jnp.int32, (G * BLK, 3 * BLK), 0) % BLK
    c = lax.broadcasted_iota(jnp.int32, (G * BLK, 3 * BLK), 1)
    kpos = n * BLK - BLK + c
    mask = (jnp.abs(c - BLK - r) <= WINDOW) & (kpos >= 0) & (kpos < seq)
    s = jnp.where(mask, s, -jnp.inf)
    sk = jnp.concatenate([jnp.full((BLK, 1), sink_ref[kvh * G + h], F32) for h in range(G)], axis=0)
    m = jnp.maximum(jnp.max(s, axis=-1, keepdims=True), sk)
    p = jnp.exp(s - m)
    denom = jnp.sum(p, axis=-1, keepdims=True) + jnp.exp(sk - m)
    o = _dot((p / denom).astype(BF16), v)
    o_ref[...] = jnp.concatenate([o[h * BLK:(h + 1) * BLK] for h in range(G)], axis=1).astype(o_ref.dtype)


def _attn(qk, v, sink, batch, seq, n_q_cols):
    T = qk.shape[0]
    G, D, BLK = GQA_GROUP, HEAD_DIM, ATTN_BLOCK
    kvh = v.shape[1] // D
    nb = seq // BLK
    k_off = n_q_cols // D
    qspec = pl.BlockSpec((BLK, G * D), lambda b, h, n: (b * nb + n, h))
    prev = lambda n: jnp.maximum(n - 1, 0)
    nxt = lambda n: jnp.minimum(n + 1, nb - 1)
    kspec = lambda f: pl.BlockSpec((BLK, D), lambda b, h, n: (b * nb + f(n), k_off + h))
    vspec = lambda f: pl.BlockSpec((BLK, D), lambda b, h, n: (b * nb + f(n), h))
    ident = lambda n: n
    return pl.pallas_call(
        functools.partial(_attn_kernel, seq=seq),
        grid=(batch, kvh, nb),
        in_specs=[pl.BlockSpec(memory_space=pltpu.SMEM), qspec,
                  kspec(prev), kspec(ident), kspec(nxt), vspec(prev), vspec(ident), vspec(nxt)],
        out_specs=qspec,
        out_shape=jax.ShapeDtypeStruct((T, n_q_cols), BF16),
        compiler_params=_params("parallel", "parallel", "arbitrary"),
        name="swa_sink",
    )(sink, qk, qk, qk, qk, v, v, v)


def _merge_kernel(oh_ref, oa_ref, wh_ref, wa_ref, ga_ref, gb_ref, o_ref):
    ya = _dot(oh_ref[...], wh_ref[...])
    yb = _dot(oa_ref[...], wa_ref[...])
    o_ref[...] = (ga_ref[...].astype(F32) * ya + gb_ref[...].astype(F32) * yb).astype(o_ref.dtype)


def _merge(oh, oa, wh, wa, gates, tm, tn):
    T, KH = oh.shape
    KA = oa.shape[1]
    D = wh.shape[1]
    nj = D // tn
    return pl.pallas_call(
        _merge_kernel,
        grid=(T // tm, nj),
        in_specs=[pl.BlockSpec((tm, KH), lambda i, j: (i, 0)), pl.BlockSpec((tm, KA), lambda i, j: (i, 0)),
                  pl.BlockSpec((KH, tn), lambda i, j: (0, j)), pl.BlockSpec((KA, tn), lambda i, j: (0, j)),
                  pl.BlockSpec((tm, tn), lambda i, j: (i, j)), pl.BlockSpec((tm, tn), lambda i, j: (i, j + nj))],
        out_specs=pl.BlockSpec((tm, tn), lambda i, j: (i, j)),
        out_shape=jax.ShapeDtypeStruct((T, D), BF16),
        compiler_params=_params("parallel", "arbitrary"),
        name="gated_merge",
    )(oh, oa, wh, wa, gates, gates)


def _mlp_kernel(h_ref, wu_ref, wd_ref, o_ref, u_ref, *, n_up, tf):
    j = pl.program_id(1)

    @pl.when(j < n_up)
    def _():
        u = jnp.square(jnp.maximum(_dot(h_ref[...], wu_ref[...]), 0.0))
        u_ref[:, pl.ds(pl.multiple_of(j * tf, tf), tf)] = u.astype(u_ref.dtype)

    @pl.when(j >= n_up)
    def _():
        o_ref[...] = _dot(u_ref[...], wd_ref[...])


def _mlp(h, wu, wd, tm, tf, tn):
    T, D = h.shape
    FF = wu.shape[1]
    n_up = FF // tf
    return pl.pallas_call(
        functools.partial(_mlp_kernel, n_up=n_up, tf=tf),
        grid=(T // tm, n_up + D // tn),
        in_specs=[pl.BlockSpec((tm, D), lambda i, j: (i, 0)),
                  pl.BlockSpec((D, tf), lambda i, j: (0, jnp.minimum(j, n_up - 1))),
                  pl.BlockSpec((FF, tn), lambda i, j: (0, jnp.maximum(j - n_up, 0)))],
        out_specs=pl.BlockSpec((tm, tn), lambda i, j: (i, jnp.maximum(j - n_up, 0))),
        out_shape=jax.ShapeDtypeStruct((T, D), F32),
        scratch_shapes=[pltpu.VMEM((tm, FF), BF16)],
        compiler_params=_params("parallel", "arbitrary"),
        name="relu2_mlp",
    )(h, wu, wd)


def _ple_kernel(x_ref, p_ref, wg_ref, wp_ref, o_ref):
    e = _dot(p_ref[...].astype(BF16), wp_ref[...])
    o_ref[...] = e * _sigmoid(_dot(x_ref[...], wg_ref[...]))


def _ple(xb, p, wg, wp, tm, tn):
    T, D = xb.shape
    P = p.shape[1]
    return pl.pallas_call(
        _ple_kernel,
        grid=(T // tm, D // tn),
        in_specs=[pl.BlockSpec((tm, D), lambda i, j: (i, 0)), pl.BlockSpec((tm, P), lambda i, j: (i, 0)),
                  pl.BlockSpec((D, tn), lambda i, j: (0, j)), pl.BlockSpec((P, tn), lambda i, j: (0, j))],
        out_specs=pl.BlockSpec((tm, tn), lambda i, j: (i, j)),
        out_shape=jax.ShapeDtypeStruct((T, D), F32),
        compiler_params=_params("parallel", "arbitrary"),
        name="ple_gate",
    )(xb, p, wg, wp)


def _rope_tables(seq):
    half = HEAD_DIM // 2
    inv_freq = ROPE_THETA ** (-jnp.arange(0, HEAD_DIM, 2, dtype=F32) / HEAD_DIM)
    ang = jnp.arange(seq, dtype=F32)[:, None] * inv_freq[None, :]
    cos, sin = jnp.cos(ang), jnp.sin(ang)
    assert cos.shape == (seq, half)
    return jnp.concatenate([cos, cos], axis=1), jnp.concatenate([-sin, sin], axis=1)


def kernel(x, p, norm_mix_pre, norm_mix_post, w_in, lb_fwd, lb_bwd, hgrn_norm, attn_sink, w_hgrn_proj,
           w_attn_proj, w_out, norm_mlp_pre, norm_mlp_post, w_mlp_up, w_mlp_down, w_ple, w_ple_gate, norm_ple):
    B, S, D = x.shape
    T = B * S
    depth = w_in.shape[0]
    HW = w_hgrn_proj.shape[1]
    AW = w_attn_proj.shape[1]
    KVW = AW // GQA_GROUP
    assert S % HGRN_ROWS == 0 and S % ATTN_BLOCK == 0 and hgrn_norm.shape[-1] == HEAD_DIM
    assert w_in.shape[2] == 5 * HW + AW + 2 * KVW + 2 * D

    tm = _pick(S, 1024, 512, 256, 128)
    tm_mlp = _pick(T, 512, 256, 128)
    tr = _pick(T, 256, 128)
    cos_t, sin_t = _rope_tables(S)
    vec = lambda a: a.reshape(1, -1)

    xf = x.reshape(T, D)
    for i in range(depth):
        w_in_b = w_in[i].astype(BF16)
        off_q, off_f, off_i, off_g = 0, HW, 3 * HW, 4 * HW
        off_aq, off_av, off_gate = 5 * HW, 5 * HW + AW + KVW, 5 * HW + AW + 2 * KVW

        h = _norm_cast(xf, vec(norm_mix_pre[i]), tr)
        tn = lambda off, n: _pick(math.gcd(off, n), 512, 256, 128)
        mm = functools.partial(_mm, h, w_in_b, tm=tm)
        hq = mm(off_q, HW, functools.partial(_epi_silu_scaled, scale=HEAD_DIM ** -0.5), BF16,
                tn=tn(off_q, HW), name="in_hq")
        lbp = jnp.concatenate([lb_fwd, lb_bwd], axis=1)
        tn_f = tn(off_f, 2 * HW)
        log_f = mm(off_f, 2 * HW, functools.partial(_epi_log_forget, layer=i), F32, tn=tn_f,
                   extras=(lbp,), extra_specs=(pl.BlockSpec((lbp.shape[0], tn_f), lambda r, j: (0, j)),),
                   name="in_logf")
        hv = mm(off_i, HW, _epi_identity, BF16, tn=tn(off_i, HW), name="in_hv")
        hg = mm(off_g, HW, _epi_silu, BF16, tn=tn(off_g, HW), name="in_hg")
        tn_r = tn(off_aq, AW + KVW)
        nrb = S // tm if S % tm == 0 else None
        assert nrb is not None
        rope_spec = pl.BlockSpec((tm, HEAD_DIM), lambda r, j: (r % nrb, 0))
        qk = mm(off_aq, AW + KVW, _epi_rotary, BF16, tn=tn_r, extras=(cos_t, sin_t),
                extra_specs=(rope_spec, rope_spec), name="in_qk")
        av = mm(off_av, KVW, _epi_identity, BF16, tn=tn(off_av, KVW), name="in_v")
        gates = mm(off_gate, 2 * D, _epi_sigmoid, BF16, tn=tn(off_gate, 2 * D), name="in_gates")

        o_h = _hgrn(hq, log_f, hv, hg, vec(hgrn_norm[i]), B, S)
        o_a = _attn(qk, av, attn_sink[i], B, S, AW)
        y = _merge(o_h, o_a, w_hgrn_proj[i].astype(BF16), w_attn_proj[i].astype(BF16), gates,
                   tm, _pick(D, 512, 256, 128))
        mix = _mm(y, w_out[i].astype(BF16), 0, D, _epi_identity, F32, tm, _pick(D, 512, 256, 128), name="w_out")
        xf, h2 = _resid_norm(xf, mix, vec(norm_mix_post[i]), vec(norm_mlp_pre[i]), tr, renorm=True)

        d = _mlp(h2, w_mlp_up[i].astype(BF16), w_mlp_down[i].astype(BF16), tm_mlp,
                 _pick(w_mlp_up.shape[2], 512, 256), _pick(D, 256, 128))
        xf, xb = _resid_norm(xf, d, vec(norm_mlp_post[i]), vec(norm_mlp_post[i]), tr, renorm=False)

        eg = _ple(xb, p[i].reshape(T, -1), w_ple_gate[i].astype(BF16), w_ple[i].astype(BF16),
                  tm, _pick(D, 512, 256, 128))
        xf = _resid_final(xf, eg, vec(norm_ple[i]), tr)
    return xf.reshape(B, S, D)
```

```python
import functools
import math

import jax
import jax.numpy as jnp
from jax import lax
from jax.experimental import pallas as pl
from jax.experimental.pallas import tpu as pltpu

F32 = jnp.float32
BF16 = jnp.bfloat16

EPS = 1e-6
HEAD_DIM = 128
GQA_GROUP = 4
WINDOW = 128
ATTN_BLOCK = 128
CHUNK = 64
HGRN_ROWS = 256
HGRN_FINISH_ROWS = 1024
ROPE_THETA = 10000.0
VMEM_LIMIT_BYTES = 56 * 1024 * 1024


def _params(*semantics):
    return pltpu.CompilerParams(dimension_semantics=semantics, vmem_limit_bytes=VMEM_LIMIT_BYTES)


def _pick(n, *cands):
    for c in cands:
        if n % c == 0:
            return c
    raise ValueError(f"no tile in {cands} divides {n}")


def _dot(a, b):
    return jnp.dot(a, b, preferred_element_type=F32)


def _dot_nt(a, b):
    return lax.dot_general(a, b, (((1,), (1,)), ((), ())), preferred_element_type=F32)


def _dot_tn(a, b):
    return lax.dot_general(a, b, (((0,), (0,)), ((), ())), preferred_element_type=F32)


def _sigmoid(x):
    return 1.0 / (1.0 + jnp.exp(-x))


def _rms(x, w):
    return x * lax.rsqrt(jnp.mean(x * x, axis=-1, keepdims=True) + EPS) * w


def _norm_cast_kernel(x_ref, w_ref, o_ref):
    o_ref[...] = _rms(x_ref[...], w_ref[...]).astype(o_ref.dtype)


def _norm_cast(x, w, tm):
    T, D = x.shape
    return pl.pallas_call(
        _norm_cast_kernel,
        grid=(T // tm,),
        in_specs=[pl.BlockSpec((tm, D), lambda i: (i, 0)), pl.BlockSpec((1, D), lambda i: (0, 0))],
        out_specs=pl.BlockSpec((tm, D), lambda i: (i, 0)),
        out_shape=jax.ShapeDtypeStruct((T, D), BF16),
        compiler_params=_params("parallel"),
        name="norm_cast",
    )(x, w)


def _resid_norm_kernel(x_ref, d_ref, w_ref, w2_ref, o_ref, h_ref, *, renorm):
    y = x_ref[...] + _rms(d_ref[...], w_ref[...])
    o_ref[...] = y
    h_ref[...] = (_rms(y, w2_ref[...]) if renorm else y).astype(h_ref.dtype)


def _resid_norm(x, d, w, w2, tm, renorm):
    T, D = x.shape
    row = pl.BlockSpec((tm, D), lambda i: (i, 0))
    vec = pl.BlockSpec((1, D), lambda i: (0, 0))
    return pl.pallas_call(
        functools.partial(_resid_norm_kernel, renorm=renorm),
        grid=(T // tm,),
        in_specs=[row, row, vec, vec],
        out_specs=[row, row],
        out_shape=[jax.ShapeDtypeStruct((T, D), F32), jax.ShapeDtypeStruct((T, D), BF16)],
        compiler_params=_params("parallel"),
        name="resid_norm",
    )(x, d, w, w2)


def _resid_final_kernel(x_ref, d_ref, w_ref, o_ref):
    o_ref[...] = x_ref[...] + _rms(d_ref[...], w_ref[...])


def _resid_final(x, d, w, tm):
    T, D = x.shape
    row = pl.BlockSpec((tm, D), lambda i: (i, 0))
    vec = pl.BlockSpec((1, D), lambda i: (0, 0))
    return pl.pallas_call(
        _resid_final_kernel,
        grid=(T // tm,),
        in_specs=[row, row, vec],
        out_specs=row,
        out_shape=jax.ShapeDtypeStruct((T, D), F32),
        compiler_params=_params("parallel"),
        name="resid_final",
    )(x, d, w)


def _mm_kernel(lhs_ref, w_ref, *rest, epilogue):
    *extra_refs, o_ref = rest
    acc = _dot(lhs_ref[...], w_ref[...])
    o_ref[...] = epilogue(acc, *extra_refs).astype(o_ref.dtype)


def _mm(lhs, w, col_off, n_cols, epilogue, out_dtype, tm, tn, extras=(), extra_specs=(), name="mm"):
    T, K = lhs.shape
    off = col_off // tn
    return pl.pallas_call(
        functools.partial(_mm_kernel, epilogue=epilogue),
        grid=(T // tm, n_cols // tn),
        in_specs=[pl.BlockSpec((tm, K), lambda i, j: (i, 0)),
                  pl.BlockSpec((K, tn), lambda i, j: (0, j + off))] + list(extra_specs),
        out_specs=pl.BlockSpec((tm, tn), lambda i, j: (i, j)),
        out_shape=jax.ShapeDtypeStruct((T, n_cols), out_dtype),
        compiler_params=_params("parallel", "arbitrary"),
        name=name,
    )(lhs, w, *extras)


def _epi_sigmoid(acc):
    return _sigmoid(acc)


def _epi_identity(acc):
    return acc


def _epi_rotary(acc, cos_ref, sin_ref, *, n_query_tiles):
    scale = jnp.where(pl.program_id(1) < n_query_tiles, HEAD_DIM ** -0.5, 1.0)
    cos, sin = cos_ref[...] * scale, sin_ref[...] * scale
    heads = []
    for h in range(acc.shape[1] // HEAD_DIM):
        xh = acc[:, h * HEAD_DIM:(h + 1) * HEAD_DIM]
        heads.append(xh * cos + pltpu.roll(xh, HEAD_DIM // 2, 1) * sin)
    return jnp.concatenate(heads, axis=1)


def _chunk_scan(x, rid, reverse):
    n = x.shape[0]
    s = 1
    while s < CHUNK:
        if reverse:
            x = x + jnp.where(rid < CHUNK - s, pltpu.roll(x, n - s, 0), 0.0)
        else:
            x = x + jnp.where(rid >= s, pltpu.roll(x, s, 0), 0.0)
        s *= 2
    return x


def _hgrn_prep_kernel(h_ref, wq_ref, wff_ref, wfb_ref, wv_ref, wg_ref, lbf_ref, lbb_ref,
                      qf_ref, kf_ref, ktf_ref, qb_ref, kb_ref, ktb_ref, v_ref, og_ref, df_ref, db_ref, *, layer):
    h = h_ref[...]
    rows, width = qf_ref.shape
    nch = rows // CHUNK
    xq = _dot(h, wq_ref[...])
    q = xq * _sigmoid(xq) * (HEAD_DIM ** -0.5)
    v_ref[...] = _dot(h, wv_ref[...]).astype(v_ref.dtype)
    xg = _dot(h, wg_ref[...])
    og_ref[...] = (xg * _sigmoid(xg)).astype(og_ref.dtype)
    rid = lax.broadcasted_iota(jnp.int32, (rows, width), 0) % CHUNK
    for w_ref, lb_ref, reverse, qo, ko, kto, do in (
            (wff_ref, lbf_ref, False, qf_ref, kf_ref, ktf_ref, df_ref),
            (wfb_ref, lbb_ref, True, qb_ref, kb_ref, ktb_ref, db_ref)):
        lbp = lb_ref[...]
        e = jnp.exp(lbp - jnp.max(lbp, axis=0, keepdims=True))
        lb = jnp.sum(e[:layer + 1], axis=0, keepdims=True) / jnp.sum(e, axis=0, keepdims=True)
        f = lb + (1.0 - lb) * _sigmoid(_dot(h, w_ref[...]))
        k = 1.0 - f
        b = _chunk_scan(jnp.log(f), rid, reverse)
        b3 = b.reshape(nch, CHUNK, width)
        edge = b3[:, 0:1, :] if reverse else b3[:, CHUNK - 1:CHUNK, :]
        tail = (edge - b3).reshape(rows, width)
        qo[...] = (q * jnp.exp(b)).astype(qo.dtype)
        ko[...] = (k * jnp.exp(-b)).astype(ko.dtype)
        kto[...] = (k * jnp.exp(tail)).astype(kto.dtype)
        do[...] = jnp.exp(edge).reshape(nch, width)


def _hgrn_prep(h, w, lbp, hw, layer, tm, tn):
    T, K = h.shape
    nseg = hw // tn
    wspec = lambda seg: pl.BlockSpec((K, tn), lambda i, j: (0, seg * nseg + j))
    lbspec = lambda seg: pl.BlockSpec((lbp.shape[0], tn), lambda i, j: (0, seg * nseg + j))
    ospec = pl.BlockSpec((tm, tn), lambda i, j: (i, j))
    dspec = pl.BlockSpec((tm // CHUNK, tn), lambda i, j: (i, j))
    act = jax.ShapeDtypeStruct((T, hw), BF16)
    dec = jax.ShapeDtypeStruct((T // CHUNK, hw), F32)
    return pl.pallas_call(
        functools.partial(_hgrn_prep_kernel, layer=layer),
        grid=(T // tm, nseg),
        in_specs=[pl.BlockSpec((tm, K), lambda i, j: (i, 0)), wspec(0), wspec(1), wspec(2), wspec(3), wspec(4),
                  lbspec(0), lbspec(1)],
        out_specs=[ospec] * 8 + [dspec] * 2,
        out_shape=[act] * 8 + [dec] * 2,
        compiler_params=_params("parallel", "arbitrary"),
        name="in_hgrn",
    )(h, w, w, w, w, w, lbp, lbp)


def _hgrn_kernel(qf_ref, kf_ref, ktf_ref, qb_ref, kb_ref, ktb_ref, v_ref, og_ref, df_ref, db_ref, nw_ref, o_ref,
                 accf_ref, accb_ref, mf_ref, mb_ref, sf_ref, sb_ref, *, seq):
    R = HGRN_ROWS
    nsb = seq // R
    ncs = R // CHUNK

    row = lax.broadcasted_iota(jnp.int32, (R, R), 0)
    col = lax.broadcasted_iota(jnp.int32, (R, R), 1)
    same = (row // CHUNK) == (col // CHUNK)
    mf_ref[...] = jnp.where(same & (col <= row), 1.0, 0.0).astype(BF16)
    mb_ref[...] = jnp.where(same & (col >= row), 1.0, 0.0).astype(BF16)
    sf_ref[...] = jnp.zeros_like(sf_ref)
    sb_ref[...] = jnp.zeros_like(sb_ref)

    chunks = [slice(c * CHUNK, (c + 1) * CHUNK) for c in range(ncs)]

    def body(sb, carry):
        dirs = ((sb, qf_ref, kf_ref, ktf_ref, df_ref, mf_ref, sf_ref, accf_ref, range(ncs)),
                (nsb - 1 - sb, qb_ref, kb_ref, ktb_ref, db_ref, mb_ref, sb_ref, accb_ref, range(ncs - 1, -1, -1)))
        r0s = [pl.multiple_of(d[0] * R, R) for d in dirs]
        qs = [d[1][pl.ds(r0, R), :] for d, r0 in zip(dirs, r0s)]
        vs = [v_ref[pl.ds(r0, R), :] for r0 in r0s]
        scores = [_dot_nt(q, d[2][pl.ds(r0, R), :]) for d, r0, q in zip(dirs, r0s, qs)]
        updates = []
        for d, r0, v in zip(dirs, r0s, vs):
            kt = d[3][pl.ds(r0, R), :]
            updates.append([_dot_tn(v[sl], kt[sl]) for sl in chunks])
        entering = []
        for d, upd in zip(dirs, updates):
            idx, d_ref, s_ref = d[0], d[4], d[6]
            st = s_ref[...]
            ent = [None] * ncs
            for c in d[8]:
                ent[c] = st.astype(BF16)
                st = st * d_ref[pl.ds(idx * ncs + c, 1), :] + upd[c]
            s_ref[...] = st
            entering.append(ent)
        inter = [[_dot_nt(q[sl], ent[c]) for c, sl in enumerate(chunks)] for q, ent in zip(qs, entering)]
        for d, r0, a, v, o_inter in zip(dirs, r0s, scores, vs, inter):
            o = _dot(jnp.where(d[5][...] > 0, a, 0.0).astype(BF16), v)
            d[7][pl.ds(r0, R), :] = o + jnp.concatenate(o_inter, axis=0)
        return carry

    lax.fori_loop(0, nsb, body, 0)

    FR = HGRN_FINISH_ROWS if seq % HGRN_FINISH_ROWS == 0 else R

    def finish(t, carry):
        r0 = pl.multiple_of(t * FR, FR)
        o = accf_ref[pl.ds(r0, FR), :] + accb_ref[pl.ds(r0, FR), :]
        y = _rms(o, nw_ref[...]) * og_ref[pl.ds(r0, FR), :].astype(F32)
        o_ref[pl.ds(r0, FR), :] = y.astype(o_ref.dtype)
        return carry

    lax.fori_loop(0, seq // FR, finish, 0)


def _hgrn(prep, nw, batch, seq):
    *acts, df, db = prep
    T, HW = acts[0].shape
    H = HW // HEAD_DIM
    blk = pl.BlockSpec((seq, HEAD_DIM), lambda b, h: (b, h))
    dblk = pl.BlockSpec((seq // CHUNK, HEAD_DIM), lambda b, h: (b, h))
    return pl.pallas_call(
        functools.partial(_hgrn_kernel, seq=seq),
        grid=(batch, H),
        in_specs=[blk] * 8 + [dblk] * 2 + [pl.BlockSpec((1, HEAD_DIM), lambda b, h: (0, 0))],
        out_specs=blk,
        out_shape=jax.ShapeDtypeStruct((T, HW), BF16),
        scratch_shapes=[pltpu.VMEM((seq, HEAD_DIM), F32), pltpu.VMEM((seq, HEAD_DIM), F32),
                        pltpu.VMEM((HGRN_ROWS, HGRN_ROWS), BF16), pltpu.VMEM((HGRN_ROWS, HGRN_ROWS), BF16),
                        pltpu.VMEM((HEAD_DIM, HEAD_DIM), F32), pltpu.VMEM((HEAD_DIM, HEAD_DIM), F32)],
        compiler_params=_params("parallel", "parallel"),
        name="hgrn2",
    )(*acts, df, db, nw)


def _attn_kernel(sink_ref, q_ref, kp_ref, kc_ref, kn_ref, vp_ref, vc_ref, vn_ref, o_ref, *, seq):
    G, D, BLK = GQA_GROUP, HEAD_DIM, ATTN_BLOCK
    rows = q_ref.shape[0]
    kvh = pl.program_id(1)
    n = pl.program_id(2)
    k = jnp.concatenate([kp_ref[...], kc_ref[...], kn_ref[...]], axis=0)
    v = jnp.concatenate([vp_ref[...], vc_ref[...], vn_ref[...]], axis=0)
    v1 = jnp.concatenate([v, jnp.ones_like(v)], axis=1)
    r = lax.broadcasted_iota(jnp.int32, (G * BLK, 3 * BLK), 0) % BLK
    c = lax.broadcasted_iota(jnp.int32, (G * BLK, 3 * BLK), 1)
    band = jnp.where(jnp.abs(c - BLK - r) <= WINDOW, 0.0, -jnp.inf)
    kpos = n * rows - BLK + lax.broadcasted_iota(jnp.int32, (1, rows + 2 * BLK), 1)
    inside = jnp.where((kpos >= 0) & (kpos < seq), 0.0, -jnp.inf)
    sk = jnp.concatenate([jnp.full((BLK, 1), sink_ref[kvh * G + h], F32) for h in range(G)], axis=0)
    nsub = rows // BLK
    scores = []
    for j in range(nsub):
        q = q_ref[j * BLK:(j + 1) * BLK, :]
        q4 = jnp.concatenate([q[:, h * D:(h + 1) * D] for h in range(G)], axis=0)
        scores.append(_dot_nt(q4, k[j * BLK:(j + 3) * BLK]))
    probs, sink_terms = [], []
    for j in range(nsub):
        s = scores[j] + band
        if j == 0 or j == nsub - 1:
            s = s + inside[:, j * BLK:(j + 3) * BLK]
        m = jnp.maximum(jnp.max(s, axis=-1, keepdims=True), sk)
        probs.append(jnp.exp(s - m).astype(BF16))
        sink_terms.append(jnp.exp(sk - m))
    for j in range(nsub):
        pv = _dot(probs[j], v1[j * BLK:(j + 3) * BLK])
        o = pv[:, :D] / (pv[:, D:] + sink_terms[j])
        o_ref[j * BLK:(j + 1) * BLK, :] = jnp.concatenate(
            [o[h * BLK:(h + 1) * BLK] for h in range(G)], axis=1).astype(o_ref.dtype)


def _attn(qk, v, sink, batch, seq, n_q_cols, rows):
    T = qk.shape[0]
    G, D, BLK = GQA_GROUP, HEAD_DIM, ATTN_BLOCK
    kvh = v.shape[1] // D
    nb = seq // BLK
    nq = seq // rows
    sub = rows // BLK
    k_off = n_q_cols // D
    qspec = pl.BlockSpec((rows, G * D), lambda b, h, n: (b * nq + n, h))
    prev = lambda b, n: b * nb + jnp.maximum(n * sub - 1, 0)
    nxt = lambda b, n: b * nb + jnp.minimum((n + 1) * sub, nb - 1)
    edge = lambda f, off: pl.BlockSpec((BLK, D), lambda b, h, n: (f(b, n), off + h))
    cur = lambda off: pl.BlockSpec((rows, D), lambda b, h, n: (b * nq + n, off + h))
    return pl.pallas_call(
        functools.partial(_attn_kernel, seq=seq),
        grid=(batch, kvh, nq),
        in_specs=[pl.BlockSpec(memory_space=pltpu.SMEM), qspec,
                  edge(prev, k_off), cur(k_off), edge(nxt, k_off), edge(prev, 0), cur(0), edge(nxt, 0)],
        out_specs=qspec,
        out_shape=jax.ShapeDtypeStruct((T, n_q_cols), BF16),
        compiler_params=_params("parallel", "parallel", "arbitrary"),
        name="swa_sink",
    )(sink, qk, qk, qk, qk, v, v, v)


def _merge_kernel(oh_ref, oa_ref, wh_ref, wa_ref, ga_ref, gb_ref, o_ref):
    ya = _dot(oh_ref[...], wh_ref[...])
    yb = _dot(oa_ref[...], wa_ref[...])
    o_ref[...] = (ga_ref[...].astype(F32) * ya + gb_ref[...].astype(F32) * yb).astype(o_ref.dtype)


def _merge(oh, oa, wh, wa, gates, tm, tn):
    T, KH = oh.shape
    KA = oa.shape[1]
    D = wh.shape[1]
    nj = D // tn
    return pl.pallas_call(
        _merge_kernel,
        grid=(T // tm, nj),
        in_specs=[pl.BlockSpec((tm, KH), lambda i, j: (i, 0)), pl.BlockSpec((tm, KA), lambda i, j: (i, 0)),
                  pl.BlockSpec((KH, tn), lambda i, j: (0, j)), pl.BlockSpec((KA, tn), lambda i, j: (0, j)),
                  pl.BlockSpec((tm, tn), lambda i, j: (i, j)), pl.BlockSpec((tm, tn), lambda i, j: (i, j + nj))],
        out_specs=pl.BlockSpec((tm, tn), lambda i, j: (i, j)),
        out_shape=jax.ShapeDtypeStruct((T, D), BF16),
        compiler_params=_params("parallel", "arbitrary"),
        name="gated_merge",
    )(oh, oa, wh, wa, gates, gates)


def _mlp_kernel(h_ref, wu_ref, wd_ref, o_ref, u_ref, *, n_up, tf):
    j = pl.program_id(1)

    @pl.when(j < n_up)
    def _():
        u = jnp.square(jnp.maximum(_dot(h_ref[...], wu_ref[...]), 0.0))
        u_ref[:, pl.ds(pl.multiple_of(j * tf, tf), tf)] = u.astype(u_ref.dtype)

    @pl.when(j >= n_up)
    def _():
        o_ref[...] = _dot(u_ref[...], wd_ref[...])


def _mlp(h, wu, wd, tm, tf, tn):
    T, D = h.shape
    FF = wu.shape[1]
    n_up = FF // tf
    return pl.pallas_call(
        functools.partial(_mlp_kernel, n_up=n_up, tf=tf),
        grid=(T // tm, n_up + D // tn),
        in_specs=[pl.BlockSpec((tm, D), lambda i, j: (i, 0)),
                  pl.BlockSpec((D, tf), lambda i, j: (0, jnp.minimum(j, n_up - 1))),
                  pl.BlockSpec((FF, tn), lambda i, j: (0, jnp.maximum(j - n_up, 0)))],
        out_specs=pl.BlockSpec((tm, tn), lambda i, j: (i, jnp.maximum(j - n_up, 0))),
        out_shape=jax.ShapeDtypeStruct((T, D), F32),
        scratch_shapes=[pltpu.VMEM((tm, FF), BF16)],
        compiler_params=_params("parallel", "arbitrary"),
        name="relu2_mlp",
    )(h, wu, wd)


def _ple_kernel(x_ref, p_ref, wg_ref, wp_ref, o_ref):
    e = _dot(p_ref[...].astype(BF16), wp_ref[...])
    o_ref[...] = e * _sigmoid(_dot(x_ref[...], wg_ref[...]))


def _ple(xb, p, wg, wp, tm, tn):
    T, D = xb.shape
    P = p.shape[1]
    return pl.pallas_call(
        _ple_kernel,
        grid=(T // tm, D // tn),
        in_specs=[pl.BlockSpec((tm, D), lambda i, j: (i, 0)), pl.BlockSpec((tm, P), lambda i, j: (i, 0)),
                  pl.BlockSpec((D, tn), lambda i, j: (0, j)), pl.BlockSpec((P, tn), lambda i, j: (0, j))],
        out_specs=pl.BlockSpec((tm, tn), lambda i, j: (i, j)),
        out_shape=jax.ShapeDtypeStruct((T, D), F32),
        compiler_params=_params("parallel", "arbitrary"),
        name="ple_gate",
    )(xb, p, wg, wp)


def _rope_tables(seq):
    half = HEAD_DIM // 2
    inv_freq = ROPE_THETA ** (-jnp.arange(0, HEAD_DIM, 2, dtype=F32) / HEAD_DIM)
    ang = jnp.arange(seq, dtype=F32)[:, None] * inv_freq[None, :]
    cos, sin = jnp.cos(ang), jnp.sin(ang)
    assert cos.shape == (seq, half)
    return jnp.concatenate([cos, cos], axis=1), jnp.concatenate([-sin, sin], axis=1)


def kernel(x, p, norm_mix_pre, norm_mix_post, w_in, lb_fwd, lb_bwd, hgrn_norm, attn_sink, w_hgrn_proj,
           w_attn_proj, w_out, norm_mlp_pre, norm_mlp_post, w_mlp_up, w_mlp_down, w_ple, w_ple_gate, norm_ple):
    B, S, D = x.shape
    T = B * S
    depth = w_in.shape[0]
    HW = w_hgrn_proj.shape[1]
    AW = w_attn_proj.shape[1]
    KVW = AW // GQA_GROUP
    assert S % HGRN_ROWS == 0 and S % ATTN_BLOCK == 0 and hgrn_norm.shape[-1] == HEAD_DIM
    assert w_in.shape[2] == 5 * HW + AW + 2 * KVW + 2 * D

    tm = _pick(S, 1024, 512, 256, 128)
    tm_mlp = _pick(T, 512, 256, 128)
    tr = _pick(T, 256, 128)
    cos_t, sin_t = _rope_tables(S)
    vec = lambda a: a.reshape(1, -1)

    xf = x.reshape(T, D)
    for i in range(depth):
        w_in_b = w_in[i].astype(BF16)
        off_aq, off_av, off_gate = 5 * HW, 5 * HW + AW + KVW, 5 * HW + AW + 2 * KVW

        h = _norm_cast(xf, vec(norm_mix_pre[i]), tr)
        tn = lambda off, n: _pick(math.gcd(off, n), 512, 256, 128)
        mm = functools.partial(_mm, h, w_in_b, tm=tm)
        prep = _hgrn_prep(h, w_in_b, jnp.concatenate([lb_fwd, lb_bwd], axis=1), HW, i, tm, 2 * HEAD_DIM)
        tn_r = tn(off_aq, AW + KVW)
        nrb = S // tm if S % tm == 0 else None
        assert nrb is not None
        rope_spec = pl.BlockSpec((tm, HEAD_DIM), lambda r, j: (r % nrb, 0))
        qk = mm(off_aq, AW + KVW, functools.partial(_epi_rotary, n_query_tiles=AW // tn_r), BF16, tn=tn_r,
                extras=(cos_t, sin_t),
                extra_specs=(rope_spec, rope_spec), name="in_qk")
        av = mm(off_av, KVW, _epi_identity, BF16, tn=tn(off_av, KVW), name="in_v")
        gates = mm(off_gate, 2 * D, _epi_sigmoid, BF16, tn=tn(off_gate, 2 * D), name="in_gates")

        o_h = _hgrn(prep, vec(hgrn_norm[i]), B, S)
        o_a = _attn(qk, av, attn_sink[i], B, S, AW, _pick(S, 512, 256, 128))
        y = _merge(o_h, o_a, w_hgrn_proj[i].astype(BF16), w_attn_proj[i].astype(BF16), gates,
                   tm, _pick(D, 512, 256, 128))
        mix = _mm(y, w_out[i].astype(BF16), 0, D, _epi_identity, F32, tm, _pick(D, 512, 256, 128), name="w_out")
        xf, h2 = _resid_norm(xf, mix, vec(norm_mix_post[i]), vec(norm_mlp_pre[i]), tr, renorm=True)

        d = _mlp(h2, w_mlp_up[i].astype(BF16), w_mlp_down[i].astype(BF16), tm_mlp,
                 _pick(w_mlp_up.shape[2], 512, 256), _pick(D, 256, 128))
        xf, xb = _resid_norm(xf, d, vec(norm_mlp_post[i]), vec(norm_mlp_post[i]), tr, renorm=False)

        eg = _ple(xb, p[i].reshape(T, -1), w_ple_gate[i].astype(BF16), w_ple[i].astype(BF16),
                  tm, _pick(D, 512, 256, 128))
        xf = _resid_final(xf, eg, vec(norm_ple[i]), tr)
    return xf.reshape(B, S, D)
```

```python
import functools
import math

import jax
import jax.numpy as jnp
from jax import lax
from jax.experimental import pallas as pl
from jax.experimental.pallas import tpu as pltpu

F32 = jnp.float32
BF16 = jnp.bfloat16

EPS = 1e-6
HEAD_DIM = 128
GQA_GROUP = 4
WINDOW = 128
ATTN_BLOCK = 128
CHUNK = 64
HGRN_ROWS = 256
HGRN_FINISH_ROWS = 1024
ROPE_THETA = 10000.0
MXU_COLS = 256
VMEM_LIMIT_BYTES = 56 * 1024 * 1024


def _params(*semantics):
    return pltpu.CompilerParams(dimension_semantics=semantics, vmem_limit_bytes=VMEM_LIMIT_BYTES)


def _pick(n, *cands):
    for c in cands:
        if n % c == 0:
            return c
    raise ValueError(f"no tile in {cands} divides {n}")


def _dot(a, b):
    return jnp.dot(a, b, preferred_element_type=F32)


def _dot_nt(a, b):
    return lax.dot_general(a, b, (((1,), (1,)), ((), ())), preferred_element_type=F32)


def _dot_tn(a, b):
    return lax.dot_general(a, b, (((0,), (0,)), ((), ())), preferred_element_type=F32)


def _sigmoid(x):
    return 1.0 / (1.0 + jnp.exp(-x))


def _rms(x, w):
    return x * lax.rsqrt(jnp.mean(x * x, axis=-1, keepdims=True) + EPS) * w


def _norm_cast_kernel(x_ref, w_ref, o_ref):
    o_ref[...] = _rms(x_ref[...], w_ref[...]).astype(o_ref.dtype)


def _norm_cast(x, w, tm):
    T, D = x.shape
    return pl.pallas_call(
        _norm_cast_kernel,
        grid=(T // tm,),
        in_specs=[pl.BlockSpec((tm, D), lambda i: (i, 0)), pl.BlockSpec((1, D), lambda i: (0, 0))],
        out_specs=pl.BlockSpec((tm, D), lambda i: (i, 0)),
        out_shape=jax.ShapeDtypeStruct((T, D), BF16),
        compiler_params=_params("parallel"),
        name="norm_cast",
    )(x, w)


def _resid_norm_kernel(x_ref, d_ref, w_ref, w2_ref, o_ref, h_ref, *, renorm):
    y = x_ref[...] + _rms(d_ref[...], w_ref[...])
    o_ref[...] = y
    h_ref[...] = (_rms(y, w2_ref[...]) if renorm else y).astype(h_ref.dtype)


def _resid_norm(x, d, w, w2, tm, renorm):
    T, D = x.shape
    row = pl.BlockSpec((tm, D), lambda i: (i, 0))
    vec = pl.BlockSpec((1, D), lambda i: (0, 0))
    return pl.pallas_call(
        functools.partial(_resid_norm_kernel, renorm=renorm),
        grid=(T // tm,),
        in_specs=[row, row, vec, vec],
        out_specs=[row, row],
        out_shape=[jax.ShapeDtypeStruct((T, D), F32), jax.ShapeDtypeStruct((T, D), BF16)],
        compiler_params=_params("parallel"),
        name="resid_norm",
    )(x, d, w, w2)


def _resid_final_kernel(x_ref, d_ref, w_ref, o_ref):
    o_ref[...] = x_ref[...] + _rms(d_ref[...], w_ref[...])


def _resid_final(x, d, w, tm):
    T, D = x.shape
    row = pl.BlockSpec((tm, D), lambda i: (i, 0))
    vec = pl.BlockSpec((1, D), lambda i: (0, 0))
    return pl.pallas_call(
        _resid_final_kernel,
        grid=(T // tm,),
        in_specs=[row, row, vec],
        out_specs=row,
        out_shape=jax.ShapeDtypeStruct((T, D), F32),
        compiler_params=_params("parallel"),
        name="resid_final",
    )(x, d, w)


def _col_slabs(width):
    step = min(width, MXU_COLS)
    return [slice(c, c + step) for c in range(0, width, step)]


def _mm_kernel(lhs_ref, w_ref, *rest, epilogue):
    *extra_refs, o_ref = rest
    lhs = lhs_ref[...]
    for cols in _col_slabs(o_ref.shape[1]):
        o_ref[:, cols] = epilogue(_dot(lhs, w_ref[:, cols]), *extra_refs).astype(o_ref.dtype)


def _mm(lhs, w, col_off, n_cols, epilogue, out_dtype, tm, tn, extras=(), extra_specs=(), name="mm"):
    T, K = lhs.shape
    off = col_off // tn
    return pl.pallas_call(
        functools.partial(_mm_kernel, epilogue=epilogue),
        grid=(T // tm, n_cols // tn),
        in_specs=[pl.BlockSpec((tm, K), lambda i, j: (i, 0)),
                  pl.BlockSpec((K, tn), lambda i, j: (0, j + off))] + list(extra_specs),
        out_specs=pl.BlockSpec((tm, tn), lambda i, j: (i, j)),
        out_shape=jax.ShapeDtypeStruct((T, n_cols), out_dtype),
        compiler_params=_params("parallel", "arbitrary"),
        name=name,
    )(lhs, w, *extras)


def _epi_sigmoid(acc):
    return _sigmoid(acc)


def _epi_identity(acc):
    return acc


def _epi_rotary(acc, cos_ref, sin_ref, *, n_query_tiles):
    scale = jnp.where(pl.program_id(1) < n_query_tiles, HEAD_DIM ** -0.5, 1.0)
    cos, sin = cos_ref[...] * scale, sin_ref[...] * scale
    heads = []
    for h in range(acc.shape[1] // HEAD_DIM):
        xh = acc[:, h * HEAD_DIM:(h + 1) * HEAD_DIM]
        heads.append(xh * cos + pltpu.roll(xh, HEAD_DIM // 2, 1) * sin)
    return jnp.concatenate(heads, axis=1)


def _chunk_scan(x, rid, reverse):
    n = x.shape[0]
    s = 1
    while s < CHUNK:
        if reverse:
            x = x + jnp.where(rid < CHUNK - s, pltpu.roll(x, n - s, 0), 0.0)
        else:
            x = x + jnp.where(rid >= s, pltpu.roll(x, s, 0), 0.0)
        s *= 2
    return x


def _hgrn_prep_kernel(h_ref, wq_ref, wff_ref, wfb_ref, wv_ref, wg_ref, lbf_ref, lbb_ref,
                      qf_ref, kf_ref, ktf_ref, qb_ref, kb_ref, ktb_ref, v_ref, og_ref, df_ref, db_ref, *, layer):
    h = h_ref[...]
    rows, width = qf_ref.shape
    xf = [_dot(h, wff_ref[...]), _dot(h, wfb_ref[...])]
    xq = _dot(h, wq_ref[...])
    xg = _dot(h, wg_ref[...])
    og_ref[...] = (xg * _sigmoid(xg)).astype(og_ref.dtype)
    v_ref[...] = _dot(h, wv_ref[...]).astype(v_ref.dtype)
    lbs = []
    for lb_ref in (lbf_ref, lbb_ref):
        lbp = lb_ref[...]
        e = jnp.exp(lbp - jnp.max(lbp, axis=0, keepdims=True))
        lbs.append(jnp.sum(e[:layer + 1], axis=0, keepdims=True) / jnp.sum(e, axis=0, keepdims=True))
    rid = lax.broadcasted_iota(jnp.int32, (CHUNK, width), 0)
    for c in range(rows // CHUNK):
        rs = slice(c * CHUNK, (c + 1) * CHUNK)
        q = xq[rs] * _sigmoid(xq[rs]) * (HEAD_DIM ** -0.5)
        for x, lb, reverse, qo, ko, kto, do in (
                (xf[0], lbs[0], False, qf_ref, kf_ref, ktf_ref, df_ref),
                (xf[1], lbs[1], True, qb_ref, kb_ref, ktb_ref, db_ref)):
            f = lb + (1.0 - lb) * _sigmoid(x[rs])
            k = 1.0 - f
            b = _chunk_scan(jnp.log(f), rid, reverse)
            edge = b[0:1, :] if reverse else b[CHUNK - 1:CHUNK, :]
            qo[rs, :] = (q * jnp.exp(b)).astype(qo.dtype)
            ko[rs, :] = (k * jnp.exp(-b)).astype(ko.dtype)
            kto[rs, :] = (k * jnp.exp(edge - b)).astype(kto.dtype)
            do[c:c + 1, :] = jnp.exp(edge)


def _hgrn_prep(h, w, lbp, hw, layer, tm, tn):
    T, K = h.shape
    nseg = hw // tn
    wspec = lambda seg: pl.BlockSpec((K, tn), lambda i, j: (0, seg * nseg + j))
    lbspec = lambda seg: pl.BlockSpec((lbp.shape[0], tn), lambda i, j: (0, seg * nseg + j))
    ospec = pl.BlockSpec((tm, tn), lambda i, j: (i, j))
    dspec = pl.BlockSpec((tm // CHUNK, tn), lambda i, j: (i, j))
    act = jax.ShapeDtypeStruct((T, hw), BF16)
    dec = jax.ShapeDtypeStruct((T // CHUNK, hw), F32)
    return pl.pallas_call(
        functools.partial(_hgrn_prep_kernel, layer=layer),
        grid=(T // tm, nseg),
        in_specs=[pl.BlockSpec((tm, K), lambda i, j: (i, 0)), wspec(0), wspec(1), wspec(2), wspec(3), wspec(4),
                  lbspec(0), lbspec(1)],
        out_specs=[ospec] * 8 + [dspec] * 2,
        out_shape=[act] * 8 + [dec] * 2,
        compiler_params=_params("parallel", "arbitrary"),
        name="in_hgrn",
    )(h, w, w, w, w, w, lbp, lbp)


def _hgrn_kernel(qf_ref, kf_ref, ktf_ref, qb_ref, kb_ref, ktb_ref, v_ref, og_ref, df_ref, db_ref, nw_ref, o_ref,
                 accf_ref, accb_ref, mf_ref, mb_ref, sf_ref, sb_ref, *, seq):
    R = HGRN_ROWS
    nsb = seq // R
    ncs = R // CHUNK

    row = lax.broadcasted_iota(jnp.int32, (R, R), 0)
    col = lax.broadcasted_iota(jnp.int32, (R, R), 1)
    same = (row // CHUNK) == (col // CHUNK)
    mf_ref[...] = jnp.where(same & (col <= row), 1.0, 0.0).astype(BF16)
    mb_ref[...] = jnp.where(same & (col >= row), 1.0, 0.0).astype(BF16)
    sf_ref[...] = jnp.zeros_like(sf_ref)
    sb_ref[...] = jnp.zeros_like(sb_ref)

    chunks = [slice(c * CHUNK, (c + 1) * CHUNK) for c in range(ncs)]

    def body(sb, carry):
        dirs = ((sb, qf_ref, kf_ref, ktf_ref, df_ref, mf_ref, sf_ref, accf_ref, range(ncs)),
                (nsb - 1 - sb, qb_ref, kb_ref, ktb_ref, db_ref, mb_ref, sb_ref, accb_ref, range(ncs - 1, -1, -1)))
        r0s = [pl.multiple_of(d[0] * R, R) for d in dirs]
        qs = [d[1][pl.ds(r0, R), :] for d, r0 in zip(dirs, r0s)]
        vs = [v_ref[pl.ds(r0, R), :] for r0 in r0s]
        scores = [_dot_nt(q, d[2][pl.ds(r0, R), :]) for d, r0, q in zip(dirs, r0s, qs)]
        updates = []
        for d, r0, v in zip(dirs, r0s, vs):
            kt = d[3][pl.ds(r0, R), :]
            updates.append([_dot_tn(v[sl], kt[sl]) for sl in chunks])
        entering = []
        for d, upd in zip(dirs, updates):
            idx, d_ref, s_ref = d[0], d[4], d[6]
            st = s_ref[...]
            ent = [None] * ncs
            for c in d[8]:
                ent[c] = st.astype(BF16)
                st = st * d_ref[pl.ds(idx * ncs + c, 1), :] + upd[c]
            s_ref[...] = st
            entering.append(ent)
        inter = [[_dot_nt(q[sl], ent[c]) for c, sl in enumerate(chunks)] for q, ent in zip(qs, entering)]
        for d, r0, a, v, o_inter in zip(dirs, r0s, scores, vs, inter):
            o = _dot(jnp.where(d[5][...] > 0, a, 0.0).astype(BF16), v)
            d[7][pl.ds(r0, R), :] = o + jnp.concatenate(o_inter, axis=0)
        return carry

    lax.fori_loop(0, nsb, body, 0)

    FR = HGRN_FINISH_ROWS if seq % HGRN_FINISH_ROWS == 0 else R

    def finish(t, carry):
        r0 = pl.multiple_of(t * FR, FR)
        o = accf_ref[pl.ds(r0, FR), :] + accb_ref[pl.ds(r0, FR), :]
        y = _rms(o, nw_ref[...]) * og_ref[pl.ds(r0, FR), :].astype(F32)
        o_ref[pl.ds(r0, FR), :] = y.astype(o_ref.dtype)
        return carry

    lax.fori_loop(0, seq // FR, finish, 0)


def _hgrn(prep, nw, batch, seq):
    *acts, df, db = prep
    T, HW = acts[0].shape
    H = HW // HEAD_DIM
    blk = pl.BlockSpec((seq, HEAD_DIM), lambda b, h: (b, h))
    dblk = pl.BlockSpec((seq // CHUNK, HEAD_DIM), lambda b, h: (b, h))
    return pl.pallas_call(
        functools.partial(_hgrn_kernel, seq=seq),
        grid=(batch, H),
        in_specs=[blk] * 8 + [dblk] * 2 + [pl.BlockSpec((1, HEAD_DIM), lambda b, h: (0, 0))],
        out_specs=blk,
        out_shape=jax.ShapeDtypeStruct((T, HW), BF16),
        scratch_shapes=[pltpu.VMEM((seq, HEAD_DIM), F32), pltpu.VMEM((seq, HEAD_DIM), F32),
                        pltpu.VMEM((HGRN_ROWS, HGRN_ROWS), BF16), pltpu.VMEM((HGRN_ROWS, HGRN_ROWS), BF16),
                        pltpu.VMEM((HEAD_DIM, HEAD_DIM), F32), pltpu.VMEM((HEAD_DIM, HEAD_DIM), F32)],
        compiler_params=_params("parallel", "parallel"),
        name="hgrn2",
    )(*acts, df, db, nw)


def _attn_kernel(sink_ref, q_ref, kp_ref, kc_ref, kn_ref, vp_ref, vc_ref, vn_ref, o_ref, *, seq):
    G, D, BLK = GQA_GROUP, HEAD_DIM, ATTN_BLOCK
    rows = q_ref.shape[0]
    kvh = pl.program_id(1)
    n = pl.program_id(2)
    k = jnp.concatenate([kp_ref[...], kc_ref[...], kn_ref[...]], axis=0)
    v = jnp.concatenate([vp_ref[...], vc_ref[...], vn_ref[...]], axis=0)
    v1 = jnp.concatenate([v, jnp.ones_like(v)], axis=1)
    r = lax.broadcasted_iota(jnp.int32, (G * BLK, 3 * BLK), 0) % BLK
    c = lax.broadcasted_iota(jnp.int32, (G * BLK, 3 * BLK), 1)
    band = jnp.where(jnp.abs(c - BLK - r) <= WINDOW, 0.0, -jnp.inf)
    kpos = n * rows - BLK + lax.broadcasted_iota(jnp.int32, (1, rows + 2 * BLK), 1)
    inside = jnp.where((kpos >= 0) & (kpos < seq), 0.0, -jnp.inf)
    sk = jnp.concatenate([jnp.full((BLK, 1), sink_ref[kvh * G + h], F32) for h in range(G)], axis=0)
    nsub = rows // BLK
    scores = []
    for j in range(nsub):
        q = q_ref[j * BLK:(j + 1) * BLK, :]
        q4 = jnp.concatenate([q[:, h * D:(h + 1) * D] for h in range(G)], axis=0)
        scores.append(_dot_nt(q4, k[j * BLK:(j + 3) * BLK]))
    probs, sink_terms = [], []
    for j in range(nsub):
        s = scores[j] + band
        if j == 0 or j == nsub - 1:
            s = s + inside[:, j * BLK:(j + 3) * BLK]
        m = jnp.maximum(jnp.max(s, axis=-1, keepdims=True), sk)
        probs.append(jnp.exp(s - m).astype(BF16))
        sink_terms.append(jnp.exp(sk - m))
    for j in range(nsub):
        pv = _dot(probs[j], v1[j * BLK:(j + 3) * BLK])
        o = pv[:, :D] / (pv[:, D:] + sink_terms[j])
        o_ref[j * BLK:(j + 1) * BLK, :] = jnp.concatenate(
            [o[h * BLK:(h + 1) * BLK] for h in range(G)], axis=1).astype(o_ref.dtype)


def _attn(qk, v, sink, batch, seq, n_q_cols, rows):
    T = qk.shape[0]
    G, D, BLK = GQA_GROUP, HEAD_DIM, ATTN_BLOCK
    kvh = v.shape[1] // D
    nb = seq // BLK
    nq = seq // rows
    sub = rows // BLK
    k_off = n_q_cols // D
    qspec = pl.BlockSpec((rows, G * D), lambda b, h, n: (b * nq + n, h))
    prev = lambda b, n: b * nb + jnp.maximum(n * sub - 1, 0)
    nxt = lambda b, n: b * nb + jnp.minimum((n + 1) * sub, nb - 1)
    edge = lambda f, off: pl.BlockSpec((BLK, D), lambda b, h, n: (f(b, n), off + h))
    cur = lambda off: pl.BlockSpec((rows, D), lambda b, h, n: (b * nq + n, off + h))
    return pl.pallas_call(
        functools.partial(_attn_kernel, seq=seq),
        grid=(batch, kvh, nq),
        in_specs=[pl.BlockSpec(memory_space=pltpu.SMEM), qspec,
                  edge(prev, k_off), cur(k_off), edge(nxt, k_off), edge(prev, 0), cur(0), edge(nxt, 0)],
        out_specs=qspec,
        out_shape=jax.ShapeDtypeStruct((T, n_q_cols), BF16),
        compiler_params=_params("parallel", "parallel", "arbitrary"),
        name="swa_sink",
    )(sink, qk, qk, qk, qk, v, v, v)


def _merge_kernel(oh_ref, oa_ref, wh_ref, wa_ref, ga_ref, gb_ref, o_ref):
    oh, oa = oh_ref[...], oa_ref[...]
    for cols in _col_slabs(o_ref.shape[1]):
        ya = _dot(oh, wh_ref[:, cols])
        yb = _dot(oa, wa_ref[:, cols])
        o_ref[:, cols] = (ga_ref[:, cols].astype(F32) * ya + gb_ref[:, cols].astype(F32) * yb).astype(o_ref.dtype)


def _merge(oh, oa, wh, wa, gates, tm, tn):
    T, KH = oh.shape
    KA = oa.shape[1]
    D = wh.shape[1]
    nj = D // tn
    return pl.pallas_call(
        _merge_kernel,
        grid=(T // tm, nj),
        in_specs=[pl.BlockSpec((tm, KH), lambda i, j: (i, 0)), pl.BlockSpec((tm, KA), lambda i, j: (i, 0)),
                  pl.BlockSpec((KH, tn), lambda i, j: (0, j)), pl.BlockSpec((KA, tn), lambda i, j: (0, j)),
                  pl.BlockSpec((tm, tn), lambda i, j: (i, j)), pl.BlockSpec((tm, tn), lambda i, j: (i, j + nj))],
        out_specs=pl.BlockSpec((tm, tn), lambda i, j: (i, j)),
        out_shape=jax.ShapeDtypeStruct((T, D), BF16),
        compiler_params=_params("parallel", "arbitrary"),
        name="gated_merge",
    )(oh, oa, wh, wa, gates, gates)


def _mlp_kernel(h_ref, wu_ref, wd_ref, o_ref, u_ref, *, n_up, tf):
    j = pl.program_id(1)

    @pl.when(j < n_up)
    def _():
        u = jnp.square(jnp.maximum(_dot(h_ref[...], wu_ref[...]), 0.0))
        u_ref[:, pl.ds(pl.multiple_of(j * tf, tf), tf)] = u.astype(u_ref.dtype)

    @pl.when(j >= n_up)
    def _():
        o_ref[...] = _dot(u_ref[...], wd_ref[...])


def _mlp(h, wu, wd, tm, tf, tn):
    T, D = h.shape
    FF = wu.shape[1]
    n_up = FF // tf
    return pl.pallas_call(
        functools.partial(_mlp_kernel, n_up=n_up, tf=tf),
        grid=(T // tm, n_up + D // tn),
        in_specs=[pl.BlockSpec((tm, D), lambda i, j: (i, 0)),
                  pl.BlockSpec((D, tf), lambda i, j: (0, jnp.minimum(j, n_up - 1))),
                  pl.BlockSpec((FF, tn), lambda i, j: (0, jnp.maximum(j - n_up, 0)))],
        out_specs=pl.BlockSpec((tm, tn), lambda i, j: (i, jnp.maximum(j - n_up, 0))),
        out_shape=jax.ShapeDtypeStruct((T, D), F32),
        scratch_shapes=[pltpu.VMEM((tm, FF), BF16)],
        compiler_params=_params("parallel", "arbitrary"),
        name="relu2_mlp",
    )(h, wu, wd)


def _ple_kernel(x_ref, p_ref, wg_ref, wp_ref, o_ref):
    x, p = x_ref[...], p_ref[...].astype(BF16)
    for cols in _col_slabs(o_ref.shape[1]):
        o_ref[:, cols] = _dot(p, wp_ref[:, cols]) * _sigmoid(_dot(x, wg_ref[:, cols]))


def _ple(xb, p, wg, wp, tm, tn):
    T, D = xb.shape
    P = p.shape[1]
    return pl.pallas_call(
        _ple_kernel,
        grid=(T // tm, D // tn),
        in_specs=[pl.BlockSpec((tm, D), lambda i, j: (i, 0)), pl.BlockSpec((tm, P), lambda i, j: (i, 0)),
                  pl.BlockSpec((D, tn), lambda i, j: (0, j)), pl.BlockSpec((P, tn), lambda i, j: (0, j))],
        out_specs=pl.BlockSpec((tm, tn), lambda i, j: (i, j)),
        out_shape=jax.ShapeDtypeStruct((T, D), F32),
        compiler_params=_params("parallel", "arbitrary"),
        name="ple_gate",
    )(xb, p, wg, wp)


def _rope_tables(seq):
    half = HEAD_DIM // 2
    inv_freq = ROPE_THETA ** (-jnp.arange(0, HEAD_DIM, 2, dtype=F32) / HEAD_DIM)
    ang = jnp.arange(seq, dtype=F32)[:, None] * inv_freq[None, :]
    cos, sin = jnp.cos(ang), jnp.sin(ang)
    assert cos.shape == (seq, half)
    return jnp.concatenate([cos, cos], axis=1), jnp.concatenate([-sin, sin], axis=1)


def kernel(x, p, norm_mix_pre, norm_mix_post, w_in, lb_fwd, lb_bwd, hgrn_norm, attn_sink, w_hgrn_proj,
           w_attn_proj, w_out, norm_mlp_pre, norm_mlp_post, w_mlp_up, w_mlp_down, w_ple, w_ple_gate, norm_ple):
    B, S, D = x.shape
    T = B * S
    depth = w_in.shape[0]
    HW = w_hgrn_proj.shape[1]
    AW = w_attn_proj.shape[1]
    KVW = AW // GQA_GROUP
    assert S % HGRN_ROWS == 0 and S % ATTN_BLOCK == 0 and hgrn_norm.shape[-1] == HEAD_DIM
    assert w_in.shape[2] == 5 * HW + AW + 2 * KVW + 2 * D

    tm = _pick(S, 1024, 512, 256, 128)
    tm_mlp = _pick(T, 512, 256, 128)
    tr = _pick(T, 256, 128)
    cos_t, sin_t = _rope_tables(S)
    vec = lambda a: a.reshape(1, -1)

    xf = x.reshape(T, D)
    for i in range(depth):
        w_in_b = w_in[i].astype(BF16)
        off_aq, off_av, off_gate = 5 * HW, 5 * HW + AW + KVW, 5 * HW + AW + 2 * KVW

        h = _norm_cast(xf, vec(norm_mix_pre[i]), tr)
        tn = lambda off, n: _pick(math.gcd(off, n), 1024, 512, 256, 128)
        mm = functools.partial(_mm, h, w_in_b, tm=tm)
        prep = _hgrn_prep(h, w_in_b, jnp.concatenate([lb_fwd, lb_bwd], axis=1), HW, i, tm, 2 * HEAD_DIM)
        tn_r = tn(off_aq, AW + KVW)
        nrb = S // tm if S % tm == 0 else None
        assert nrb is not None
        rope_spec = pl.BlockSpec((tm, HEAD_DIM), lambda r, j: (r % nrb, 0))
        qk = mm(off_aq, AW + KVW, functools.partial(_epi_rotary, n_query_tiles=AW // tn_r), BF16, tn=tn_r,
                extras=(cos_t, sin_t),
                extra_specs=(rope_spec, rope_spec), name="in_qk")
        av = mm(off_av, KVW, _epi_identity, BF16, tn=tn(off_av, KVW), name="in_v")
        gates = mm(off_gate, 2 * D, _epi_sigmoid, BF16, tn=tn(off_gate, 2 * D), name="in_gates")

        o_h = _hgrn(prep, vec(hgrn_norm[i]), B, S)
        o_a = _attn(qk, av, attn_sink[i], B, S, AW, _pick(S, 512, 256, 128))
        y = _merge(o_h, o_a, w_hgrn_proj[i].astype(BF16), w_attn_proj[i].astype(BF16), gates,
                   tm, _pick(D, 1024, 512, 256, 128))
        mix = _mm(y, w_out[i].astype(BF16), 0, D, _epi_identity, F32, tm, _pick(D, 1024, 512, 256, 128), name="w_out")
        xf, h2 = _resid_norm(xf, mix, vec(norm_mix_post[i]), vec(norm_mlp_pre[i]), tr, renorm=True)

        d = _mlp(h2, w_mlp_up[i].astype(BF16), w_mlp_down[i].astype(BF16), tm_mlp,
                 _pick(w_mlp_up.shape[2], 512, 256), _pick(D, 256, 128))
        xf, xb = _resid_norm(xf, d, vec(norm_mlp_post[i]), vec(norm_mlp_post[i]), tr, renorm=False)

        eg = _ple(xb, p[i].reshape(T, -1), w_ple_gate[i].astype(BF16), w_ple[i].astype(BF16),
                  tm, _pick(D, 1024, 512, 256, 128))
        xf = _resid_final(xf, eg, vec(norm_ple[i]), tr)
    return xf.reshape(B, S, D)
```

```python
import functools
import math

import jax
import jax.numpy as jnp
from jax import lax
from jax.experimental import pallas as pl
from jax.experimental.pallas import tpu as pltpu

F32 = jnp.float32
BF16 = jnp.bfloat16

EPS = 1e-6
HEAD_DIM = 128
GQA_GROUP = 4
WINDOW = 128
ATTN_BLOCK = 128
CHUNK = 64
HGRN_ROWS = 256
HGRN_UNROLL = 2
HGRN_FINISH_ROWS = 1024
ROPE_THETA = 10000.0
MXU_COLS = 256
SUBLANES = 8
VMEM_LIMIT_BYTES = 56 * 1024 * 1024


def _params(*semantics):
    return pltpu.CompilerParams(dimension_semantics=semantics, vmem_limit_bytes=VMEM_LIMIT_BYTES)


def _pick(n, *cands):
    for c in cands:
        if n % c == 0:
            return c
    raise ValueError(f"no tile in {cands} divides {n}")


def _dot(a, b):
    return jnp.dot(a, b, preferred_element_type=F32)


def _dot_nt(a, b):
    return lax.dot_general(a, b, (((1,), (1,)), ((), ())), preferred_element_type=F32)


def _dot_tn(a, b):
    return lax.dot_general(a, b, (((0,), (0,)), ((), ())), preferred_element_type=F32)


def _sigmoid(x):
    return 1.0 / (1.0 + jnp.exp(-x))


def _rms(x, w):
    return x * lax.rsqrt(jnp.mean(x * x, axis=-1, keepdims=True) + EPS) * w


def _norm_cast_kernel(x_ref, w_ref, o_ref):
    o_ref[...] = _rms(x_ref[...], w_ref[...]).astype(o_ref.dtype)


def _norm_cast(x, w, tm):
    T, D = x.shape
    return pl.pallas_call(
        _norm_cast_kernel,
        grid=(T // tm,),
        in_specs=[pl.BlockSpec((tm, D), lambda i: (i, 0)), pl.BlockSpec((1, D), lambda i: (0, 0))],
        out_specs=pl.BlockSpec((tm, D), lambda i: (i, 0)),
        out_shape=jax.ShapeDtypeStruct((T, D), BF16),
        compiler_params=_params("parallel"),
        name="norm_cast",
    )(x, w)


def _resid_norm_kernel(x_ref, d_ref, w_ref, w2_ref, o_ref, h_ref, *, renorm):
    y = x_ref[...] + _rms(d_ref[...], w_ref[...])
    o_ref[...] = y
    h_ref[...] = (_rms(y, w2_ref[...]) if renorm else y).astype(h_ref.dtype)


def _resid_norm(x, d, w, w2, tm, renorm):
    T, D = x.shape
    row = pl.BlockSpec((tm, D), lambda i: (i, 0))
    vec = pl.BlockSpec((1, D), lambda i: (0, 0))
    return pl.pallas_call(
        functools.partial(_resid_norm_kernel, renorm=renorm),
        grid=(T // tm,),
        in_specs=[row, row, vec, vec],
        out_specs=[row, row],
        out_shape=[jax.ShapeDtypeStruct((T, D), F32), jax.ShapeDtypeStruct((T, D), BF16)],
        compiler_params=_params("parallel"),
        name="resid_norm",
    )(x, d, w, w2)


def _resid_final_kernel(x_ref, d_ref, w_ref, o_ref):
    o_ref[...] = x_ref[...] + _rms(d_ref[...], w_ref[...])


def _resid_final(x, d, w, tm):
    T, D = x.shape
    row = pl.BlockSpec((tm, D), lambda i: (i, 0))
    vec = pl.BlockSpec((1, D), lambda i: (0, 0))
    return pl.pallas_call(
        _resid_final_kernel,
        grid=(T // tm,),
        in_specs=[row, row, vec],
        out_specs=row,
        out_shape=jax.ShapeDtypeStruct((T, D), F32),
        compiler_params=_params("parallel"),
        name="resid_final",
    )(x, d, w)


def _col_slabs(width):
    step = min(width, MXU_COLS)
    return [slice(c, c + step) for c in range(0, width, step)]


def _mm_kernel(lhs_ref, w_ref, *rest, epilogue):
    *extra_refs, o_ref = rest
    lhs = lhs_ref[...]
    for cols in _col_slabs(o_ref.shape[1]):
        o_ref[:, cols] = epilogue(_dot(lhs, w_ref[:, cols]), *extra_refs).astype(o_ref.dtype)


def _mm(lhs, w, col_off, n_cols, epilogue, out_dtype, tm, tn, extras=(), extra_specs=(), name="mm"):
    T, K = lhs.shape
    off = col_off // tn
    return pl.pallas_call(
        functools.partial(_mm_kernel, epilogue=epilogue),
        grid=(T // tm, n_cols // tn),
        in_specs=[pl.BlockSpec((tm, K), lambda i, j: (i, 0)),
                  pl.BlockSpec((K, tn), lambda i, j: (0, j + off))] + list(extra_specs),
        out_specs=pl.BlockSpec((tm, tn), lambda i, j: (i, j)),
        out_shape=jax.ShapeDtypeStruct((T, n_cols), out_dtype),
        compiler_params=_params("parallel", "arbitrary"),
        name=name,
    )(lhs, w, *extras)


def _epi_sigmoid(acc):
    return _sigmoid(acc)


def _epi_identity(acc):
    return acc


def _epi_rotary(acc, cos_ref, sin_ref, *, n_query_tiles):
    scale = jnp.where(pl.program_id(1) < n_query_tiles, HEAD_DIM ** -0.5, 1.0)
    cos, sin = cos_ref[...] * scale, sin_ref[...] * scale
    heads = []
    for h in range(acc.shape[1] // HEAD_DIM):
        xh = acc[:, h * HEAD_DIM:(h + 1) * HEAD_DIM]
        heads.append(xh * cos + pltpu.roll(xh, HEAD_DIM // 2, 1) * sin)
    return jnp.concatenate(heads, axis=1)


def _chunk_scan(x, reverse):
    n = SUBLANES
    rid = lax.broadcasted_iota(jnp.int32, (n, x.shape[1]), 0)
    slabs = []
    for g in range(x.shape[0] // n):
        y = x[g * n:(g + 1) * n]
        s = 1
        while s < n:
            if reverse:
                y = y + jnp.where(rid < n - s, pltpu.roll(y, n - s, 0), 0.0)
            else:
                y = y + jnp.where(rid >= s, pltpu.roll(y, s, 0), 0.0)
            s *= 2
        slabs.append(y)
    order = range(len(slabs) - 1, -1, -1) if reverse else range(len(slabs))
    total = None
    for g in order:
        if total is not None:
            slabs[g] = slabs[g] + total
        total = slabs[g][0:1] if reverse else slabs[g][n - 1:n]
    return jnp.concatenate(slabs, axis=0)


def _hgrn_prep_kernel(h_ref, wq_ref, wff_ref, wfb_ref, wv_ref, wg_ref, lbf_ref, lbb_ref,
                      qf_ref, kf_ref, ktf_ref, qb_ref, kb_ref, ktb_ref, v_ref, og_ref, df_ref, db_ref, *, layer):
    h = h_ref[...]
    rows, width = qf_ref.shape
    xf = [_dot(h, wff_ref[...]), _dot(h, wfb_ref[...])]
    xq = _dot(h, wq_ref[...])
    xg = _dot(h, wg_ref[...])
    og_ref[...] = (xg * _sigmoid(xg)).astype(og_ref.dtype)
    v_ref[...] = _dot(h, wv_ref[...]).astype(v_ref.dtype)
    lbs = []
    for lb_ref in (lbf_ref, lbb_ref):
        lbp = lb_ref[...]
        e = jnp.exp(lbp - jnp.max(lbp, axis=0, keepdims=True))
        lbs.append(jnp.sum(e[:layer + 1], axis=0, keepdims=True) / jnp.sum(e, axis=0, keepdims=True))
    for c in range(rows // CHUNK):
        rs = slice(c * CHUNK, (c + 1) * CHUNK)
        q = xq[rs] * _sigmoid(xq[rs]) * (HEAD_DIM ** -0.5)
        for x, lb, reverse, qo, ko, kto, do in (
                (xf[0], lbs[0], False, qf_ref, kf_ref, ktf_ref, df_ref),
                (xf[1], lbs[1], True, qb_ref, kb_ref, ktb_ref, db_ref)):
            f = lb + (1.0 - lb) * _sigmoid(x[rs])
            k = 1.0 - f
            b = _chunk_scan(jnp.log(f), reverse)
            edge = b[0:1, :] if reverse else b[CHUNK - 1:CHUNK, :]
            qo[rs, :] = (q * jnp.exp(b)).astype(qo.dtype)
            ko[rs, :] = (k * jnp.exp(-b)).astype(ko.dtype)
            kto[rs, :] = (k * jnp.exp(edge - b)).astype(kto.dtype)
            do[c:c + 1, :] = jnp.exp(edge)


def _hgrn_prep(h, w, lbp, hw, layer, tm, tn):
    T, K = h.shape
    nseg = hw // tn
    wspec = lambda seg: pl.BlockSpec((K, tn), lambda i, j: (0, seg * nseg + j))
    lbspec = lambda seg: pl.BlockSpec((lbp.shape[0], tn), lambda i, j: (0, seg * nseg + j))
    ospec = pl.BlockSpec((tm, tn), lambda i, j: (i, j))
    dspec = pl.BlockSpec((tm // CHUNK, tn), lambda i, j: (i, j))
    act = jax.ShapeDtypeStruct((T, hw), BF16)
    dec = jax.ShapeDtypeStruct((T // CHUNK, hw), F32)
    return pl.pallas_call(
        functools.partial(_hgrn_prep_kernel, layer=layer),
        grid=(T // tm, nseg),
        in_specs=[pl.BlockSpec((tm, K), lambda i, j: (i, 0)), wspec(0), wspec(1), wspec(2), wspec(3), wspec(4),
                  lbspec(0), lbspec(1)],
        out_specs=[ospec] * 8 + [dspec] * 2,
        out_shape=[act] * 8 + [dec] * 2,
        compiler_params=_params("parallel", "arbitrary"),
        name="in_hgrn",
    )(h, w, w, w, w, w, lbp, lbp)


def _hgrn_kernel(qf_ref, kf_ref, ktf_ref, qb_ref, kb_ref, ktb_ref, v_ref, og_ref, df_ref, db_ref, nw_ref, o_ref,
                 accf_ref, accb_ref, mf_ref, mb_ref, sf_ref, sb_ref, *, seq):
    R = HGRN_ROWS
    nsb = seq // R
    ncs = R // CHUNK

    row = lax.broadcasted_iota(jnp.int32, (R, R), 0)
    col = lax.broadcasted_iota(jnp.int32, (R, R), 1)
    same = (row // CHUNK) == (col // CHUNK)
    mf_ref[...] = jnp.where(same & (col <= row), 1.0, 0.0).astype(BF16)
    mb_ref[...] = jnp.where(same & (col >= row), 1.0, 0.0).astype(BF16)
    sf_ref[...] = jnp.zeros_like(sf_ref)
    sb_ref[...] = jnp.zeros_like(sb_ref)

    U = HGRN_UNROLL if nsb % HGRN_UNROLL == 0 else 1
    W = U * R
    nw = U * ncs
    blocks = [slice(u * R, (u + 1) * R) for u in range(U)]
    chunks = [slice(c * CHUNK, (c + 1) * CHUNK) for c in range(nw)]

    def body(t, carry):
        dirs = ((t, qf_ref, kf_ref, ktf_ref, df_ref, mf_ref, sf_ref, accf_ref, range(nw)),
                (nsb // U - 1 - t, qb_ref, kb_ref, ktb_ref, db_ref, mb_ref, sb_ref, accb_ref, range(nw - 1, -1, -1)))
        r0s = [pl.multiple_of(d[0] * W, W) for d in dirs]
        qs = [d[1][pl.ds(r0, W), :] for d, r0 in zip(dirs, r0s)]
        vs = [v_ref[pl.ds(r0, W), :] for r0 in r0s]
        scores = []
        for d, r0, q in zip(dirs, r0s, qs):
            k = d[2][pl.ds(r0, W), :]
            scores.append([_dot_nt(q[bl], k[bl]) for bl in blocks])
        updates = []
        for d, r0, v in zip(dirs, r0s, vs):
            kt = d[3][pl.ds(r0, W), :]
            updates.append([_dot_tn(v[sl], kt[sl]) for sl in chunks])
        entering = []
        for d, upd in zip(dirs, updates):
            idx, d_ref, s_ref = d[0], d[4], d[6]
            st = s_ref[...]
            ent = [None] * nw
            for c in d[8]:
                ent[c] = st.astype(BF16)
                st = st * d_ref[pl.ds(idx * nw + c, 1), :] + upd[c]
            s_ref[...] = st
            entering.append(ent)
        inter = [[_dot_nt(q[sl], ent[c]) for c, sl in enumerate(chunks)] for q, ent in zip(qs, entering)]
        for d, r0, sc, v, o_inter in zip(dirs, r0s, scores, vs, inter):
            mask = d[5][...] > 0
            for u, bl in enumerate(blocks):
                o = _dot(jnp.where(mask, sc[u], 0.0).astype(BF16), v[bl])
                d[7][pl.ds(r0 + u * R, R), :] = o + jnp.concatenate(o_inter[u * ncs:(u + 1) * ncs], axis=0)
        return carry

    lax.fori_loop(0, nsb // U, body, 0)

    FR = HGRN_FINISH_ROWS if seq % HGRN_FINISH_ROWS == 0 else R

    def finish(t, carry):
        r0 = pl.multiple_of(t * FR, FR)
        o = accf_ref[pl.ds(r0, FR), :] + accb_ref[pl.ds(r0, FR), :]
        y = _rms(o, nw_ref[...]) * og_ref[pl.ds(r0, FR), :].astype(F32)
        o_ref[pl.ds(r0, FR), :] = y.astype(o_ref.dtype)
        return carry

    lax.fori_loop(0, seq // FR, finish, 0)


def _hgrn(prep, nw, batch, seq):
    *acts, df, db = prep
    T, HW = acts[0].shape
    H = HW // HEAD_DIM
    blk = pl.BlockSpec((seq, HEAD_DIM), lambda b, h: (b, h))
    dblk = pl.BlockSpec((seq // CHUNK, HEAD_DIM), lambda b, h: (b, h))
    return pl.pallas_call(
        functools.partial(_hgrn_kernel, seq=seq),
        grid=(batch, H),
        in_specs=[blk] * 8 + [dblk] * 2 + [pl.BlockSpec((1, HEAD_DIM), lambda b, h: (0, 0))],
        out_specs=blk,
        out_shape=jax.ShapeDtypeStruct((T, HW), BF16),
        scratch_shapes=[pltpu.VMEM((seq, HEAD_DIM), F32), pltpu.VMEM((seq, HEAD_DIM), F32),
                        pltpu.VMEM((HGRN_ROWS, HGRN_ROWS), BF16), pltpu.VMEM((HGRN_ROWS, HGRN_ROWS), BF16),
                        pltpu.VMEM((HEAD_DIM, HEAD_DIM), F32), pltpu.VMEM((HEAD_DIM, HEAD_DIM), F32)],
        compiler_params=_params("parallel", "parallel"),
        name="hgrn2",
    )(*acts, df, db, nw)


def _attn_kernel(sink_ref, q_ref, kp_ref, kc_ref, kn_ref, vp_ref, vc_ref, vn_ref, o_ref, *, seq):
    G, D, BLK = GQA_GROUP, HEAD_DIM, ATTN_BLOCK
    rows = q_ref.shape[0]
    kvh = pl.program_id(1)
    n = pl.program_id(2)
    k = jnp.concatenate([kp_ref[...], kc_ref[...], kn_ref[...]], axis=0)
    v = jnp.concatenate([vp_ref[...], vc_ref[...], vn_ref[...]], axis=0)
    v1 = jnp.concatenate([v, jnp.ones_like(v)], axis=1)
    r = lax.broadcasted_iota(jnp.int32, (G * BLK, 3 * BLK), 0) % BLK
    c = lax.broadcasted_iota(jnp.int32, (G * BLK, 3 * BLK), 1)
    band = jnp.where(jnp.abs(c - BLK - r) <= WINDOW, 0.0, -jnp.inf)
    kpos = n * rows - BLK + lax.broadcasted_iota(jnp.int32, (1, rows + 2 * BLK), 1)
    inside = jnp.where((kpos >= 0) & (kpos < seq), 0.0, -jnp.inf)
    sk = jnp.concatenate([jnp.full((BLK, 1), sink_ref[kvh * G + h], F32) for h in range(G)], axis=0)
    nsub = rows // BLK
    scores = []
    for j in range(nsub):
        q = q_ref[j * BLK:(j + 1) * BLK, :]
        q4 = jnp.concatenate([q[:, h * D:(h + 1) * D] for h in range(G)], axis=0)
        scores.append(_dot_nt(q4, k[j * BLK:(j + 3) * BLK]))
    probs, sink_terms = [], []
    for j in range(nsub):
        s = scores[j] + band
        if j == 0 or j == nsub - 1:
            s = s + inside[:, j * BLK:(j + 3) * BLK]
        m = jnp.maximum(jnp.max(s, axis=-1, keepdims=True), sk)
        probs.append(jnp.exp(s - m).astype(BF16))
        sink_terms.append(jnp.exp(sk - m))
    for j in range(nsub):
        pv = _dot(probs[j], v1[j * BLK:(j + 3) * BLK])
        o = pv[:, :D] / (pv[:, D:] + sink_terms[j])
        o_ref[j * BLK:(j + 1) * BLK, :] = jnp.concatenate(
            [o[h * BLK:(h + 1) * BLK] for h in range(G)], axis=1).astype(o_ref.dtype)


def _attn(qk, v, sink, batch, seq, n_q_cols, rows):
    T = qk.shape[0]
    G, D, BLK = GQA_GROUP, HEAD_DIM, ATTN_BLOCK
    kvh = v.shape[1] // D
    nb = seq // BLK
    nq = seq // rows
    sub = rows // BLK
    k_off = n_q_cols // D
    qspec = pl.BlockSpec((rows, G * D), lambda b, h, n: (b * nq + n, h))
    prev = lambda b, n: b * nb + jnp.maximum(n * sub - 1, 0)
    nxt = lambda b, n: b * nb + jnp.minimum((n + 1) * sub, nb - 1)
    edge = lambda f, off: pl.BlockSpec((BLK, D), lambda b, h, n: (f(b, n), off + h))
    cur = lambda off: pl.BlockSpec((rows, D), lambda b, h, n: (b * nq + n, off + h))
    return pl.pallas_call(
        functools.partial(_attn_kernel, seq=seq),
        grid=(batch, kvh, nq),
        in_specs=[pl.BlockSpec(memory_space=pltpu.SMEM), qspec,
                  edge(prev, k_off), cur(k_off), edge(nxt, k_off), edge(prev, 0), cur(0), edge(nxt, 0)],
        out_specs=qspec,
        out_shape=jax.ShapeDtypeStruct((T, n_q_cols), BF16),
        compiler_params=_params("parallel", "parallel", "arbitrary"),
        name="swa_sink",
    )(sink, qk, qk, qk, qk, v, v, v)


def _merge_kernel(oh_ref, oa_ref, wh_ref, wa_ref, ga_ref, gb_ref, o_ref):
    oh, oa = oh_ref[...], oa_ref[...]
    for cols in _col_slabs(o_ref.shape[1]):
        ya = _dot(oh, wh_ref[:, cols])
        yb = _dot(oa, wa_ref[:, cols])
        o_ref[:, cols] = (ga_ref[:, cols].astype(F32) * ya + gb_ref[:, cols].astype(F32) * yb).astype(o_ref.dtype)


def _merge(oh, oa, wh, wa, gates, tm, tn):
    T, KH = oh.shape
    KA = oa.shape[1]
    D = wh.shape[1]
    nj = D // tn
    return pl.pallas_call(
        _merge_kernel,
        grid=(T // tm, nj),
        in_specs=[pl.BlockSpec((tm, KH), lambda i, j: (i, 0)), pl.BlockSpec((tm, KA), lambda i, j: (i, 0)),
                  pl.BlockSpec((KH, tn), lambda i, j: (0, j)), pl.BlockSpec((KA, tn), lambda i, j: (0, j)),
                  pl.BlockSpec((tm, tn), lambda i, j: (i, j)), pl.BlockSpec((tm, tn), lambda i, j: (i, j + nj))],
        out_specs=pl.BlockSpec((tm, tn), lambda i, j: (i, j)),
        out_shape=jax.ShapeDtypeStruct((T, D), BF16),
        compiler_params=_params("parallel", "arbitrary"),
        name="gated_merge",
    )(oh, oa, wh, wa, gates, gates)


def _mlp_kernel(h_ref, wu_ref, wd_ref, o_ref, u_ref, *, n_up, tf):
    j = pl.program_id(1)

    @pl.when(j < n_up)
    def _():
        u = jnp.square(jnp.maximum(_dot(h_ref[...], wu_ref[...]), 0.0))
        u_ref[:, pl.ds(pl.multiple_of(j * tf, tf), tf)] = u.astype(u_ref.dtype)

    @pl.when(j >= n_up)
    def _():
        o_ref[...] = _dot(u_ref[...], wd_ref[...])


def _col_blocked(w, tn):
    K, N = w.shape
    return w.reshape(K, N // tn, tn).transpose(1, 0, 2)


def _mlp(h, wu, wd, tm):
    T, D = h.shape
    n_up, _, tf = wu.shape
    n_down, FF, tn = wd.shape
    return pl.pallas_call(
        functools.partial(_mlp_kernel, n_up=n_up, tf=tf),
        grid=(T // tm, n_up + n_down),
        in_specs=[pl.BlockSpec((tm, D), lambda i, j: (i, 0)),
                  pl.BlockSpec((None, D, tf), lambda i, j: (jnp.minimum(j, n_up - 1), 0, 0)),
                  pl.BlockSpec((None, FF, tn), lambda i, j: (jnp.maximum(j - n_up, 0), 0, 0))],
        out_specs=pl.BlockSpec((tm, tn), lambda i, j: (i, jnp.maximum(j - n_up, 0))),
        out_shape=jax.ShapeDtypeStruct((T, D), F32),
        scratch_shapes=[pltpu.VMEM((tm, FF), BF16)],
        compiler_params=_params("parallel", "arbitrary"),
        name="relu2_mlp",
    )(h, wu, wd)


def _ple_kernel(x_ref, p_ref, wg_ref, wp_ref, o_ref):
    x, p = x_ref[...], p_ref[...].astype(BF16)
    for cols in _col_slabs(o_ref.shape[1]):
        o_ref[:, cols] = _dot(p, wp_ref[:, cols]) * _sigmoid(_dot(x, wg_ref[:, cols]))


def _ple(xb, p, wg, wp, tm, tn):
    T, D = xb.shape
    P = p.shape[1]
    return pl.pallas_call(
        _ple_kernel,
        grid=(T // tm, D // tn),
        in_specs=[pl.BlockSpec((tm, D), lambda i, j: (i, 0)), pl.BlockSpec((tm, P), lambda i, j: (i, 0)),
                  pl.BlockSpec((D, tn), lambda i, j: (0, j)), pl.BlockSpec((P, tn), lambda i, j: (0, j))],
        out_specs=pl.BlockSpec((tm, tn), lambda i, j: (i, j)),
        out_shape=jax.ShapeDtypeStruct((T, D), F32),
        compiler_params=_params("parallel", "arbitrary"),
        name="ple_gate",
    )(xb, p, wg, wp)


def _rope_tables(seq):
    half = HEAD_DIM // 2
    inv_freq = ROPE_THETA ** (-jnp.arange(0, HEAD_DIM, 2, dtype=F32) / HEAD_DIM)
    ang = jnp.arange(seq, dtype=F32)[:, None] * inv_freq[None, :]
    cos, sin = jnp.cos(ang), jnp.sin(ang)
    assert cos.shape == (seq, half)
    return jnp.concatenate([cos, cos], axis=1), jnp.concatenate([-sin, sin], axis=1)


def kernel(x, p, norm_mix_pre, norm_mix_post, w_in, lb_fwd, lb_bwd, hgrn_norm, attn_sink, w_hgrn_proj,
           w_attn_proj, w_out, norm_mlp_pre, norm_mlp_post, w_mlp_up, w_mlp_down, w_ple, w_ple_gate, norm_ple):
    B, S, D = x.shape
    T = B * S
    depth = w_in.shape[0]
    HW = w_hgrn_proj.shape[1]
    AW = w_attn_proj.shape[1]
    KVW = AW // GQA_GROUP
    assert S % HGRN_ROWS == 0 and S % ATTN_BLOCK == 0 and hgrn_norm.shape[-1] == HEAD_DIM
    assert w_in.shape[2] == 5 * HW + AW + 2 * KVW + 2 * D

    tm = _pick(S, 1024, 512, 256, 128)
    tm_mlp = _pick(T, 512, 256, 128)
    tr = _pick(T, 256, 128)
    cos_t, sin_t = _rope_tables(S)
    vec = lambda a: a.reshape(1, -1)

    xf = x.reshape(T, D)
    for i in range(depth):
        w_in_b = w_in[i].astype(BF16)
        off_aq, off_av, off_gate = 5 * HW, 5 * HW + AW + KVW, 5 * HW + AW + 2 * KVW

        h = _norm_cast(xf, vec(norm_mix_pre[i]), tr)
        tn = lambda off, n: _pick(math.gcd(off, n), 1024, 512, 256, 128)
        mm = functools.partial(_mm, h, w_in_b, tm=tm)
        prep = _hgrn_prep(h, w_in_b, jnp.concatenate([lb_fwd, lb_bwd], axis=1), HW, i, tm, 2 * HEAD_DIM)
        tn_r = tn(off_aq, AW + KVW)
        nrb = S // tm if S % tm == 0 else None
        assert nrb is not None
        rope_spec = pl.BlockSpec((tm, HEAD_DIM), lambda r, j: (r % nrb, 0))
        qk = mm(off_aq, AW + KVW, functools.partial(_epi_rotary, n_query_tiles=AW // tn_r), BF16, tn=tn_r,
                extras=(cos_t, sin_t),
                extra_specs=(rope_spec, rope_spec), name="in_qk")
        av = mm(off_av, KVW, _epi_identity, BF16, tn=tn(off_av, KVW), name="in_v")
        gates = mm(off_gate, 2 * D, _epi_sigmoid, BF16, tn=tn(off_gate, 2 * D), name="in_gates")

        o_h = _hgrn(prep, vec(hgrn_norm[i]), B, S)
        o_a = _attn(qk, av, attn_sink[i], B, S, AW, _pick(S, 512, 256, 128))
        y = _merge(o_h, o_a, w_hgrn_proj[i].astype(BF16), w_attn_proj[i].astype(BF16), gates,
                   tm, _pick(D, 1024, 512, 256, 128))
        mix = _mm(y, w_out[i].astype(BF16), 0, D, _epi_identity, F32, tm, _pick(D, 1024, 512, 256, 128), name="w_out")
        xf, h2 = _resid_norm(xf, mix, vec(norm_mix_post[i]), vec(norm_mlp_pre[i]), tr, renorm=True)

        d = _mlp(h2, _col_blocked(w_mlp_up[i].astype(BF16), _pick(w_mlp_up.shape[2], 512, 256)),
                 _col_blocked(w_mlp_down[i].astype(BF16), _pick(D, 256, 128)), tm_mlp)
        xf, xb = _resid_norm(xf, d, vec(norm_mlp_post[i]), vec(norm_mlp_post[i]), tr, renorm=False)

        eg = _ple(xb, p[i].reshape(T, -1), w_ple_gate[i].astype(BF16), w_ple[i].astype(BF16),
                  tm, _pick(D, 1024, 512, 256, 128))
        xf = _resid_final(xf, eg, vec(norm_ple[i]), tr)
    return xf.reshape(B, S, D)
```

```python
import functools
import math

import jax
import jax.numpy as jnp
from jax import lax
from jax.experimental import pallas as pl
from jax.experimental.pallas import tpu as pltpu

F32 = jnp.float32
BF16 = jnp.bfloat16

EPS = 1e-6
HEAD_DIM = 128
GQA_GROUP = 4
WINDOW = 128
ATTN_BLOCK = 128
CHUNK = 64
HGRN_ROWS = 256
HGRN_UNROLL = 2
HGRN_FINISH_ROWS = 1024
ROPE_THETA = 10000.0
MXU_COLS = 256
SUBLANES = 8
VMEM_LIMIT_BYTES = 56 * 1024 * 1024


def _params(*semantics):
    return pltpu.CompilerParams(dimension_semantics=semantics, vmem_limit_bytes=VMEM_LIMIT_BYTES)


def _pick(n, *cands):
    for c in cands:
        if n % c == 0:
            return c
    raise ValueError(f"no tile in {cands} divides {n}")


def _dot(a, b):
    return jnp.dot(a, b, preferred_element_type=F32)


def _dot_nt(a, b):
    return lax.dot_general(a, b, (((1,), (1,)), ((), ())), preferred_element_type=F32)


def _dot_tn(a, b):
    return lax.dot_general(a, b, (((0,), (0,)), ((), ())), preferred_element_type=F32)


def _sigmoid(x):
    return 1.0 / (1.0 + jnp.exp(-x))


def _rms(x, w):
    return x * lax.rsqrt(jnp.mean(x * x, axis=-1, keepdims=True) + EPS) * w


def _norm_cast_kernel(x_ref, w_ref, o_ref):
    o_ref[...] = _rms(x_ref[...], w_ref[...]).astype(o_ref.dtype)


def _norm_cast(x, w, tm):
    T, D = x.shape
    return pl.pallas_call(
        _norm_cast_kernel,
        grid=(T // tm,),
        in_specs=[pl.BlockSpec((tm, D), lambda i: (i, 0)), pl.BlockSpec((1, D), lambda i: (0, 0))],
        out_specs=pl.BlockSpec((tm, D), lambda i: (i, 0)),
        out_shape=jax.ShapeDtypeStruct((T, D), BF16),
        compiler_params=_params("parallel"),
        name="norm_cast",
    )(x, w)


def _resid_norm_kernel(x_ref, d_ref, w_ref, w2_ref, o_ref, h_ref, *, renorm):
    y = x_ref[...] + _rms(d_ref[...], w_ref[...])
    o_ref[...] = y
    h_ref[...] = (_rms(y, w2_ref[...]) if renorm else y).astype(h_ref.dtype)


def _resid_norm(x, d, w, w2, tm, renorm):
    T, D = x.shape
    row = pl.BlockSpec((tm, D), lambda i: (i, 0))
    vec = pl.BlockSpec((1, D), lambda i: (0, 0))
    return pl.pallas_call(
        functools.partial(_resid_norm_kernel, renorm=renorm),
        grid=(T // tm,),
        in_specs=[row, row, vec, vec],
        out_specs=[row, row],
        out_shape=[jax.ShapeDtypeStruct((T, D), F32), jax.ShapeDtypeStruct((T, D), BF16)],
        compiler_params=_params("parallel"),
        name="resid_norm",
    )(x, d, w, w2)


def _resid_final_kernel(x_ref, d_ref, w_ref, o_ref):
    o_ref[...] = x_ref[...] + _rms(d_ref[...], w_ref[...])


def _resid_final(x, d, w, tm):
    T, D = x.shape
    row = pl.BlockSpec((tm, D), lambda i: (i, 0))
    vec = pl.BlockSpec((1, D), lambda i: (0, 0))
    return pl.pallas_call(
        _resid_final_kernel,
        grid=(T // tm,),
        in_specs=[row, row, vec],
        out_specs=row,
        out_shape=jax.ShapeDtypeStruct((T, D), F32),
        compiler_params=_params("parallel"),
        name="resid_final",
    )(x, d, w)


def _col_slabs(width):
    step = min(width, MXU_COLS)
    return [slice(c, c + step) for c in range(0, width, step)]


def _mm_kernel(lhs_ref, w_ref, *rest, epilogue):
    *extra_refs, o_ref = rest
    lhs = lhs_ref[...]
    for cols in _col_slabs(o_ref.shape[1]):
        o_ref[:, cols] = epilogue(_dot(lhs, w_ref[:, cols]), *extra_refs).astype(o_ref.dtype)


def _mm(lhs, w, col_off, n_cols, epilogue, out_dtype, tm, tn, extras=(), extra_specs=(), name="mm"):
    T, K = lhs.shape
    off = col_off // tn
    return pl.pallas_call(
        functools.partial(_mm_kernel, epilogue=epilogue),
        grid=(T // tm, n_cols // tn),
        in_specs=[pl.BlockSpec((tm, K), lambda i, j: (i, 0)),
                  pl.BlockSpec((K, tn), lambda i, j: (0, j + off))] + list(extra_specs),
        out_specs=pl.BlockSpec((tm, tn), lambda i, j: (i, j)),
        out_shape=jax.ShapeDtypeStruct((T, n_cols), out_dtype),
        compiler_params=_params("parallel", "arbitrary"),
        name=name,
    )(lhs, w, *extras)


def _epi_sigmoid(acc):
    return _sigmoid(acc)


def _epi_identity(acc):
    return acc


def _epi_rotary(acc, cos_ref, sin_ref, *, n_query_tiles):
    scale = jnp.where(pl.program_id(1) < n_query_tiles, HEAD_DIM ** -0.5, 1.0)
    cos, sin = cos_ref[...] * scale, sin_ref[...] * scale
    heads = []
    for h in range(acc.shape[1] // HEAD_DIM):
        xh = acc[:, h * HEAD_DIM:(h + 1) * HEAD_DIM]
        heads.append(xh * cos + pltpu.roll(xh, HEAD_DIM // 2, 1) * sin)
    return jnp.concatenate(heads, axis=1)


def _chunk_scan(x, reverse):
    n = SUBLANES
    rid = lax.broadcasted_iota(jnp.int32, (n, x.shape[1]), 0)
    slabs = []
    for g in range(x.shape[0] // n):
        y = x[g * n:(g + 1) * n]
        s = 1
        while s < n:
            if reverse:
                y = y + jnp.where(rid < n - s, pltpu.roll(y, n - s, 0), 0.0)
            else:
                y = y + jnp.where(rid >= s, pltpu.roll(y, s, 0), 0.0)
            s *= 2
        slabs.append(y)
    order = range(len(slabs) - 1, -1, -1) if reverse else range(len(slabs))
    total = None
    for g in order:
        if total is not None:
            slabs[g] = slabs[g] + total
        total = slabs[g][0:1] if reverse else slabs[g][n - 1:n]
    return jnp.concatenate(slabs, axis=0)


def _hgrn_prep_kernel(h_ref, wq_ref, wff_ref, wfb_ref, wv_ref, wg_ref, lbf_ref, lbb_ref,
                      qf_ref, kf_ref, ktf_ref, qb_ref, kb_ref, ktb_ref, v_ref, og_ref, df_ref, db_ref, *, layer):
    h = h_ref[...]
    rows, width = qf_ref.shape
    xf = [_dot(h, wff_ref[...]), _dot(h, wfb_ref[...])]
    xq = _dot(h, wq_ref[...])
    xg = _dot(h, wg_ref[...])
    og_ref[...] = (xg * _sigmoid(xg)).astype(og_ref.dtype)
    v_ref[...] = _dot(h, wv_ref[...]).astype(v_ref.dtype)
    lbs = []
    for lb_ref in (lbf_ref, lbb_ref):
        lbp = lb_ref[...]
        e = jnp.exp(lbp - jnp.max(lbp, axis=0, keepdims=True))
        lbs.append(jnp.sum(e[:layer + 1], axis=0, keepdims=True) / jnp.sum(e, axis=0, keepdims=True))
    for c in range(rows // CHUNK):
        rs = slice(c * CHUNK, (c + 1) * CHUNK)
        q = xq[rs] * _sigmoid(xq[rs]) * (HEAD_DIM ** -0.5)
        for x, lb, reverse, qo, ko, kto, do in (
                (xf[0], lbs[0], False, qf_ref, kf_ref, ktf_ref, df_ref),
                (xf[1], lbs[1], True, qb_ref, kb_ref, ktb_ref, db_ref)):
            f = lb + (1.0 - lb) * _sigmoid(x[rs])
            k = 1.0 - f
            b = _chunk_scan(jnp.log(f), reverse)
            edge = b[0:1, :] if reverse else b[CHUNK - 1:CHUNK, :]
            qo[rs, :] = (q * jnp.exp(b)).astype(qo.dtype)
            ko[rs, :] = (k * jnp.exp(-b)).astype(ko.dtype)
            kto[rs, :] = (k * jnp.exp(edge - b)).astype(kto.dtype)
            do[c:c + 1, :] = jnp.exp(edge)


def _hgrn_prep(h, w, lbp, hw, layer, tm, tn):
    T, K = h.shape
    nseg = hw // tn
    wspec = lambda seg: pl.BlockSpec((K, tn), lambda i, j: (0, seg * nseg + j))
    lbspec = lambda seg: pl.BlockSpec((lbp.shape[0], tn), lambda i, j: (0, seg * nseg + j))
    ospec = pl.BlockSpec((tm, tn), lambda i, j: (i, j))
    dspec = pl.BlockSpec((tm // CHUNK, tn), lambda i, j: (i, j))
    act = jax.ShapeDtypeStruct((T, hw), BF16)
    dec = jax.ShapeDtypeStruct((T // CHUNK, hw), F32)
    return pl.pallas_call(
        functools.partial(_hgrn_prep_kernel, layer=layer),
        grid=(T // tm, nseg),
        in_specs=[pl.BlockSpec((tm, K), lambda i, j: (i, 0)), wspec(0), wspec(1), wspec(2), wspec(3), wspec(4),
                  lbspec(0), lbspec(1)],
        out_specs=[ospec] * 8 + [dspec] * 2,
        out_shape=[act] * 8 + [dec] * 2,
        compiler_params=_params("parallel", "arbitrary"),
        name="in_hgrn",
    )(h, w, w, w, w, w, lbp, lbp)


def _hgrn_kernel(qf_ref, kf_ref, ktf_ref, qb_ref, kb_ref, ktb_ref, v_ref, og_ref, df_ref, db_ref, nw_ref, o_ref,
                 accf_ref, accb_ref, mf_ref, mb_ref, sf_ref, sb_ref, *, seq):
    R = HGRN_ROWS
    nsb = seq // R
    ncs = R // CHUNK

    row = lax.broadcasted_iota(jnp.int32, (R, R), 0)
    col = lax.broadcasted_iota(jnp.int32, (R, R), 1)
    same = (row // CHUNK) == (col // CHUNK)
    mf_ref[...] = jnp.where(same & (col <= row), 1.0, 0.0).astype(BF16)
    mb_ref[...] = jnp.where(same & (col >= row), 1.0, 0.0).astype(BF16)
    sf_ref[...] = jnp.zeros_like(sf_ref)
    sb_ref[...] = jnp.zeros_like(sb_ref)

    U = HGRN_UNROLL if nsb % HGRN_UNROLL == 0 else 1
    W = U * R
    nw = U * ncs
    blocks = [slice(u * R, (u + 1) * R) for u in range(U)]
    chunks = [slice(c * CHUNK, (c + 1) * CHUNK) for c in range(nw)]

    def body(t, carry):
        dirs = ((t, qf_ref, kf_ref, ktf_ref, df_ref, mf_ref, sf_ref, accf_ref, range(nw)),
                (nsb // U - 1 - t, qb_ref, kb_ref, ktb_ref, db_ref, mb_ref, sb_ref, accb_ref, range(nw - 1, -1, -1)))
        r0s = [pl.multiple_of(d[0] * W, W) for d in dirs]
        qs = [d[1][pl.ds(r0, W), :] for d, r0 in zip(dirs, r0s)]
        vs = [v_ref[pl.ds(r0, W), :] for r0 in r0s]
        scores = []
        for d, r0, q in zip(dirs, r0s, qs):
            k = d[2][pl.ds(r0, W), :]
            scores.append([_dot_nt(q[bl], k[bl]) for bl in blocks])
        updates = []
        for d, r0, v in zip(dirs, r0s, vs):
            kt = d[3][pl.ds(r0, W), :]
            updates.append([_dot_tn(v[sl], kt[sl]) for sl in chunks])
        entering = []
        for d, upd in zip(dirs, updates):
            idx, d_ref, s_ref = d[0], d[4], d[6]
            st = s_ref[...]
            ent = [None] * nw
            for c in d[8]:
                ent[c] = st.astype(BF16)
                st = st * d_ref[pl.ds(idx * nw + c, 1), :] + upd[c]
            s_ref[...] = st
            entering.append(ent)
        inter = [[_dot_nt(q[sl], ent[c]) for c, sl in enumerate(chunks)] for q, ent in zip(qs, entering)]
        for d, r0, sc, v, o_inter in zip(dirs, r0s, scores, vs, inter):
            mask = d[5][...] > 0
            for u, bl in enumerate(blocks):
                o = _dot(jnp.where(mask, sc[u], 0.0).astype(BF16), v[bl])
                d[7][pl.ds(r0 + u * R, R), :] = o + jnp.concatenate(o_inter[u * ncs:(u + 1) * ncs], axis=0)
        return carry

    lax.fori_loop(0, nsb // U, body, 0)

    FR = HGRN_FINISH_ROWS if seq % HGRN_FINISH_ROWS == 0 else R

    def finish(t, carry):
        r0 = pl.multiple_of(t * FR, FR)
        o = accf_ref[pl.ds(r0, FR), :] + accb_ref[pl.ds(r0, FR), :]
        y = _rms(o, nw_ref[...]) * og_ref[pl.ds(r0, FR), :].astype(F32)
        o_ref[pl.ds(r0, FR), :] = y.astype(o_ref.dtype)
        return carry

    lax.fori_loop(0, seq // FR, finish, 0)


def _hgrn(prep, nw, batch, seq):
    *acts, df, db = prep
    T, HW = acts[0].shape
    H = HW // HEAD_DIM
    blk = pl.BlockSpec((seq, HEAD_DIM), lambda b, h: (b, h))
    dblk = pl.BlockSpec((seq // CHUNK, HEAD_DIM), lambda b, h: (b, h))
    return pl.pallas_call(
        functools.partial(_hgrn_kernel, seq=seq),
        grid=(batch, H),
        in_specs=[blk] * 8 + [dblk] * 2 + [pl.BlockSpec((1, HEAD_DIM), lambda b, h: (0, 0))],
        out_specs=blk,
        out_shape=jax.ShapeDtypeStruct((T, HW), BF16),
        scratch_shapes=[pltpu.VMEM((seq, HEAD_DIM), F32), pltpu.VMEM((seq, HEAD_DIM), F32),
                        pltpu.VMEM((HGRN_ROWS, HGRN_ROWS), BF16), pltpu.VMEM((HGRN_ROWS, HGRN_ROWS), BF16),
                        pltpu.VMEM((HEAD_DIM, HEAD_DIM), F32), pltpu.VMEM((HEAD_DIM, HEAD_DIM), F32)],
        compiler_params=_params("parallel", "parallel"),
        name="hgrn2",
    )(*acts, df, db, nw)


def _attn_kernel(sink_ref, q_ref, kp_ref, kc_ref, kn_ref, vp_ref, vc_ref, vn_ref, o_ref, *, seq):
    G, D, BLK = GQA_GROUP, HEAD_DIM, ATTN_BLOCK
    rows = q_ref.shape[0]
    kvh = pl.program_id(1)
    n = pl.program_id(2)
    k = jnp.concatenate([kp_ref[...], kc_ref[...], kn_ref[...]], axis=0)
    v = jnp.concatenate([vp_ref[...], vc_ref[...], vn_ref[...]], axis=0)
    v1 = jnp.concatenate([v, jnp.ones_like(v)], axis=1)
    r = lax.broadcasted_iota(jnp.int32, (G * BLK, 3 * BLK), 0) % BLK
    c = lax.broadcasted_iota(jnp.int32, (G * BLK, 3 * BLK), 1)
    band = jnp.where(jnp.abs(c - BLK - r) <= WINDOW, 0.0, -jnp.inf)
    kpos = n * rows - BLK + lax.broadcasted_iota(jnp.int32, (1, rows + 2 * BLK), 1)
    inside = jnp.where((kpos >= 0) & (kpos < seq), 0.0, -jnp.inf)
    sk = jnp.concatenate([jnp.full((BLK, 1), sink_ref[kvh * G + h], F32) for h in range(G)], axis=0)
    nsub = rows // BLK
    scores = []
    for j in range(nsub):
        q = q_ref[j * BLK:(j + 1) * BLK, :]
        q4 = jnp.concatenate([q[:, h * D:(h + 1) * D] for h in range(G)], axis=0)
        scores.append(_dot_nt(q4, k[j * BLK:(j + 3) * BLK]))
    probs, sink_terms = [], []
    for j in range(nsub):
        s = scores[j] + band
        if j == 0 or j == nsub - 1:
            s = s + inside[:, j * BLK:(j + 3) * BLK]
        m = jnp.maximum(jnp.max(s, axis=-1, keepdims=True), sk)
        probs.append(jnp.exp(s - m).astype(BF16))
        sink_terms.append(jnp.exp(sk - m))
    for j in range(nsub):
        pv = _dot(probs[j], v1[j * BLK:(j + 3) * BLK])
        o = pv[:, :D] / (pv[:, D:] + sink_terms[j])
        o_ref[j * BLK:(j + 1) * BLK, :] = jnp.concatenate(
            [o[h * BLK:(h + 1) * BLK] for h in range(G)], axis=1).astype(o_ref.dtype)


def _attn(qk, v, sink, batch, seq, n_q_cols, rows):
    T = qk.shape[0]
    G, D, BLK = GQA_GROUP, HEAD_DIM, ATTN_BLOCK
    kvh = v.shape[1] // D
    nb = seq // BLK
    nq = seq // rows
    sub = rows // BLK
    k_off = n_q_cols // D
    qspec = pl.BlockSpec((rows, G * D), lambda b, h, n: (b * nq + n, h))
    prev = lambda b, n: b * nb + jnp.maximum(n * sub - 1, 0)
    nxt = lambda b, n: b * nb + jnp.minimum((n + 1) * sub, nb - 1)
    edge = lambda f, off: pl.BlockSpec((BLK, D), lambda b, h, n: (f(b, n), off + h))
    cur = lambda off: pl.BlockSpec((rows, D), lambda b, h, n: (b * nq + n, off + h))
    return pl.pallas_call(
        functools.partial(_attn_kernel, seq=seq),
        grid=(batch, kvh, nq),
        in_specs=[pl.BlockSpec(memory_space=pltpu.SMEM), qspec,
                  edge(prev, k_off), cur(k_off), edge(nxt, k_off), edge(prev, 0), cur(0), edge(nxt, 0)],
        out_specs=qspec,
        out_shape=jax.ShapeDtypeStruct((T, n_q_cols), BF16),
        compiler_params=_params("parallel", "parallel", "arbitrary"),
        name="swa_sink",
    )(sink, qk, qk, qk, qk, v, v, v)


def _merge_kernel(oh_ref, oa_ref, wh_ref, wa_ref, ga_ref, gb_ref, o_ref):
    oh, oa = oh_ref[...], oa_ref[...]
    for cols in _col_slabs(o_ref.shape[1]):
        ya = _dot(oh, wh_ref[:, cols])
        yb = _dot(oa, wa_ref[:, cols])
        o_ref[:, cols] = (ga_ref[:, cols].astype(F32) * ya + gb_ref[:, cols].astype(F32) * yb).astype(o_ref.dtype)


def _merge(oh, oa, wh, wa, gates, tm, tn):
    T, KH = oh.shape
    KA = oa.shape[1]
    D = wh.shape[1]
    nj = D // tn
    return pl.pallas_call(
        _merge_kernel,
        grid=(T // tm, nj),
        in_specs=[pl.BlockSpec((tm, KH), lambda i, j: (i, 0)), pl.BlockSpec((tm, KA), lambda i, j: (i, 0)),
                  pl.BlockSpec((KH, tn), lambda i, j: (0, j)), pl.BlockSpec((KA, tn), lambda i, j: (0, j)),
                  pl.BlockSpec((tm, tn), lambda i, j: (i, j)), pl.BlockSpec((tm, tn), lambda i, j: (i, j + nj))],
        out_specs=pl.BlockSpec((tm, tn), lambda i, j: (i, j)),
        out_shape=jax.ShapeDtypeStruct((T, D), BF16),
        compiler_params=_params("parallel", "arbitrary"),
        name="gated_merge",
    )(oh, oa, wh, wa, gates, gates)


def _mlp_kernel(h_ref, wu_ref, wd_ref, o_ref):
    @pl.when(pl.program_id(1) == 0)
    def _():
        o_ref[...] = jnp.zeros_like(o_ref)

    u = jnp.square(jnp.maximum(_dot(h_ref[...], wu_ref[...]), 0.0)).astype(BF16)
    o_ref[...] += _dot(u, wd_ref[...])


def _mlp(h, wu, wd, tm, tf):
    T, D = h.shape
    FF = wu.shape[1]
    return pl.pallas_call(
        _mlp_kernel,
        grid=(T // tm, FF // tf),
        in_specs=[pl.BlockSpec((tm, D), lambda i, j: (i, 0)),
                  pl.BlockSpec((D, tf), lambda i, j: (0, j)),
                  pl.BlockSpec((tf, D), lambda i, j: (j, 0))],
        out_specs=pl.BlockSpec((tm, D), lambda i, j: (i, 0)),
        out_shape=jax.ShapeDtypeStruct((T, D), F32),
        compiler_params=_params("parallel", "arbitrary"),
        name="relu2_mlp",
    )(h, wu, wd)


def _ple_kernel(x_ref, p_ref, wg_ref, wp_ref, o_ref):
    x, p = x_ref[...], p_ref[...].astype(BF16)
    for cols in _col_slabs(o_ref.shape[1]):
        o_ref[:, cols] = _dot(p, wp_ref[:, cols]) * _sigmoid(_dot(x, wg_ref[:, cols]))


def _ple(xb, p, wg, wp, tm, tn):
    T, D = xb.shape
    P = p.shape[1]
    return pl.pallas_call(
        _ple_kernel,
        grid=(T // tm, D // tn),
        in_specs=[pl.BlockSpec((tm, D), lambda i, j: (i, 0)), pl.BlockSpec((tm, P), lambda i, j: (i, 0)),
                  pl.BlockSpec((D, tn), lambda i, j: (0, j)), pl.BlockSpec((P, tn), lambda i, j: (0, j))],
        out_specs=pl.BlockSpec((tm, tn), lambda i, j: (i, j)),
        out_shape=jax.ShapeDtypeStruct((T, D), F32),
        compiler_params=_params("parallel", "arbitrary"),
        name="ple_gate",
    )(xb, p, wg, wp)


def _rope_tables(seq):
    half = HEAD_DIM // 2
    inv_freq = ROPE_THETA ** (-jnp.arange(0, HEAD_DIM, 2, dtype=F32) / HEAD_DIM)
    ang = jnp.arange(seq, dtype=F32)[:, None] * inv_freq[None, :]
    cos, sin = jnp.cos(ang), jnp.sin(ang)
    assert cos.shape == (seq, half)
    return jnp.concatenate([cos, cos], axis=1), jnp.concatenate([-sin, sin], axis=1)


def kernel(x, p, norm_mix_pre, norm_mix_post, w_in, lb_fwd, lb_bwd, hgrn_norm, attn_sink, w_hgrn_proj,
           w_attn_proj, w_out, norm_mlp_pre, norm_mlp_post, w_mlp_up, w_mlp_down, w_ple, w_ple_gate, norm_ple):
    B, S, D = x.shape
    T = B * S
    depth = w_in.shape[0]
    HW = w_hgrn_proj.shape[1]
    AW = w_attn_proj.shape[1]
    KVW = AW // GQA_GROUP
    assert S % HGRN_ROWS == 0 and S % ATTN_BLOCK == 0 and hgrn_norm.shape[-1] == HEAD_DIM
    assert w_in.shape[2] == 5 * HW + AW + 2 * KVW + 2 * D

    tm = _pick(S, 1024, 512, 256, 128)
    tm_mlp = _pick(T, 512, 256, 128)
    tr = _pick(T, 256, 128)
    cos_t, sin_t = _rope_tables(S)
    vec = lambda a: a.reshape(1, -1)

    xf = x.reshape(T, D)
    for i in range(depth):
        w_in_b = w_in[i].astype(BF16)
        off_aq, off_av, off_gate = 5 * HW, 5 * HW + AW + KVW, 5 * HW + AW + 2 * KVW

        h = _norm_cast(xf, vec(norm_mix_pre[i]), tr)
        tn = lambda off, n: _pick(math.gcd(off, n), 1024, 512, 256, 128)
        mm = functools.partial(_mm, h, w_in_b, tm=tm)
        prep = _hgrn_prep(h, w_in_b, jnp.concatenate([lb_fwd, lb_bwd], axis=1), HW, i, tm, 2 * HEAD_DIM)
        tn_r = tn(off_aq, AW + KVW)
        nrb = S // tm if S % tm == 0 else None
        assert nrb is not None
        rope_spec = pl.BlockSpec((tm, HEAD_DIM), lambda r, j: (r % nrb, 0))
        qk = mm(off_aq, AW + KVW, functools.partial(_epi_rotary, n_query_tiles=AW // tn_r), BF16, tn=tn_r,
                extras=(cos_t, sin_t),
                extra_specs=(rope_spec, rope_spec), name="in_qk")
        av = mm(off_av, KVW, _epi_identity, BF16, tn=tn(off_av, KVW), name="in_v")
        gates = mm(off_gate, 2 * D, _epi_sigmoid, BF16, tn=tn(off_gate, 2 * D), name="in_gates")

        o_h = _hgrn(prep, vec(hgrn_norm[i]), B, S)
        o_a = _attn(qk, av, attn_sink[i], B, S, AW, _pick(S, 512, 256, 128))
        y = _merge(o_h, o_a, w_hgrn_proj[i].astype(BF16), w_attn_proj[i].astype(BF16), gates,
                   tm, _pick(D, 1024, 512, 256, 128))
        mix = _mm(y, w_out[i].astype(BF16), 0, D, _epi_identity, F32, tm, _pick(D, 1024, 512, 256, 128), name="w_out")
        xf, h2 = _resid_norm(xf, mix, vec(norm_mix_post[i]), vec(norm_mlp_pre[i]), tr, renorm=True)

        d = _mlp(h2, w_mlp_up[i].astype(BF16), w_mlp_down[i].astype(BF16), tm_mlp,
                 _pick(w_mlp_up.shape[2], 512, 256))
        xf, xb = _resid_norm(xf, d, vec(norm_mlp_post[i]), vec(norm_mlp_post[i]), tr, renorm=False)

        eg = _ple(xb, p[i].reshape(T, -1), w_ple_gate[i].astype(BF16), w_ple[i].astype(BF16),
                  tm, _pick(D, 1024, 512, 256, 128))
        xf = _resid_final(xf, eg, vec(norm_ple[i]), tr)
    return xf.reshape(B, S, D)
```

```python
import functools
import math

import jax
import jax.numpy as jnp
from jax import lax
from jax.experimental import pallas as pl
from jax.experimental.pallas import tpu as pltpu

F32 = jnp.float32
BF16 = jnp.bfloat16

EPS = 1e-6
HEAD_DIM = 128
GQA_GROUP = 4
WINDOW = 128
ATTN_BLOCK = 128
CHUNK = 64
HGRN_ROWS = 256
HGRN_UNROLL = 4
HGRN_FINISH_ROWS = 1024
ROPE_THETA = 10000.0
MXU_COLS = 256
SUBLANES = 8
VMEM_LIMIT_BYTES = 56 * 1024 * 1024


VMEM_LIMIT_WIDE_BYTES = 60 * 1024 * 1024


def _params(*semantics, vmem_limit_bytes=VMEM_LIMIT_BYTES):
    return pltpu.CompilerParams(dimension_semantics=semantics, vmem_limit_bytes=vmem_limit_bytes)


def _pick(n, *cands):
    for c in cands:
        if n % c == 0:
            return c
    raise ValueError(f"no tile in {cands} divides {n}")


def _dot(a, b):
    return jnp.dot(a, b, preferred_element_type=F32)


def _dot_nt(a, b):
    return lax.dot_general(a, b, (((1,), (1,)), ((), ())), preferred_element_type=F32)


def _dot_tn(a, b):
    return lax.dot_general(a, b, (((0,), (0,)), ((), ())), preferred_element_type=F32)


def _sigmoid(x):
    return 1.0 / (1.0 + jnp.exp(-x))


def _rms(x, w):
    return x * lax.rsqrt(jnp.mean(x * x, axis=-1, keepdims=True) + EPS) * w


def _norm_cast_kernel(x_ref, w_ref, o_ref):
    o_ref[...] = _rms(x_ref[...], w_ref[...]).astype(o_ref.dtype)


def _norm_cast(x, w, tm):
    T, D = x.shape
    return pl.pallas_call(
        _norm_cast_kernel,
        grid=(T // tm,),
        in_specs=[pl.BlockSpec((tm, D), lambda i: (i, 0)), pl.BlockSpec((1, D), lambda i: (0, 0))],
        out_specs=pl.BlockSpec((tm, D), lambda i: (i, 0)),
        out_shape=jax.ShapeDtypeStruct((T, D), BF16),
        compiler_params=_params("parallel"),
        name="norm_cast",
    )(x, w)


def _resid_norm_kernel(x_ref, d_ref, w_ref, w2_ref, o_ref, h_ref, *, renorm):
    y = x_ref[...] + _rms(d_ref[...], w_ref[...])
    o_ref[...] = y
    h_ref[...] = (_rms(y, w2_ref[...]) if renorm else y).astype(h_ref.dtype)


def _resid_norm(x, d, w, w2, tm, renorm):
    T, D = x.shape
    row = pl.BlockSpec((tm, D), lambda i: (i, 0))
    vec = pl.BlockSpec((1, D), lambda i: (0, 0))
    return pl.pallas_call(
        functools.partial(_resid_norm_kernel, renorm=renorm),
        grid=(T // tm,),
        in_specs=[row, row, vec, vec],
        out_specs=[row, row],
        out_shape=[jax.ShapeDtypeStruct((T, D), F32), jax.ShapeDtypeStruct((T, D), BF16)],
        compiler_params=_params("parallel"),
        name="resid_norm",
    )(x, d, w, w2)


def _resid_final_kernel(x_ref, d_ref, w_ref, o_ref):
    o_ref[...] = x_ref[...] + _rms(d_ref[...], w_ref[...])


def _resid_final(x, d, w, tm):
    T, D = x.shape
    row = pl.BlockSpec((tm, D), lambda i: (i, 0))
    vec = pl.BlockSpec((1, D), lambda i: (0, 0))
    return pl.pallas_call(
        _resid_final_kernel,
        grid=(T // tm,),
        in_specs=[row, row, vec],
        out_specs=row,
        out_shape=jax.ShapeDtypeStruct((T, D), F32),
        compiler_params=_params("parallel"),
        name="resid_final",
    )(x, d, w)


def _col_slabs(width):
    step = min(width, MXU_COLS)
    return [slice(c, c + step) for c in range(0, width, step)]


def _mm_kernel(lhs_ref, w_ref, *rest, epilogue):
    *extra_refs, o_ref = rest
    lhs = lhs_ref[...]
    for cols in _col_slabs(o_ref.shape[1]):
        o_ref[:, cols] = epilogue(_dot(lhs, w_ref[:, cols]), *extra_refs).astype(o_ref.dtype)


def _mm(lhs, w, col_off, n_cols, epilogue, out_dtype, tm, tn, extras=(), extra_specs=(), name="mm"):
    T, K = lhs.shape
    off = col_off // tn
    return pl.pallas_call(
        functools.partial(_mm_kernel, epilogue=epilogue),
        grid=(T // tm, n_cols // tn),
        in_specs=[pl.BlockSpec((tm, K), lambda i, j: (i, 0)),
                  pl.BlockSpec((K, tn), lambda i, j: (0, j + off))] + list(extra_specs),
        out_specs=pl.BlockSpec((tm, tn), lambda i, j: (i, j)),
        out_shape=jax.ShapeDtypeStruct((T, n_cols), out_dtype),
        compiler_params=_params("parallel", "arbitrary"),
        name=name,
    )(lhs, w, *extras)


def _epi_sigmoid(acc):
    return _sigmoid(acc)


def _epi_identity(acc):
    return acc


def _epi_rotary(acc, cos_ref, sin_ref, *, n_query_tiles):
    scale = jnp.where(pl.program_id(1) < n_query_tiles, HEAD_DIM ** -0.5, 1.0)
    cos, sin = cos_ref[...] * scale, sin_ref[...] * scale
    heads = []
    for h in range(acc.shape[1] // HEAD_DIM):
        xh = acc[:, h * HEAD_DIM:(h + 1) * HEAD_DIM]
        heads.append(xh * cos + pltpu.roll(xh, HEAD_DIM // 2, 1) * sin)
    return jnp.concatenate(heads, axis=1)


def _chunk_scan(x, reverse):
    n = SUBLANES
    rid = lax.broadcasted_iota(jnp.int32, (n, x.shape[1]), 0)
    slabs = []
    for g in range(x.shape[0] // n):
        y = x[g * n:(g + 1) * n]
        s = 1
        while s < n:
            if reverse:
                y = y + jnp.where(rid < n - s, pltpu.roll(y, n - s, 0), 0.0)
            else:
                y = y + jnp.where(rid >= s, pltpu.roll(y, s, 0), 0.0)
            s *= 2
        slabs.append(y)
    order = range(len(slabs) - 1, -1, -1) if reverse else range(len(slabs))
    total = None
    for g in order:
        if total is not None:
            slabs[g] = slabs[g] + total
        total = slabs[g][0:1] if reverse else slabs[g][n - 1:n]
    return jnp.concatenate(slabs, axis=0)


def _hgrn_prep_kernel(h_ref, wq_ref, wff_ref, wfb_ref, wv_ref, wg_ref, lbf_ref, lbb_ref,
                      qf_ref, kf_ref, ktf_ref, qb_ref, kb_ref, ktb_ref, v_ref, og_ref, df_ref, db_ref, *, layer):
    h = h_ref[...]
    rows, width = qf_ref.shape
    xf = [_dot(h, wff_ref[...]), _dot(h, wfb_ref[...])]
    xq = _dot(h, wq_ref[...])
    xg = _dot(h, wg_ref[...])
    og_ref[...] = (xg * _sigmoid(xg)).astype(og_ref.dtype)
    v_ref[...] = _dot(h, wv_ref[...]).astype(v_ref.dtype)
    lbs = []
    for lb_ref in (lbf_ref, lbb_ref):
        lbp = lb_ref[...]
        e = jnp.exp(lbp - jnp.max(lbp, axis=0, keepdims=True))
        lbs.append(jnp.sum(e[:layer + 1], axis=0, keepdims=True) / jnp.sum(e, axis=0, keepdims=True))
    for c in range(rows // CHUNK):
        rs = slice(c * CHUNK, (c + 1) * CHUNK)
        q = xq[rs] * _sigmoid(xq[rs]) * (HEAD_DIM ** -0.5)
        for x, lb, reverse, qo, ko, kto, do in (
                (xf[0], lbs[0], False, qf_ref, kf_ref, ktf_ref, df_ref),
                (xf[1], lbs[1], True, qb_ref, kb_ref, ktb_ref, db_ref)):
            f = lb + (1.0 - lb) * _sigmoid(x[rs])
            k = 1.0 - f
            b = _chunk_scan(jnp.log(f), reverse)
            edge = b[0:1, :] if reverse else b[CHUNK - 1:CHUNK, :]
            qo[rs, :] = (q * jnp.exp(b)).astype(qo.dtype)
            ko[rs, :] = (k * jnp.exp(-b)).astype(ko.dtype)
            kto[rs, :] = (k * jnp.exp(edge - b)).astype(kto.dtype)
            do[c:c + 1, :] = jnp.exp(edge)


def _hgrn_prep(h, w, lbp, hw, layer, tm, tn):
    T, K = h.shape
    nseg = hw // tn
    wspec = lambda seg: pl.BlockSpec((K, tn), lambda i, j: (0, seg * nseg + j))
    lbspec = lambda seg: pl.BlockSpec((lbp.shape[0], tn), lambda i, j: (0, seg * nseg + j))
    ospec = pl.BlockSpec((tm, tn), lambda i, j: (i, j))
    dspec = pl.BlockSpec((tm // CHUNK, tn), lambda i, j: (i, j))
    act = jax.ShapeDtypeStruct((T, hw), BF16)
    dec = jax.ShapeDtypeStruct((T // CHUNK, hw), F32)
    return pl.pallas_call(
        functools.partial(_hgrn_prep_kernel, layer=layer),
        grid=(T // tm, nseg),
        in_specs=[pl.BlockSpec((tm, K), lambda i, j: (i, 0)), wspec(0), wspec(1), wspec(2), wspec(3), wspec(4),
                  lbspec(0), lbspec(1)],
        out_specs=[ospec] * 8 + [dspec] * 2,
        out_shape=[act] * 8 + [dec] * 2,
        compiler_params=_params("parallel", "arbitrary"),
        name="in_hgrn",
    )(h, w, w, w, w, w, lbp, lbp)


def _hgrn_kernel(qf_ref, kf_ref, ktf_ref, qb_ref, kb_ref, ktb_ref, v_ref, og_ref, df_ref, db_ref, nw_ref, o_ref,
                 accf_ref, accb_ref, mf_ref, mb_ref, sf_ref, sb_ref, *, seq):
    R = HGRN_ROWS
    nsb = seq // R
    ncs = R // CHUNK

    row = lax.broadcasted_iota(jnp.int32, (R, R), 0)
    col = lax.broadcasted_iota(jnp.int32, (R, R), 1)
    same = (row // CHUNK) == (col // CHUNK)
    mf_ref[...] = jnp.where(same & (col <= row), 1.0, 0.0).astype(BF16)
    mb_ref[...] = jnp.where(same & (col >= row), 1.0, 0.0).astype(BF16)
    sf_ref[...] = jnp.zeros_like(sf_ref)
    sb_ref[...] = jnp.zeros_like(sb_ref)

    U = HGRN_UNROLL if nsb % HGRN_UNROLL == 0 else 1
    W = U * R
    nw = U * ncs
    blocks = [slice(u * R, (u + 1) * R) for u in range(U)]
    chunks = [slice(c * CHUNK, (c + 1) * CHUNK) for c in range(nw)]

    def body(t, carry):
        dirs = ((t, qf_ref, kf_ref, ktf_ref, df_ref, mf_ref, sf_ref, accf_ref, range(nw)),
                (nsb // U - 1 - t, qb_ref, kb_ref, ktb_ref, db_ref, mb_ref, sb_ref, accb_ref, range(nw - 1, -1, -1)))
        r0s = [pl.multiple_of(d[0] * W, W) for d in dirs]
        qs = [d[1][pl.ds(r0, W), :] for d, r0 in zip(dirs, r0s)]
        vs = [v_ref[pl.ds(r0, W), :] for r0 in r0s]
        scores = []
        for d, r0, q in zip(dirs, r0s, qs):
            k = d[2][pl.ds(r0, W), :]
            scores.append([_dot_nt(q[bl], k[bl]) for bl in blocks])
        updates = []
        for d, r0, v in zip(dirs, r0s, vs):
            kt = d[3][pl.ds(r0, W), :]
            updates.append([_dot_tn(v[sl], kt[sl]) for sl in chunks])
        entering = []
        for d, upd in zip(dirs, updates):
            idx, d_ref, s_ref = d[0], d[4], d[6]
            st = s_ref[...]
            ent = [None] * nw
            for c in d[8]:
                ent[c] = st.astype(BF16)
                st = st * d_ref[pl.ds(idx * nw + c, 1), :] + upd[c]
            s_ref[...] = st
            entering.append(ent)
        inter = [[_dot_nt(q[sl], ent[c]) for c, sl in enumerate(chunks)] for q, ent in zip(qs, entering)]
        for d, r0, sc, v, o_inter in zip(dirs, r0s, scores, vs, inter):
            mask = d[5][...] > 0
            for u, bl in enumerate(blocks):
                o = _dot(jnp.where(mask, sc[u], 0.0).astype(BF16), v[bl])
                d[7][pl.ds(r0 + u * R, R), :] = o + jnp.concatenate(o_inter[u * ncs:(u + 1) * ncs], axis=0)
        return carry

    lax.fori_loop(0, nsb // U, body, 0)

    FR = HGRN_FINISH_ROWS if seq % HGRN_FINISH_ROWS == 0 else R

    def finish(t, carry):
        r0 = pl.multiple_of(t * FR, FR)
        o = accf_ref[pl.ds(r0, FR), :] + accb_ref[pl.ds(r0, FR), :]
        y = _rms(o, nw_ref[...]) * og_ref[pl.ds(r0, FR), :].astype(F32)
        o_ref[pl.ds(r0, FR), :] = y.astype(o_ref.dtype)
        return carry

    lax.fori_loop(0, seq // FR, finish, 0)


def _hgrn(prep, nw, batch, seq):
    *acts, df, db = prep
    T, HW = acts[0].shape
    H = HW // HEAD_DIM
    blk = pl.BlockSpec((seq, HEAD_DIM), lambda b, h: (b, h))
    dblk = pl.BlockSpec((seq // CHUNK, HEAD_DIM), lambda b, h: (b, h))
    return pl.pallas_call(
        functools.partial(_hgrn_kernel, seq=seq),
        grid=(batch, H),
        in_specs=[blk] * 8 + [dblk] * 2 + [pl.BlockSpec((1, HEAD_DIM), lambda b, h: (0, 0))],
        out_specs=blk,
        out_shape=jax.ShapeDtypeStruct((T, HW), BF16),
        scratch_shapes=[pltpu.VMEM((seq, HEAD_DIM), F32), pltpu.VMEM((seq, HEAD_DIM), F32),
                        pltpu.VMEM((HGRN_ROWS, HGRN_ROWS), BF16), pltpu.VMEM((HGRN_ROWS, HGRN_ROWS), BF16),
                        pltpu.VMEM((HEAD_DIM, HEAD_DIM), F32), pltpu.VMEM((HEAD_DIM, HEAD_DIM), F32)],
        compiler_params=_params("parallel", "parallel"),
        name="hgrn2",
    )(*acts, df, db, nw)


def _attn_kernel(sink_ref, q_ref, kp_ref, kc_ref, kn_ref, vp_ref, vc_ref, vn_ref, o_ref, *, seq):
    G, D, BLK = GQA_GROUP, HEAD_DIM, ATTN_BLOCK
    rows = q_ref.shape[0]
    kvh = pl.program_id(1)
    n = pl.program_id(2)
    k = jnp.concatenate([kp_ref[...], kc_ref[...], kn_ref[...]], axis=0)
    v = jnp.concatenate([vp_ref[...], vc_ref[...], vn_ref[...]], axis=0)
    v1 = jnp.concatenate([v, jnp.ones_like(v)], axis=1)
    r = lax.broadcasted_iota(jnp.int32, (G * BLK, 3 * BLK), 0) % BLK
    c = lax.broadcasted_iota(jnp.int32, (G * BLK, 3 * BLK), 1)
    band = jnp.where(jnp.abs(c - BLK - r) <= WINDOW, 0.0, -jnp.inf)
    kpos = n * rows - BLK + lax.broadcasted_iota(jnp.int32, (1, rows + 2 * BLK), 1)
    inside = jnp.where((kpos >= 0) & (kpos < seq), 0.0, -jnp.inf)
    sk = jnp.concatenate([jnp.full((BLK, BLK), sink_ref[kvh * G + h], F32) for h in range(G)], axis=0)
    nsub = rows // BLK
    scores = []
    for j in range(nsub):
        q = q_ref[j * BLK:(j + 1) * BLK, :]
        q4 = jnp.concatenate([q[:, h * D:(h + 1) * D] for h in range(G)], axis=0)
        scores.append(_dot_nt(q4, k[j * BLK:(j + 3) * BLK]))
    probs, sink_terms = [], []
    for j in range(nsub):
        s = scores[j] + band
        if j == 0 or j == nsub - 1:
            s = s + inside[:, j * BLK:(j + 3) * BLK]
        m = jnp.maximum(jnp.broadcast_to(jnp.max(s, axis=-1, keepdims=True), (G * BLK, BLK)), sk)
        probs.append(jnp.concatenate(
            [jnp.exp(s[:, i * BLK:(i + 1) * BLK] - m) for i in range(3)], axis=1).astype(BF16))
        sink_terms.append(jnp.exp(sk - m))
    for j in range(nsub):
        pv = _dot(probs[j], v1[j * BLK:(j + 3) * BLK])
        o = pv[:, :D] / (pv[:, D:] + sink_terms[j])
        o_ref[j * BLK:(j + 1) * BLK, :] = jnp.concatenate(
            [o[h * BLK:(h + 1) * BLK] for h in range(G)], axis=1).astype(o_ref.dtype)


def _attn(qk, v, sink, batch, seq, n_q_cols, rows):
    T = qk.shape[0]
    G, D, BLK = GQA_GROUP, HEAD_DIM, ATTN_BLOCK
    kvh = v.shape[1] // D
    nb = seq // BLK
    nq = seq // rows
    sub = rows // BLK
    k_off = n_q_cols // D
    qspec = pl.BlockSpec((rows, G * D), lambda b, h, n: (b * nq + n, h))
    prev = lambda b, n: b * nb + jnp.maximum(n * sub - 1, 0)
    nxt = lambda b, n: b * nb + jnp.minimum((n + 1) * sub, nb - 1)
    edge = lambda f, off: pl.BlockSpec((BLK, D), lambda b, h, n: (f(b, n), off + h))
    cur = lambda off: pl.BlockSpec((rows, D), lambda b, h, n: (b * nq + n, off + h))
    return pl.pallas_call(
        functools.partial(_attn_kernel, seq=seq),
        grid=(batch, kvh, nq),
        in_specs=[pl.BlockSpec(memory_space=pltpu.SMEM), qspec,
                  edge(prev, k_off), cur(k_off), edge(nxt, k_off), edge(prev, 0), cur(0), edge(nxt, 0)],
        out_specs=qspec,
        out_shape=jax.ShapeDtypeStruct((T, n_q_cols), BF16),
        compiler_params=_params("parallel", "parallel", "arbitrary"),
        name="swa_sink",
    )(sink, qk, qk, qk, qk, v, v, v)


def _merge_kernel(oh_ref, oa_ref, wh_ref, wa_ref, ga_ref, gb_ref, o_ref):
    oh, oa = oh_ref[...], oa_ref[...]
    for cols in _col_slabs(o_ref.shape[1]):
        ya = _dot(oh, wh_ref[:, cols])
        yb = _dot(oa, wa_ref[:, cols])
        o_ref[:, cols] = (ga_ref[:, cols].astype(F32) * ya + gb_ref[:, cols].astype(F32) * yb).astype(o_ref.dtype)


def _merge(oh, oa, wh, wa, gates, tm, tn):
    T, KH = oh.shape
    KA = oa.shape[1]
    D = wh.shape[1]
    nj = D // tn
    return pl.pallas_call(
        _merge_kernel,
        grid=(T // tm, nj),
        in_specs=[pl.BlockSpec((tm, KH), lambda i, j: (i, 0)), pl.BlockSpec((tm, KA), lambda i, j: (i, 0)),
                  pl.BlockSpec((KH, tn), lambda i, j: (0, j)), pl.BlockSpec((KA, tn), lambda i, j: (0, j)),
                  pl.BlockSpec((tm, tn), lambda i, j: (i, j)), pl.BlockSpec((tm, tn), lambda i, j: (i, j + nj))],
        out_specs=pl.BlockSpec((tm, tn), lambda i, j: (i, j)),
        out_shape=jax.ShapeDtypeStruct((T, D), BF16),
        compiler_params=_params("parallel", "arbitrary"),
        name="gated_merge",
    )(oh, oa, wh, wa, gates, gates)


def _mlp_kernel(h_ref, wu_ref, wd_ref, o_ref):
    @pl.when(pl.program_id(1) == 0)
    def _():
        o_ref[...] = jnp.zeros_like(o_ref)

    u = jnp.square(jnp.maximum(_dot(h_ref[...], wu_ref[...]), 0.0)).astype(BF16)
    o_ref[...] += _dot(u, wd_ref[...])


def _mlp(h, wu, wd, tm, tf):
    T, D = h.shape
    FF = wu.shape[1]
    return pl.pallas_call(
        _mlp_kernel,
        grid=(T // tm, FF // tf),
        in_specs=[pl.BlockSpec((tm, D), lambda i, j: (i, 0), pipeline_mode=pl.Buffered(1)),
                  pl.BlockSpec((D, tf), lambda i, j: (0, j)),
                  pl.BlockSpec((tf, D), lambda i, j: (j, 0))],
        out_specs=pl.BlockSpec((tm, D), lambda i, j: (i, 0)),
        out_shape=jax.ShapeDtypeStruct((T, D), F32),
        compiler_params=_params("parallel", "arbitrary", vmem_limit_bytes=VMEM_LIMIT_WIDE_BYTES),
        name="relu2_mlp",
    )(h, wu, wd)


def _ple_kernel(x_ref, p_ref, wg_ref, wp_ref, o_ref):
    x, p = x_ref[...], p_ref[...].astype(BF16)
    for cols in _col_slabs(o_ref.shape[1]):
        o_ref[:, cols] = _dot(p, wp_ref[:, cols]) * _sigmoid(_dot(x, wg_ref[:, cols]))


def _ple(xb, p, wg, wp, tm, tn):
    T, D = xb.shape
    P = p.shape[1]
    return pl.pallas_call(
        _ple_kernel,
        grid=(T // tm, D // tn),
        in_specs=[pl.BlockSpec((tm, D), lambda i, j: (i, 0)), pl.BlockSpec((tm, P), lambda i, j: (i, 0)),
                  pl.BlockSpec((D, tn), lambda i, j: (0, j)), pl.BlockSpec((P, tn), lambda i, j: (0, j))],
        out_specs=pl.BlockSpec((tm, tn), lambda i, j: (i, j)),
        out_shape=jax.ShapeDtypeStruct((T, D), F32),
        compiler_params=_params("parallel", "arbitrary"),
        name="ple_gate",
    )(xb, p, wg, wp)


def _rope_tables(seq):
    half = HEAD_DIM // 2
    inv_freq = ROPE_THETA ** (-jnp.arange(0, HEAD_DIM, 2, dtype=F32) / HEAD_DIM)
    ang = jnp.arange(seq, dtype=F32)[:, None] * inv_freq[None, :]
    cos, sin = jnp.cos(ang), jnp.sin(ang)
    assert cos.shape == (seq, half)
    return jnp.concatenate([cos, cos], axis=1), jnp.concatenate([-sin, sin], axis=1)


def kernel(x, p, norm_mix_pre, norm_mix_post, w_in, lb_fwd, lb_bwd, hgrn_norm, attn_sink, w_hgrn_proj,
           w_attn_proj, w_out, norm_mlp_pre, norm_mlp_post, w_mlp_up, w_mlp_down, w_ple, w_ple_gate, norm_ple):
    B, S, D = x.shape
    T = B * S
    depth = w_in.shape[0]
    HW = w_hgrn_proj.shape[1]
    AW = w_attn_proj.shape[1]
    KVW = AW // GQA_GROUP
    assert S % HGRN_ROWS == 0 and S % ATTN_BLOCK == 0 and hgrn_norm.shape[-1] == HEAD_DIM
    assert w_in.shape[2] == 5 * HW + AW + 2 * KVW + 2 * D

    tm = _pick(S, 1024, 512, 256, 128)
    tm_mlp = _pick(T, 512, 256, 128)
    tr = _pick(T, 256, 128)
    cos_t, sin_t = _rope_tables(S)
    vec = lambda a: a.reshape(1, -1)

    xf = x.reshape(T, D)
    for i in range(depth):
        w_in_b = w_in[i].astype(BF16)
        off_aq, off_av, off_gate = 5 * HW, 5 * HW + AW + KVW, 5 * HW + AW + 2 * KVW

        h = _norm_cast(xf, vec(norm_mix_pre[i]), tr)
        tn = lambda off, n: _pick(math.gcd(off, n), 1024, 512, 256, 128)
        mm = functools.partial(_mm, h, w_in_b, tm=tm)
        prep = _hgrn_prep(h, w_in_b, jnp.concatenate([lb_fwd, lb_bwd], axis=1), HW, i, tm, 2 * HEAD_DIM)
        tn_r = tn(off_aq, AW + KVW)
        nrb = S // tm if S % tm == 0 else None
        assert nrb is not None
        rope_spec = pl.BlockSpec((tm, HEAD_DIM), lambda r, j: (r % nrb, 0))
        qk = mm(off_aq, AW + KVW, functools.partial(_epi_rotary, n_query_tiles=AW // tn_r), BF16, tn=tn_r,
                extras=(cos_t, sin_t),
                extra_specs=(rope_spec, rope_spec), name="in_qk")
        av = mm(off_av, KVW, _epi_identity, BF16, tn=tn(off_av, KVW), name="in_v")
        gates = mm(off_gate, 2 * D, _epi_sigmoid, BF16, tn=tn(off_gate, 2 * D), name="in_gates")

        o_h = _hgrn(prep, vec(hgrn_norm[i]), B, S)
        o_a = _attn(qk, av, attn_sink[i], B, S, AW, _pick(S, 512, 256, 128))
        y = _merge(o_h, o_a, w_hgrn_proj[i].astype(BF16), w_attn_proj[i].astype(BF16), gates,
                   tm, _pick(D, 1024, 512, 256, 128))
        mix = _mm(y, w_out[i].astype(BF16), 0, D, _epi_identity, F32, tm, _pick(D, 1024, 512, 256, 128), name="w_out")
        xf, h2 = _resid_norm(xf, mix, vec(norm_mix_post[i]), vec(norm_mlp_pre[i]), tr, renorm=True)

        d = _mlp(h2, w_mlp_up[i].astype(BF16), w_mlp_down[i].astype(BF16), tm_mlp,
                 _pick(w_mlp_up.shape[2], 1024, 512, 256))
        xf, xb = _resid_norm(xf, d, vec(norm_mlp_post[i]), vec(norm_mlp_post[i]), tr, renorm=False)

        eg = _ple(xb, p[i].reshape(T, -1), w_ple_gate[i].astype(BF16), w_ple[i].astype(BF16),
                  tm, _pick(D, 1024, 512, 256, 128))
        xf = _resid_final(xf, eg, vec(norm_ple[i]), tr)
    return xf.reshape(B, S, D)
```

```python
import functools
import math

import jax
import jax.numpy as jnp
from jax import lax
from jax.experimental import pallas as pl
from jax.experimental.pallas import tpu as pltpu

F32 = jnp.float32
BF16 = jnp.bfloat16

EPS = 1e-6
HEAD_DIM = 128
GQA_GROUP = 4
WINDOW = 128
ATTN_BLOCK = 128
CHUNK = 64
HGRN_ROWS = 256
HGRN_UNROLL = 4
HGRN_FINISH_ROWS = 1024
ROPE_THETA = 10000.0
MXU_COLS = 256
SUBLANES = 8
BF16_ROWS = 16
VMEM_LIMIT_BYTES = 60 * 1024 * 1024


def _params(*semantics):
    return pltpu.CompilerParams(dimension_semantics=semantics, vmem_limit_bytes=VMEM_LIMIT_BYTES)


def _pick(n, *cands):
    for c in cands:
        if n % c == 0:
            return c
    raise ValueError(f"no tile in {cands} divides {n}")


def _dot(a, b):
    return jnp.dot(a, b, preferred_element_type=F32)


def _dot_nt(a, b):
    return lax.dot_general(a, b, (((1,), (1,)), ((), ())), preferred_element_type=F32)


def _dot_tn(a, b):
    return lax.dot_general(a, b, (((0,), (0,)), ((), ())), preferred_element_type=F32)


def _sigmoid(x):
    return 1.0 / (1.0 + jnp.exp(-x))


def _rms(x, w):
    return x * lax.rsqrt(jnp.mean(x * x, axis=-1, keepdims=True) + EPS) * w


def _norm_cast_kernel(x_ref, w_ref, o_ref):
    o_ref[...] = _rms(x_ref[...], w_ref[...]).astype(o_ref.dtype)


def _norm_cast(x, w, tm):
    T, D = x.shape
    return pl.pallas_call(
        _norm_cast_kernel,
        grid=(T // tm,),
        in_specs=[pl.BlockSpec((tm, D), lambda i: (i, 0)), pl.BlockSpec((1, D), lambda i: (0, 0))],
        out_specs=pl.BlockSpec((tm, D), lambda i: (i, 0)),
        out_shape=jax.ShapeDtypeStruct((T, D), BF16),
        compiler_params=_params("parallel"),
        name="norm_cast",
    )(x, w)


def _resid_norm_kernel(x_ref, d_ref, w_ref, w2_ref, o_ref, h_ref, *, renorm):
    y = x_ref[...] + _rms(d_ref[...], w_ref[...])
    o_ref[...] = y
    h_ref[...] = (_rms(y, w2_ref[...]) if renorm else y).astype(h_ref.dtype)


def _resid_norm(x, d, w, w2, tm, renorm):
    T, D = x.shape
    row = pl.BlockSpec((tm, D), lambda i: (i, 0))
    vec = pl.BlockSpec((1, D), lambda i: (0, 0))
    return pl.pallas_call(
        functools.partial(_resid_norm_kernel, renorm=renorm),
        grid=(T // tm,),
        in_specs=[row, row, vec, vec],
        out_specs=[row, row],
        out_shape=[jax.ShapeDtypeStruct((T, D), F32), jax.ShapeDtypeStruct((T, D), BF16)],
        compiler_params=_params("parallel"),
        name="resid_norm",
    )(x, d, w, w2)


def _resid_final_kernel(x_ref, d_ref, w_ref, o_ref):
    o_ref[...] = x_ref[...] + _rms(d_ref[...], w_ref[...])


def _resid_final(x, d, w, tm):
    T, D = x.shape
    row = pl.BlockSpec((tm, D), lambda i: (i, 0))
    vec = pl.BlockSpec((1, D), lambda i: (0, 0))
    return pl.pallas_call(
        _resid_final_kernel,
        grid=(T // tm,),
        in_specs=[row, row, vec],
        out_specs=row,
        out_shape=jax.ShapeDtypeStruct((T, D), F32),
        compiler_params=_params("parallel"),
        name="resid_final",
    )(x, d, w)


def _col_slabs(width):
    step = min(width, MXU_COLS)
    return [slice(c, c + step) for c in range(0, width, step)]


def _mm_kernel(lhs_ref, w_ref, *rest, epilogue, n_casts):
    n_extra = len(rest) - 2 * n_casts - 1
    extra_refs, cast_in = rest[:n_extra], rest[n_extra:n_extra + n_casts]
    o_ref, cast_out = rest[n_extra + n_casts], rest[n_extra + n_casts + 1:]
    for src, dst in zip(cast_in, cast_out):
        dst[...] = src[...].astype(dst.dtype)
    lhs = lhs_ref[...]
    for cols in _col_slabs(o_ref.shape[1]):
        o_ref[:, cols] = epilogue(_dot(lhs, w_ref[:, cols]), *extra_refs).astype(o_ref.dtype)


def _cast_rows_per_step(a, steps):
    rows = a.shape[0] // steps
    return rows if rows * steps == a.shape[0] and rows % BF16_ROWS == 0 else None


def _mm(lhs, w, col_off, n_cols, epilogue, out_dtype, tm, tn, extras=(), extra_specs=(), casts=(), name="mm"):
    T, K = lhs.shape
    off = col_off // tn
    nj = n_cols // tn
    steps = (T // tm) * nj
    cast_specs = [pl.BlockSpec((_cast_rows_per_step(a, steps), a.shape[1]), lambda i, j: (i * nj + j, 0))
                  for a in casts]
    res = pl.pallas_call(
        functools.partial(_mm_kernel, epilogue=epilogue, n_casts=len(casts)),
        grid=(T // tm, nj),
        in_specs=[pl.BlockSpec((tm, K), lambda i, j: (i, 0)),
                  pl.BlockSpec((K, tn), lambda i, j: (0, j + off))] + list(extra_specs) + cast_specs,
        out_specs=[pl.BlockSpec((tm, tn), lambda i, j: (i, j))] + cast_specs,
        out_shape=[jax.ShapeDtypeStruct((T, n_cols), out_dtype)]
        + [jax.ShapeDtypeStruct(a.shape, BF16) for a in casts],
        compiler_params=_params("parallel", "arbitrary"),
        name=name,
    )(lhs, w, *extras, *casts)
    return (res[0], list(res[1:])) if casts else res[0]


def _epi_sigmoid(acc):
    return _sigmoid(acc)


def _epi_identity(acc):
    return acc


def _epi_rotary(acc, cos_ref, sin_ref, *, n_query_tiles):
    scale = jnp.where(pl.program_id(1) < n_query_tiles, HEAD_DIM ** -0.5, 1.0)
    cos, sin = cos_ref[...] * scale, sin_ref[...] * scale
    heads = []
    for h in range(acc.shape[1] // HEAD_DIM):
        xh = acc[:, h * HEAD_DIM:(h + 1) * HEAD_DIM]
        heads.append(xh * cos + pltpu.roll(xh, HEAD_DIM // 2, 1) * sin)
    return jnp.concatenate(heads, axis=1)


def _chunk_scan(x, reverse):
    n = SUBLANES
    rid = lax.broadcasted_iota(jnp.int32, (n, x.shape[1]), 0)
    slabs = []
    for g in range(x.shape[0] // n):
        y = x[g * n:(g + 1) * n]
        s = 1
        while s < n:
            if reverse:
                y = y + jnp.where(rid < n - s, pltpu.roll(y, n - s, 0), 0.0)
            else:
                y = y + jnp.where(rid >= s, pltpu.roll(y, s, 0), 0.0)
            s *= 2
        slabs.append(y)
    order = range(len(slabs) - 1, -1, -1) if reverse else range(len(slabs))
    total = None
    for g in order:
        if total is not None:
            slabs[g] = slabs[g] + total
        total = slabs[g][0:1] if reverse else slabs[g][n - 1:n]
    return jnp.concatenate(slabs, axis=0)


def _hgrn_prep_kernel(h_ref, wq_ref, wff_ref, wfb_ref, wv_ref, wg_ref, lbf_ref, lbb_ref,
                      qf_ref, kf_ref, ktf_ref, qb_ref, kb_ref, ktb_ref, v_ref, og_ref, df_ref, db_ref, *, layer):
    h = h_ref[...]
    rows, width = qf_ref.shape
    xf = [_dot(h, wff_ref[...]), _dot(h, wfb_ref[...])]
    xq = _dot(h, wq_ref[...])
    xg = _dot(h, wg_ref[...])
    og_ref[...] = (xg * _sigmoid(xg)).astype(og_ref.dtype)
    v_ref[...] = _dot(h, wv_ref[...]).astype(v_ref.dtype)
    lbs = []
    for lb_ref in (lbf_ref, lbb_ref):
        lbp = lb_ref[...]
        e = jnp.exp(lbp - jnp.max(lbp, axis=0, keepdims=True))
        lbs.append(jnp.sum(e[:layer + 1], axis=0, keepdims=True) / jnp.sum(e, axis=0, keepdims=True))
    for c in range(rows // CHUNK):
        rs = slice(c * CHUNK, (c + 1) * CHUNK)
        q = xq[rs] * _sigmoid(xq[rs]) * (HEAD_DIM ** -0.5)
        for x, lb, reverse, qo, ko, kto, do in (
                (xf[0], lbs[0], False, qf_ref, kf_ref, ktf_ref, df_ref),
                (xf[1], lbs[1], True, qb_ref, kb_ref, ktb_ref, db_ref)):
            f = lb + (1.0 - lb) * _sigmoid(x[rs])
            k = 1.0 - f
            b = _chunk_scan(jnp.log(f), reverse)
            edge = b[0:1, :] if reverse else b[CHUNK - 1:CHUNK, :]
            qo[rs, :] = (q * jnp.exp(b)).astype(qo.dtype)
            ko[rs, :] = (k * jnp.exp(-b)).astype(ko.dtype)
            kto[rs, :] = (k * jnp.exp(edge - b)).astype(kto.dtype)
            do[c:c + 1, :] = jnp.exp(edge)


def _hgrn_prep(h, w, lbp, hw, layer, tm, tn):
    T, K = h.shape
    nseg = hw // tn
    wspec = lambda seg: pl.BlockSpec((K, tn), lambda i, j: (0, seg * nseg + j))
    lbspec = lambda seg: pl.BlockSpec((lbp.shape[0], tn), lambda i, j: (0, seg * nseg + j))
    ospec = pl.BlockSpec((tm, tn), lambda i, j: (i, j))
    dspec = pl.BlockSpec((tm // CHUNK, tn), lambda i, j: (i, j))
    act = jax.ShapeDtypeStruct((T, hw), BF16)
    dec = jax.ShapeDtypeStruct((T // CHUNK, hw), F32)
    return pl.pallas_call(
        functools.partial(_hgrn_prep_kernel, layer=layer),
        grid=(T // tm, nseg),
        in_specs=[pl.BlockSpec((tm, K), lambda i, j: (i, 0)), wspec(0), wspec(1), wspec(2), wspec(3), wspec(4),
                  lbspec(0), lbspec(1)],
        out_specs=[ospec] * 8 + [dspec] * 2,
        out_shape=[act] * 8 + [dec] * 2,
        compiler_params=_params("parallel", "arbitrary"),
        name="in_hgrn",
    )(h, w, w, w, w, w, lbp, lbp)


def _hgrn_kernel(qf_ref, kf_ref, ktf_ref, qb_ref, kb_ref, ktb_ref, v_ref, og_ref, df_ref, db_ref, nw_ref, o_ref,
                 accf_ref, accb_ref, mf_ref, mb_ref, sf_ref, sb_ref, *, seq):
    R = HGRN_ROWS
    nsb = seq // R
    ncs = R // CHUNK

    row = lax.broadcasted_iota(jnp.int32, (R, R), 0)
    col = lax.broadcasted_iota(jnp.int32, (R, R), 1)
    same = (row // CHUNK) == (col // CHUNK)
    mf_ref[...] = jnp.where(same & (col <= row), 1.0, 0.0).astype(BF16)
    mb_ref[...] = jnp.where(same & (col >= row), 1.0, 0.0).astype(BF16)
    sf_ref[...] = jnp.zeros_like(sf_ref)
    sb_ref[...] = jnp.zeros_like(sb_ref)

    U = HGRN_UNROLL if nsb % HGRN_UNROLL == 0 else 1
    W = U * R
    nw = U * ncs
    blocks = [slice(u * R, (u + 1) * R) for u in range(U)]
    chunks = [slice(c * CHUNK, (c + 1) * CHUNK) for c in range(nw)]

    def body(t, carry):
        dirs = ((t, qf_ref, kf_ref, ktf_ref, df_ref, mf_ref, sf_ref, accf_ref, range(nw)),
                (nsb // U - 1 - t, qb_ref, kb_ref, ktb_ref, db_ref, mb_ref, sb_ref, accb_ref, range(nw - 1, -1, -1)))
        r0s = [pl.multiple_of(d[0] * W, W) for d in dirs]
        qs = [d[1][pl.ds(r0, W), :] for d, r0 in zip(dirs, r0s)]
        vs = [v_ref[pl.ds(r0, W), :] for r0 in r0s]
        scores = []
        for d, r0, q in zip(dirs, r0s, qs):
            k = d[2][pl.ds(r0, W), :]
            scores.append([_dot_nt(q[bl], k[bl]) for bl in blocks])
        updates = []
        for d, r0, v in zip(dirs, r0s, vs):
            kt = d[3][pl.ds(r0, W), :]
            updates.append([_dot_tn(v[sl], kt[sl]) for sl in chunks])
        entering = []
        for d, upd in zip(dirs, updates):
            idx, d_ref, s_ref = d[0], d[4], d[6]
            st = s_ref[...]
            ent = [None] * nw
            for c in d[8]:
                ent[c] = st.astype(BF16)
                st = st * d_ref[pl.ds(idx * nw + c, 1), :] + upd[c]
            s_ref[...] = st
            entering.append(ent)
        inter = [[_dot_nt(q[sl], ent[c]) for c, sl in enumerate(chunks)] for q, ent in zip(qs, entering)]
        for d, r0, sc, v, o_inter in zip(dirs, r0s, scores, vs, inter):
            mask = d[5][...] > 0
            for u, bl in enumerate(blocks):
                o = _dot(jnp.where(mask, sc[u], 0.0).astype(BF16), v[bl])
                d[7][pl.ds(r0 + u * R, R), :] = o + jnp.concatenate(o_inter[u * ncs:(u + 1) * ncs], axis=0)
        return carry

    lax.fori_loop(0, nsb // U, body, 0)

    FR = HGRN_FINISH_ROWS if seq % HGRN_FINISH_ROWS == 0 else R

    def finish(t, carry):
        r0 = pl.multiple_of(t * FR, FR)
        o = accf_ref[pl.ds(r0, FR), :] + accb_ref[pl.ds(r0, FR), :]
        y = _rms(o, nw_ref[...]) * og_ref[pl.ds(r0, FR), :].astype(F32)
        o_ref[pl.ds(r0, FR), :] = y.astype(o_ref.dtype)
        return carry

    lax.fori_loop(0, seq // FR, finish, 0)


def _hgrn(prep, nw, batch, seq):
    *acts, df, db = prep
    T, HW = acts[0].shape
    H = HW // HEAD_DIM
    blk = pl.BlockSpec((seq, HEAD_DIM), lambda b, h: (b, h))
    dblk = pl.BlockSpec((seq // CHUNK, HEAD_DIM), lambda b, h: (b, h))
    return pl.pallas_call(
        functools.partial(_hgrn_kernel, seq=seq),
        grid=(batch, H),
        in_specs=[blk] * 8 + [dblk] * 2 + [pl.BlockSpec((1, HEAD_DIM), lambda b, h: (0, 0))],
        out_specs=blk,
        out_shape=jax.ShapeDtypeStruct((T, HW), BF16),
        scratch_shapes=[pltpu.VMEM((seq, HEAD_DIM), F32), pltpu.VMEM((seq, HEAD_DIM), F32),
                        pltpu.VMEM((HGRN_ROWS, HGRN_ROWS), BF16), pltpu.VMEM((HGRN_ROWS, HGRN_ROWS), BF16),
                        pltpu.VMEM((HEAD_DIM, HEAD_DIM), F32), pltpu.VMEM((HEAD_DIM, HEAD_DIM), F32)],
        compiler_params=_params("parallel", "parallel"),
        name="hgrn2",
    )(*acts, df, db, nw)


def _attn_kernel(sink_ref, q_ref, kp_ref, kc_ref, kn_ref, vp_ref, vc_ref, vn_ref, o_ref, *, seq):
    G, D, BLK = GQA_GROUP, HEAD_DIM, ATTN_BLOCK
    rows = q_ref.shape[0]
    kvh = pl.program_id(1)
    n = pl.program_id(2)
    k = jnp.concatenate([kp_ref[...], kc_ref[...], kn_ref[...]], axis=0)
    v = jnp.concatenate([vp_ref[...], vc_ref[...], vn_ref[...]], axis=0)
    v1 = jnp.concatenate([v, jnp.ones_like(v)], axis=1)
    r = lax.broadcasted_iota(jnp.int32, (G * BLK, 3 * BLK), 0) % BLK
    c = lax.broadcasted_iota(jnp.int32, (G * BLK, 3 * BLK), 1)
    band = jnp.where(jnp.abs(c - BLK - r) <= WINDOW, 0.0, -jnp.inf)
    kpos = n * rows - BLK + lax.broadcasted_iota(jnp.int32, (1, rows + 2 * BLK), 1)
    inside = jnp.where((kpos >= 0) & (kpos < seq), 0.0, -jnp.inf)
    sk = jnp.concatenate([jnp.full((BLK, BLK), sink_ref[kvh * G + h], F32) for h in range(G)], axis=0)
    nsub = rows // BLK
    scores = []
    for j in range(nsub):
        q = q_ref[j * BLK:(j + 1) * BLK, :]
        q4 = jnp.concatenate([q[:, h * D:(h + 1) * D] for h in range(G)], axis=0)
        scores.append(_dot_nt(q4, k[j * BLK:(j + 3) * BLK]))
    probs, sink_terms = [], []
    for j in range(nsub):
        s = scores[j] + band
        if j == 0 or j == nsub - 1:
            s = s + inside[:, j * BLK:(j + 3) * BLK]
        m = jnp.maximum(jnp.broadcast_to(jnp.max(s, axis=-1, keepdims=True), (G * BLK, BLK)), sk)
        probs.append(jnp.concatenate(
            [jnp.exp(s[:, i * BLK:(i + 1) * BLK] - m) for i in range(3)], axis=1).astype(BF16))
        sink_terms.append(jnp.exp(sk - m))
    for j in range(nsub):
        pv = _dot(probs[j], v1[j * BLK:(j + 3) * BLK])
        o = pv[:, :D] / (pv[:, D:] + sink_terms[j])
        o_ref[j * BLK:(j + 1) * BLK, :] = jnp.concatenate(
            [o[h * BLK:(h + 1) * BLK] for h in range(G)], axis=1).astype(o_ref.dtype)


def _attn(qk, v, sink, batch, seq, n_q_cols, rows):
    T = qk.shape[0]
    G, D, BLK = GQA_GROUP, HEAD_DIM, ATTN_BLOCK
    kvh = v.shape[1] // D
    nb = seq // BLK
    nq = seq // rows
    sub = rows // BLK
    k_off = n_q_cols // D
    qspec = pl.BlockSpec((rows, G * D), lambda b, h, n: (b * nq + n, h))
    prev = lambda b, n: b * nb + jnp.maximum(n * sub - 1, 0)
    nxt = lambda b, n: b * nb + jnp.minimum((n + 1) * sub, nb - 1)
    edge = lambda f, off: pl.BlockSpec((BLK, D), lambda b, h, n: (f(b, n), off + h))
    cur = lambda off: pl.BlockSpec((rows, D), lambda b, h, n: (b * nq + n, off + h))
    return pl.pallas_call(
        functools.partial(_attn_kernel, seq=seq),
        grid=(batch, kvh, nq),
        in_specs=[pl.BlockSpec(memory_space=pltpu.SMEM), qspec,
                  edge(prev, k_off), cur(k_off), edge(nxt, k_off), edge(prev, 0), cur(0), edge(nxt, 0)],
        out_specs=qspec,
        out_shape=jax.ShapeDtypeStruct((T, n_q_cols), BF16),
        compiler_params=_params("parallel", "parallel", "arbitrary"),
        name="swa_sink",
    )(sink, qk, qk, qk, qk, v, v, v)


def _merge_kernel(oh_ref, oa_ref, wh_ref, wa_ref, ga_ref, gb_ref, o_ref):
    oh, oa = oh_ref[...], oa_ref[...]
    for cols in _col_slabs(o_ref.shape[1]):
        ya = _dot(oh, wh_ref[:, cols])
        yb = _dot(oa, wa_ref[:, cols])
        o_ref[:, cols] = (ga_ref[:, cols].astype(F32) * ya + gb_ref[:, cols].astype(F32) * yb).astype(o_ref.dtype)


def _merge(oh, oa, wh, wa, gates, tm, tn):
    T, KH = oh.shape
    KA = oa.shape[1]
    D = wh.shape[1]
    nj = D // tn
    return pl.pallas_call(
        _merge_kernel,
        grid=(T // tm, nj),
        in_specs=[pl.BlockSpec((tm, KH), lambda i, j: (i, 0)), pl.BlockSpec((tm, KA), lambda i, j: (i, 0)),
                  pl.BlockSpec((KH, tn), lambda i, j: (0, j)), pl.BlockSpec((KA, tn), lambda i, j: (0, j)),
                  pl.BlockSpec((tm, tn), lambda i, j: (i, j)), pl.BlockSpec((tm, tn), lambda i, j: (i, j + nj))],
        out_specs=pl.BlockSpec((tm, tn), lambda i, j: (i, j)),
        out_shape=jax.ShapeDtypeStruct((T, D), BF16),
        compiler_params=_params("parallel", "arbitrary"),
        name="gated_merge",
    )(oh, oa, wh, wa, gates, gates)


def _mlp_kernel(h_ref, wu_ref, wd_ref, o_ref):
    @pl.when(pl.program_id(1) == 0)
    def _():
        o_ref[...] = jnp.zeros_like(o_ref)

    u = jnp.square(jnp.maximum(_dot(h_ref[...], wu_ref[...]), 0.0)).astype(BF16)
    o_ref[...] += _dot(u, wd_ref[...])


def _mlp(h, wu, wd, tm, tf):
    T, D = h.shape
    FF = wu.shape[1]
    return pl.pallas_call(
        _mlp_kernel,
        grid=(T // tm, FF // tf),
        in_specs=[pl.BlockSpec((tm, D), lambda i, j: (i, 0)),
                  pl.BlockSpec((D, tf), lambda i, j: (0, j)),
                  pl.BlockSpec((tf, D), lambda i, j: (j, 0))],
        out_specs=pl.BlockSpec((tm, D), lambda i, j: (i, 0)),
        out_shape=jax.ShapeDtypeStruct((T, D), F32),
        compiler_params=_params("parallel", "arbitrary"),
        name="relu2_mlp",
    )(h, wu, wd)


def _ple_kernel(x_ref, p_ref, wg_ref, wp_ref, o_ref):
    x, p = x_ref[...], p_ref[...].astype(BF16)
    for cols in _col_slabs(o_ref.shape[1]):
        o_ref[:, cols] = _dot(p, wp_ref[:, cols]) * _sigmoid(_dot(x, wg_ref[:, cols]))


def _ple(xb, p, wg, wp, tm, tn):
    T, D = xb.shape
    P = p.shape[1]
    return pl.pallas_call(
        _ple_kernel,
        grid=(T // tm, D // tn),
        in_specs=[pl.BlockSpec((tm, D), lambda i, j: (i, 0)), pl.BlockSpec((tm, P), lambda i, j: (i, 0)),
                  pl.BlockSpec((D, tn), lambda i, j: (0, j)), pl.BlockSpec((P, tn), lambda i, j: (0, j))],
        out_specs=pl.BlockSpec((tm, tn), lambda i, j: (i, j)),
        out_shape=jax.ShapeDtypeStruct((T, D), F32),
        compiler_params=_params("parallel", "arbitrary"),
        name="ple_gate",
    )(xb, p, wg, wp)


def _rope_tables(seq):
    half = HEAD_DIM // 2
    inv_freq = ROPE_THETA ** (-jnp.arange(0, HEAD_DIM, 2, dtype=F32) / HEAD_DIM)
    ang = jnp.arange(seq, dtype=F32)[:, None] * inv_freq[None, :]
    cos, sin = jnp.cos(ang), jnp.sin(ang)
    assert cos.shape == (seq, half)
    return jnp.concatenate([cos, cos], axis=1), jnp.concatenate([-sin, sin], axis=1)


def kernel(x, p, norm_mix_pre, norm_mix_post, w_in, lb_fwd, lb_bwd, hgrn_norm, attn_sink, w_hgrn_proj,
           w_attn_proj, w_out, norm_mlp_pre, norm_mlp_post, w_mlp_up, w_mlp_down, w_ple, w_ple_gate, norm_ple):
    B, S, D = x.shape
    T = B * S
    depth = w_in.shape[0]
    HW = w_hgrn_proj.shape[1]
    AW = w_attn_proj.shape[1]
    KVW = AW // GQA_GROUP
    assert S % HGRN_ROWS == 0 and S % ATTN_BLOCK == 0 and hgrn_norm.shape[-1] == HEAD_DIM
    assert w_in.shape[2] == 5 * HW + AW + 2 * KVW + 2 * D

    tm = _pick(S, 1024, 512, 256, 128)
    tm_mlp = _pick(T, 512, 256, 128)
    tr = _pick(T, 256, 128)
    cos_t, sin_t = _rope_tables(S)
    vec = lambda a: a.reshape(1, -1)

    xf = x.reshape(T, D)
    for i in range(depth):
        w_in_b = w_in[i].astype(BF16)
        off_aq, off_av, off_gate = 5 * HW, 5 * HW + AW + KVW, 5 * HW + AW + 2 * KVW

        h = _norm_cast(xf, vec(norm_mix_pre[i]), tr)
        tn = lambda off, n: _pick(math.gcd(off, n), 1024, 512, 256, 128)
        mm = functools.partial(_mm, h, w_in_b, tm=tm)
        prep = _hgrn_prep(h, w_in_b, jnp.concatenate([lb_fwd, lb_bwd], axis=1), HW, i, tm, 2 * HEAD_DIM)
        tn_r = tn(off_aq, AW + KVW)
        nrb = S // tm if S % tm == 0 else None
        assert nrb is not None
        rope_spec = pl.BlockSpec((tm, HEAD_DIM), lambda r, j: (r % nrb, 0))
        qk = mm(off_aq, AW + KVW, functools.partial(_epi_rotary, n_query_tiles=AW // tn_r), BF16, tn=tn_r,
                extras=(cos_t, sin_t),
                extra_specs=(rope_spec, rope_spec), name="in_qk")
        av = mm(off_av, KVW, _epi_identity, BF16, tn=tn(off_av, KVW), name="in_v")
        tn_g = tn(off_gate, 2 * D)
        later = [w_out[i], w_mlp_up[i], w_mlp_down[i], w_ple_gate[i]]
        steps = (T // tm) * (2 * D // tn_g)
        on_side = [a for a in later if _cast_rows_per_step(a, steps) is not None]
        gates, side = mm(off_gate, 2 * D, _epi_sigmoid, BF16, tn=tn_g, casts=on_side, name="in_gates") \
            if on_side else (mm(off_gate, 2 * D, _epi_sigmoid, BF16, tn=tn_g, name="in_gates"), [])
        side = iter(side)
        w_o, w_up, w_dn, w_pg = [
            next(side) if _cast_rows_per_step(a, steps) is not None else a.astype(BF16) for a in later]
        w_hp, w_ap = w_hgrn_proj[i].astype(BF16), w_attn_proj[i].astype(BF16)

        o_h = _hgrn(prep, vec(hgrn_norm[i]), B, S)
        o_a = _attn(qk, av, attn_sink[i], B, S, AW, _pick(S, 512, 256, 128))
        y = _merge(o_h, o_a, w_hp, w_ap, gates, tm, _pick(D, 1024, 512, 256, 128))
        mix = _mm(y, w_o, 0, D, _epi_identity, F32, tm, _pick(D, 1024, 512, 256, 128), name="w_out")
        xf, h2 = _resid_norm(xf, mix, vec(norm_mix_post[i]), vec(norm_mlp_pre[i]), tr, renorm=True)

        d = _mlp(h2, w_up, w_dn, tm_mlp, _pick(w_mlp_up.shape[2], 512, 256))
        xf, xb = _resid_norm(xf, d, vec(norm_mlp_post[i]), vec(norm_mlp_post[i]), tr, renorm=False)

        eg = _ple(xb, p[i].reshape(T, -1), w_pg, w_ple[i].astype(BF16), tm, _pick(D, 1024, 512, 256, 128))
        xf = _resid_final(xf, eg, vec(norm_ple[i]), tr)
    return xf.reshape(B, S, D)
```

```python
import functools
import math

import jax
import jax.numpy as jnp
from jax import lax
from jax.experimental import pallas as pl
from jax.experimental.pallas import tpu as pltpu

F32 = jnp.float32
BF16 = jnp.bfloat16

EPS = 1e-6
HEAD_DIM = 128
GQA_GROUP = 4
WINDOW = 128
ATTN_BLOCK = 128
CHUNK = 64
HGRN_ROWS = 256
HGRN_UNROLL = 4
HGRN_FINISH_ROWS = 1024
ROPE_THETA = 10000.0
MXU_COLS = 256
SUBLANES = 8
BF16_ROWS = 16
RESID_ROWS = 64
VMEM_LIMIT_BYTES = 60 * 1024 * 1024


def _params(*semantics):
    return pltpu.CompilerParams(dimension_semantics=semantics, vmem_limit_bytes=VMEM_LIMIT_BYTES)


def _pick(n, *cands):
    for c in cands:
        if n % c == 0:
            return c
    raise ValueError(f"no tile in {cands} divides {n}")


def _dot(a, b):
    return jnp.dot(a, b, preferred_element_type=F32)


def _dot_nt(a, b):
    return lax.dot_general(a, b, (((1,), (1,)), ((), ())), preferred_element_type=F32)


def _dot_tn(a, b):
    return lax.dot_general(a, b, (((0,), (0,)), ((), ())), preferred_element_type=F32)


def _sigmoid(x):
    return 1.0 / (1.0 + jnp.exp(-x))


def _rms(x, w):
    return x * lax.rsqrt(jnp.mean(x * x, axis=-1, keepdims=True) + EPS) * w


def _norm_cast_kernel(x_ref, w_ref, o_ref):
    o_ref[...] = _rms(x_ref[...], w_ref[...]).astype(o_ref.dtype)


def _norm_cast(x, w, tm):
    T, D = x.shape
    return pl.pallas_call(
        _norm_cast_kernel,
        grid=(T // tm,),
        in_specs=[pl.BlockSpec((tm, D), lambda i: (i, 0)), pl.BlockSpec((1, D), lambda i: (0, 0))],
        out_specs=pl.BlockSpec((tm, D), lambda i: (i, 0)),
        out_shape=jax.ShapeDtypeStruct((T, D), BF16),
        compiler_params=_params("parallel"),
        name="norm_cast",
    )(x, w)


def _resid_norm_kernel(x_ref, d_ref, w_ref, w2_ref, o_ref, h_ref, *, renorm):
    y = x_ref[...] + _rms(d_ref[...], w_ref[...])
    o_ref[...] = y
    h_ref[...] = (_rms(y, w2_ref[...]) if renorm else y).astype(h_ref.dtype)


def _resid_norm(x, d, w, w2, tm, renorm):
    T, D = x.shape
    row = pl.BlockSpec((tm, D), lambda i: (i, 0))
    vec = pl.BlockSpec((1, D), lambda i: (0, 0))
    return pl.pallas_call(
        functools.partial(_resid_norm_kernel, renorm=renorm),
        grid=(T // tm,),
        in_specs=[row, row, vec, vec],
        out_specs=[row, row],
        out_shape=[jax.ShapeDtypeStruct((T, D), F32), jax.ShapeDtypeStruct((T, D), BF16)],
        compiler_params=_params("parallel"),
        name="resid_norm",
    )(x, d, w, w2)


def _resid_final_kernel(x_ref, d_ref, w_ref, o_ref):
    o_ref[...] = x_ref[...] + _rms(d_ref[...], w_ref[...])


def _resid_final(x, d, w, tm):
    T, D = x.shape
    row = pl.BlockSpec((tm, D), lambda i: (i, 0))
    vec = pl.BlockSpec((1, D), lambda i: (0, 0))
    return pl.pallas_call(
        _resid_final_kernel,
        grid=(T // tm,),
        in_specs=[row, row, vec],
        out_specs=row,
        out_shape=jax.ShapeDtypeStruct((T, D), F32),
        compiler_params=_params("parallel"),
        name="resid_final",
    )(x, d, w)


def _col_slabs(width):
    step = min(width, MXU_COLS)
    return [slice(c, c + step) for c in range(0, width, step)]


def _mm_kernel(lhs_ref, w_ref, *rest, epilogue, n_casts):
    n_extra = len(rest) - 2 * n_casts - 1
    extra_refs, cast_in = rest[:n_extra], rest[n_extra:n_extra + n_casts]
    o_ref, cast_out = rest[n_extra + n_casts], rest[n_extra + n_casts + 1:]
    for src, dst in zip(cast_in, cast_out):
        dst[...] = src[...].astype(dst.dtype)
    lhs = lhs_ref[...]
    for cols in _col_slabs(o_ref.shape[1]):
        o_ref[:, cols] = epilogue(_dot(lhs, w_ref[:, cols]), *extra_refs).astype(o_ref.dtype)


def _cast_rows_per_step(a, steps):
    rows = a.shape[0] // steps
    return rows if rows * steps == a.shape[0] and rows % BF16_ROWS == 0 else None


def _mm(lhs, w, col_off, n_cols, epilogue, out_dtype, tm, tn, extras=(), extra_specs=(), casts=(), name="mm"):
    T, K = lhs.shape
    off = col_off // tn
    nj = n_cols // tn
    steps = (T // tm) * nj
    cast_specs = [pl.BlockSpec((_cast_rows_per_step(a, steps), a.shape[1]), lambda i, j: (i * nj + j, 0))
                  for a in casts]
    res = pl.pallas_call(
        functools.partial(_mm_kernel, epilogue=epilogue, n_casts=len(casts)),
        grid=(T // tm, nj),
        in_specs=[pl.BlockSpec((tm, K), lambda i, j: (i, 0)),
                  pl.BlockSpec((K, tn), lambda i, j: (0, j + off))] + list(extra_specs) + cast_specs,
        out_specs=[pl.BlockSpec((tm, tn), lambda i, j: (i, j))] + cast_specs,
        out_shape=[jax.ShapeDtypeStruct((T, n_cols), out_dtype)]
        + [jax.ShapeDtypeStruct(a.shape, BF16) for a in casts],
        compiler_params=_params("parallel", "arbitrary"),
        name=name,
    )(lhs, w, *extras, *casts)
    return (res[0], list(res[1:])) if casts else res[0]


def _epi_sigmoid(acc):
    return _sigmoid(acc)


def _epi_identity(acc):
    return acc


def _epi_rotary(acc, cos_ref, sin_ref, *, n_query_tiles):
    scale = jnp.where(pl.program_id(1) < n_query_tiles, HEAD_DIM ** -0.5, 1.0)
    cos, sin = cos_ref[...] * scale, sin_ref[...] * scale
    heads = []
    for h in range(acc.shape[1] // HEAD_DIM):
        xh = acc[:, h * HEAD_DIM:(h + 1) * HEAD_DIM]
        heads.append(xh * cos + pltpu.roll(xh, HEAD_DIM // 2, 1) * sin)
    return jnp.concatenate(heads, axis=1)


def _chunk_cumprod(x, reverse):
    n = SUBLANES
    rid = lax.broadcasted_iota(jnp.int32, (n, x.shape[1]), 0)
    slabs = []
    for g in range(x.shape[0] // n):
        y = x[g * n:(g + 1) * n]
        s = 1
        while s < n:
            if reverse:
                y = y * jnp.where(rid < n - s, pltpu.roll(y, n - s, 0), 1.0)
            else:
                y = y * jnp.where(rid >= s, pltpu.roll(y, s, 0), 1.0)
            s *= 2
        slabs.append(y)
    order = range(len(slabs) - 1, -1, -1) if reverse else range(len(slabs))
    total = None
    for g in order:
        if total is not None:
            slabs[g] = slabs[g] * total
        total = slabs[g][0:1] if reverse else slabs[g][n - 1:n]
    return jnp.concatenate(slabs, axis=0)


def _hgrn_prep_kernel(h_ref, wq_ref, wff_ref, wfb_ref, wv_ref, wg_ref, lbf_ref, lbb_ref,
                      qf_ref, kf_ref, ktf_ref, qb_ref, kb_ref, ktb_ref, v_ref, og_ref, df_ref, db_ref, *, layer):
    h = h_ref[...]
    rows, width = qf_ref.shape
    xf = [_dot(h, wff_ref[...]), _dot(h, wfb_ref[...])]
    xq = _dot(h, wq_ref[...])
    xg = _dot(h, wg_ref[...])
    og_ref[...] = (xg * _sigmoid(xg)).astype(og_ref.dtype)
    v_ref[...] = _dot(h, wv_ref[...]).astype(v_ref.dtype)
    lbs = []
    for lb_ref in (lbf_ref, lbb_ref):
        lbp = lb_ref[...]
        e = jnp.exp(lbp - jnp.max(lbp, axis=0, keepdims=True))
        lbs.append(jnp.sum(e[:layer + 1], axis=0, keepdims=True) / jnp.sum(e, axis=0, keepdims=True))
    for c in range(rows // CHUNK):
        rs = slice(c * CHUNK, (c + 1) * CHUNK)
        q = xq[rs] * _sigmoid(xq[rs]) * (HEAD_DIM ** -0.5)
        for x, lb, reverse, qo, ko, kto, do in (
                (xf[0], lbs[0], False, qf_ref, kf_ref, ktf_ref, df_ref),
                (xf[1], lbs[1], True, qb_ref, kb_ref, ktb_ref, db_ref)):
            f = lb + (1.0 - lb) * _sigmoid(x[rs])
            eb = _chunk_cumprod(f, reverse)
            decay = eb[0:1, :] if reverse else eb[CHUNK - 1:CHUNK, :]
            k_hat = (1.0 - f) / eb
            qo[rs, :] = (q * eb).astype(qo.dtype)
            ko[rs, :] = k_hat.astype(ko.dtype)
            kto[rs, :] = (k_hat * decay).astype(kto.dtype)
            do[c:c + 1, :] = decay


def _hgrn_prep(h, w, lbp, hw, layer, tm, tn):
    T, K = h.shape
    nseg = hw // tn
    wspec = lambda seg: pl.BlockSpec((K, tn), lambda i, j: (0, seg * nseg + j))
    lbspec = lambda seg: pl.BlockSpec((lbp.shape[0], tn), lambda i, j: (0, seg * nseg + j))
    ospec = pl.BlockSpec((tm, tn), lambda i, j: (i, j))
    dspec = pl.BlockSpec((tm // CHUNK, tn), lambda i, j: (i, j))
    act = jax.ShapeDtypeStruct((T, hw), BF16)
    dec = jax.ShapeDtypeStruct((T // CHUNK, hw), F32)
    return pl.pallas_call(
        functools.partial(_hgrn_prep_kernel, layer=layer),
        grid=(T // tm, nseg),
        in_specs=[pl.BlockSpec((tm, K), lambda i, j: (i, 0)), wspec(0), wspec(1), wspec(2), wspec(3), wspec(4),
                  lbspec(0), lbspec(1)],
        out_specs=[ospec] * 8 + [dspec] * 2,
        out_shape=[act] * 8 + [dec] * 2,
        compiler_params=_params("parallel", "arbitrary"),
        name="in_hgrn",
    )(h, w, w, w, w, w, lbp, lbp)


def _hgrn_kernel(qf_ref, kf_ref, ktf_ref, qb_ref, kb_ref, ktb_ref, v_ref, og_ref, df_ref, db_ref, nw_ref, o_ref,
                 accf_ref, accb_ref, mf_ref, mb_ref, sf_ref, sb_ref, *, seq):
    R = HGRN_ROWS
    nsb = seq // R
    ncs = R // CHUNK

    row = lax.broadcasted_iota(jnp.int32, (R, R), 0)
    col = lax.broadcasted_iota(jnp.int32, (R, R), 1)
    same = (row // CHUNK) == (col // CHUNK)
    mf_ref[...] = jnp.where(same & (col <= row), 1.0, 0.0).astype(BF16)
    mb_ref[...] = jnp.where(same & (col >= row), 1.0, 0.0).astype(BF16)
    sf_ref[...] = jnp.zeros_like(sf_ref)
    sb_ref[...] = jnp.zeros_like(sb_ref)

    U = HGRN_UNROLL if nsb % HGRN_UNROLL == 0 else 1
    W = U * R
    nw = U * ncs
    blocks = [slice(u * R, (u + 1) * R) for u in range(U)]
    chunks = [slice(c * CHUNK, (c + 1) * CHUNK) for c in range(nw)]

    def body(t, carry):
        dirs = ((t, qf_ref, kf_ref, ktf_ref, df_ref, mf_ref, sf_ref, accf_ref, range(nw)),
                (nsb // U - 1 - t, qb_ref, kb_ref, ktb_ref, db_ref, mb_ref, sb_ref, accb_ref, range(nw - 1, -1, -1)))
        r0s = [pl.multiple_of(d[0] * W, W) for d in dirs]
        qs = [d[1][pl.ds(r0, W), :] for d, r0 in zip(dirs, r0s)]
        vs = [v_ref[pl.ds(r0, W), :] for r0 in r0s]
        scores = []
        for d, r0, q in zip(dirs, r0s, qs):
            k = d[2][pl.ds(r0, W), :]
            scores.append([_dot_nt(q[bl], k[bl]) for bl in blocks])
        updates = []
        for d, r0, v in zip(dirs, r0s, vs):
            kt = d[3][pl.ds(r0, W), :]
            updates.append([_dot_tn(v[sl], kt[sl]) for sl in chunks])
        entering = []
        for d, upd in zip(dirs, updates):
            idx, d_ref, s_ref = d[0], d[4], d[6]
            st = s_ref[...]
            ent = [None] * nw
            for c in d[8]:
                ent[c] = st.astype(BF16)
                st = st * d_ref[pl.ds(idx * nw + c, 1), :] + upd[c]
            s_ref[...] = st
            entering.append(ent)
        inter = [[_dot_nt(q[sl], ent[c]) for c, sl in enumerate(chunks)] for q, ent in zip(qs, entering)]
        for d, r0, sc, v, o_inter in zip(dirs, r0s, scores, vs, inter):
            mask = d[5][...] > 0
            for u, bl in enumerate(blocks):
                o = _dot(jnp.where(mask, sc[u], 0.0).astype(BF16), v[bl])
                d[7][pl.ds(r0 + u * R, R), :] = o + jnp.concatenate(o_inter[u * ncs:(u + 1) * ncs], axis=0)
        return carry

    lax.fori_loop(0, nsb // U, body, 0)

    FR = HGRN_FINISH_ROWS if seq % HGRN_FINISH_ROWS == 0 else R

    def finish(t, carry):
        r0 = pl.multiple_of(t * FR, FR)
        o = accf_ref[pl.ds(r0, FR), :] + accb_ref[pl.ds(r0, FR), :]
        y = _rms(o, nw_ref[...]) * og_ref[pl.ds(r0, FR), :].astype(F32)
        o_ref[pl.ds(r0, FR), :] = y.astype(o_ref.dtype)
        return carry

    lax.fori_loop(0, seq // FR, finish, 0)


def _hgrn(prep, nw, batch, seq):
    *acts, df, db = prep
    T, HW = acts[0].shape
    H = HW // HEAD_DIM
    blk = pl.BlockSpec((seq, HEAD_DIM), lambda b, h: (b, h))
    dblk = pl.BlockSpec((seq // CHUNK, HEAD_DIM), lambda b, h: (b, h))
    return pl.pallas_call(
        functools.partial(_hgrn_kernel, seq=seq),
        grid=(batch, H),
        in_specs=[blk] * 8 + [dblk] * 2 + [pl.BlockSpec((1, HEAD_DIM), lambda b, h: (0, 0))],
        out_specs=blk,
        out_shape=jax.ShapeDtypeStruct((T, HW), BF16),
        scratch_shapes=[pltpu.VMEM((seq, HEAD_DIM), F32), pltpu.VMEM((seq, HEAD_DIM), F32),
                        pltpu.VMEM((HGRN_ROWS, HGRN_ROWS), BF16), pltpu.VMEM((HGRN_ROWS, HGRN_ROWS), BF16),
                        pltpu.VMEM((HEAD_DIM, HEAD_DIM), F32), pltpu.VMEM((HEAD_DIM, HEAD_DIM), F32)],
        compiler_params=_params("parallel", "parallel"),
        name="hgrn2",
    )(*acts, df, db, nw)


def _attn_kernel(sink_ref, q_ref, kp_ref, kc_ref, kn_ref, vp_ref, vc_ref, vn_ref, o_ref, *, seq):
    G, D, BLK = GQA_GROUP, HEAD_DIM, ATTN_BLOCK
    rows = q_ref.shape[0]
    kvh = pl.program_id(1)
    n = pl.program_id(2)
    k = jnp.concatenate([kp_ref[...], kc_ref[...], kn_ref[...]], axis=0)
    v = jnp.concatenate([vp_ref[...], vc_ref[...], vn_ref[...]], axis=0)
    v1 = jnp.concatenate([v, jnp.ones_like(v)], axis=1)
    r = lax.broadcasted_iota(jnp.int32, (G * BLK, 3 * BLK), 0) % BLK
    c = lax.broadcasted_iota(jnp.int32, (G * BLK, 3 * BLK), 1)
    band = jnp.where(jnp.abs(c - BLK - r) <= WINDOW, 0.0, -jnp.inf)
    kpos = n * rows - BLK + lax.broadcasted_iota(jnp.int32, (1, rows + 2 * BLK), 1)
    inside = jnp.where((kpos >= 0) & (kpos < seq), 0.0, -jnp.inf)
    sk = jnp.concatenate([jnp.full((BLK, BLK), sink_ref[kvh * G + h], F32) for h in range(G)], axis=0)
    nsub = rows // BLK
    scores = []
    for j in range(nsub):
        q = q_ref[j * BLK:(j + 1) * BLK, :]
        q4 = jnp.concatenate([q[:, h * D:(h + 1) * D] for h in range(G)], axis=0)
        scores.append(_dot_nt(q4, k[j * BLK:(j + 3) * BLK]))
    probs, sink_terms = [], []
    for j in range(nsub):
        s = scores[j] + band
        if j == 0 or j == nsub - 1:
            s = s + inside[:, j * BLK:(j + 3) * BLK]
        m = jnp.maximum(jnp.broadcast_to(jnp.max(s, axis=-1, keepdims=True), (G * BLK, BLK)), sk)
        probs.append(jnp.concatenate(
            [jnp.exp(s[:, i * BLK:(i + 1) * BLK] - m) for i in range(3)], axis=1).astype(BF16))
        sink_terms.append(jnp.exp(sk - m))
    for j in range(nsub):
        pv = _dot(probs[j], v1[j * BLK:(j + 3) * BLK])
        o = pv[:, :D] / (pv[:, D:] + sink_terms[j])
        o_ref[j * BLK:(j + 1) * BLK, :] = jnp.concatenate(
            [o[h * BLK:(h + 1) * BLK] for h in range(G)], axis=1).astype(o_ref.dtype)


def _attn(qk, v, sink, batch, seq, n_q_cols, rows):
    T = qk.shape[0]
    G, D, BLK = GQA_GROUP, HEAD_DIM, ATTN_BLOCK
    kvh = v.shape[1] // D
    nb = seq // BLK
    nq = seq // rows
    sub = rows // BLK
    k_off = n_q_cols // D
    qspec = pl.BlockSpec((rows, G * D), lambda b, h, n: (b * nq + n, h))
    prev = lambda b, n: b * nb + jnp.maximum(n * sub - 1, 0)
    nxt = lambda b, n: b * nb + jnp.minimum((n + 1) * sub, nb - 1)
    edge = lambda f, off: pl.BlockSpec((BLK, D), lambda b, h, n: (f(b, n), off + h))
    cur = lambda off: pl.BlockSpec((rows, D), lambda b, h, n: (b * nq + n, off + h))
    return pl.pallas_call(
        functools.partial(_attn_kernel, seq=seq),
        grid=(batch, kvh, nq),
        in_specs=[pl.BlockSpec(memory_space=pltpu.SMEM), qspec,
                  edge(prev, k_off), cur(k_off), edge(nxt, k_off), edge(prev, 0), cur(0), edge(nxt, 0)],
        out_specs=qspec,
        out_shape=jax.ShapeDtypeStruct((T, n_q_cols), BF16),
        compiler_params=_params("parallel", "parallel", "arbitrary"),
        name="swa_sink",
    )(sink, qk, qk, qk, qk, v, v, v)


def _merge_kernel(oh_ref, oa_ref, wh_ref, wa_ref, ga_ref, gb_ref, o_ref):
    oh, oa = oh_ref[...], oa_ref[...]
    for cols in _col_slabs(o_ref.shape[1]):
        ya = _dot(oh, wh_ref[:, cols])
        yb = _dot(oa, wa_ref[:, cols])
        o_ref[:, cols] = (ga_ref[:, cols].astype(F32) * ya + gb_ref[:, cols].astype(F32) * yb).astype(o_ref.dtype)


def _merge(oh, oa, wh, wa, gates, tm, tn):
    T, KH = oh.shape
    KA = oa.shape[1]
    D = wh.shape[1]
    nj = D // tn
    return pl.pallas_call(
        _merge_kernel,
        grid=(T // tm, nj),
        in_specs=[pl.BlockSpec((tm, KH), lambda i, j: (i, 0)), pl.BlockSpec((tm, KA), lambda i, j: (i, 0)),
                  pl.BlockSpec((KH, tn), lambda i, j: (0, j)), pl.BlockSpec((KA, tn), lambda i, j: (0, j)),
                  pl.BlockSpec((tm, tn), lambda i, j: (i, j)), pl.BlockSpec((tm, tn), lambda i, j: (i, j + nj))],
        out_specs=pl.BlockSpec((tm, tn), lambda i, j: (i, j)),
        out_shape=jax.ShapeDtypeStruct((T, D), BF16),
        compiler_params=_params("parallel", "arbitrary"),
        name="gated_merge",
    )(oh, oa, wh, wa, gates, gates)


def _mlp_kernel(h_ref, wu_ref, wd_ref, x_ref, g_ref, o_ref, ob_ref):
    j = pl.program_id(1)

    @pl.when(j == 0)
    def _():
        o_ref[...] = jnp.zeros_like(o_ref)

    u = jnp.square(jnp.maximum(_dot(h_ref[...], wu_ref[...]), 0.0)).astype(BF16)
    o_ref[...] += _dot(u, wd_ref[...])

    @pl.when(j == pl.num_programs(1) - 1)
    def _():
        g = g_ref[...]

        def rows(t, carry):
            rs = pl.ds(pl.multiple_of(t * RESID_ROWS, RESID_ROWS), RESID_ROWS)
            y = x_ref[rs, :] + _rms(o_ref[rs, :], g)
            o_ref[rs, :] = y
            ob_ref[rs, :] = y.astype(ob_ref.dtype)
            return carry

        lax.fori_loop(0, o_ref.shape[0] // RESID_ROWS, rows, 0)


def _mlp(h, wu, wd, x, g, tm, tf):
    T, D = h.shape
    FF = wu.shape[1]
    row = lambda i, j: (i, 0)
    return pl.pallas_call(
        _mlp_kernel,
        grid=(T // tm, FF // tf),
        in_specs=[pl.BlockSpec((tm, D), row),
                  pl.BlockSpec((D, tf), lambda i, j: (0, j)),
                  pl.BlockSpec((tf, D), lambda i, j: (j, 0)),
                  pl.BlockSpec((tm, D), row, pipeline_mode=pl.Buffered(1)),
                  pl.BlockSpec((1, D), lambda i, j: (0, 0))],
        out_specs=[pl.BlockSpec((tm, D), row), pl.BlockSpec((tm, D), row)],
        out_shape=[jax.ShapeDtypeStruct((T, D), F32), jax.ShapeDtypeStruct((T, D), BF16)],
        compiler_params=_params("parallel", "arbitrary"),
        name="relu2_mlp",
    )(h, wu, wd, x, g)


def _ple_kernel(x_ref, p_ref, wg_ref, wp_ref, o_ref):
    x, p = x_ref[...], p_ref[...].astype(BF16)
    for cols in _col_slabs(o_ref.shape[1]):
        o_ref[:, cols] = _dot(p, wp_ref[:, cols]) * _sigmoid(_dot(x, wg_ref[:, cols]))


def _ple(xb, p, wg, wp, tm, tn):
    T, D = xb.shape
    P = p.shape[1]
    return pl.pallas_call(
        _ple_kernel,
        grid=(T // tm, D // tn),
        in_specs=[pl.BlockSpec((tm, D), lambda i, j: (i, 0)), pl.BlockSpec((tm, P), lambda i, j: (i, 0)),
                  pl.BlockSpec((D, tn), lambda i, j: (0, j)), pl.BlockSpec((P, tn), lambda i, j: (0, j))],
        out_specs=pl.BlockSpec((tm, tn), lambda i, j: (i, j)),
        out_shape=jax.ShapeDtypeStruct((T, D), F32),
        compiler_params=_params("parallel", "arbitrary"),
        name="ple_gate",
    )(xb, p, wg, wp)


def _rope_tables(seq):
    half = HEAD_DIM // 2
    inv_freq = ROPE_THETA ** (-jnp.arange(0, HEAD_DIM, 2, dtype=F32) / HEAD_DIM)
    ang = jnp.arange(seq, dtype=F32)[:, None] * inv_freq[None, :]
    cos, sin = jnp.cos(ang), jnp.sin(ang)
    assert cos.shape == (seq, half)
    return jnp.concatenate([cos, cos], axis=1), jnp.concatenate([-sin, sin], axis=1)


def kernel(x, p, norm_mix_pre, norm_mix_post, w_in, lb_fwd, lb_bwd, hgrn_norm, attn_sink, w_hgrn_proj,
           w_attn_proj, w_out, norm_mlp_pre, norm_mlp_post, w_mlp_up, w_mlp_down, w_ple, w_ple_gate, norm_ple):
    B, S, D = x.shape
    T = B * S
    depth = w_in.shape[0]
    HW = w_hgrn_proj.shape[1]
    AW = w_attn_proj.shape[1]
    KVW = AW // GQA_GROUP
    assert S % HGRN_ROWS == 0 and S % ATTN_BLOCK == 0 and hgrn_norm.shape[-1] == HEAD_DIM
    assert w_in.shape[2] == 5 * HW + AW + 2 * KVW + 2 * D

    tm = _pick(S, 1024, 512, 256, 128)
    tm_mlp = _pick(T, 512, 256, 128)
    tr = _pick(T, 256, 128)
    cos_t, sin_t = _rope_tables(S)
    vec = lambda a: a.reshape(1, -1)

    xf = x.reshape(T, D)
    for i in range(depth):
        w_in_b = w_in[i].astype(BF16)
        off_aq, off_av, off_gate = 5 * HW, 5 * HW + AW + KVW, 5 * HW + AW + 2 * KVW

        h = _norm_cast(xf, vec(norm_mix_pre[i]), tr)
        tn = lambda off, n: _pick(math.gcd(off, n), 1024, 512, 256, 128)
        mm = functools.partial(_mm, h, w_in_b, tm=tm)
        prep = _hgrn_prep(h, w_in_b, jnp.concatenate([lb_fwd, lb_bwd], axis=1), HW, i, tm, 2 * HEAD_DIM)
        tn_r = tn(off_aq, AW + KVW)
        nrb = S // tm if S % tm == 0 else None
        assert nrb is not None
        rope_spec = pl.BlockSpec((tm, HEAD_DIM), lambda r, j: (r % nrb, 0))
        qk = mm(off_aq, AW + KVW, functools.partial(_epi_rotary, n_query_tiles=AW // tn_r), BF16, tn=tn_r,
                extras=(cos_t, sin_t),
                extra_specs=(rope_spec, rope_spec), name="in_qk")
        av = mm(off_av, KVW, _epi_identity, BF16, tn=tn(off_av, KVW), name="in_v")
        tn_g = tn(off_gate, 2 * D)
        later = [w_out[i], w_mlp_up[i], w_mlp_down[i], w_ple_gate[i]]
        steps = (T // tm) * (2 * D // tn_g)
        on_side = [a for a in later if _cast_rows_per_step(a, steps) is not None]
        gates, side = mm(off_gate, 2 * D, _epi_sigmoid, BF16, tn=tn_g, casts=on_side, name="in_gates") \
            if on_side else (mm(off_gate, 2 * D, _epi_sigmoid, BF16, tn=tn_g, name="in_gates"), [])
        side = iter(side)
        w_o, w_up, w_dn, w_pg = [
            next(side) if _cast_rows_per_step(a, steps) is not None else a.astype(BF16) for a in later]
        w_hp, w_ap = w_hgrn_proj[i].astype(BF16), w_attn_proj[i].astype(BF16)

        o_h = _hgrn(prep, vec(hgrn_norm[i]), B, S)
        o_a = _attn(qk, av, attn_sink[i], B, S, AW, _pick(S, 512, 256, 128))
        y = _merge(o_h, o_a, w_hp, w_ap, gates, tm, _pick(D, 1024, 512, 256, 128))
        mix = _mm(y, w_o, 0, D, _epi_identity, F32, tm, _pick(D, 1024, 512, 256, 128), name="w_out")
        xf, h2 = _resid_norm(xf, mix, vec(norm_mix_post[i]), vec(norm_mlp_pre[i]), tr, renorm=True)

        xf, xb = _mlp(h2, w_up, w_dn, xf, vec(norm_mlp_post[i]), tm_mlp, _pick(w_mlp_up.shape[2], 512, 256))

        eg = _ple(xb, p[i].reshape(T, -1), w_pg, w_ple[i].astype(BF16), tm, _pick(D, 1024, 512, 256, 128))
        xf = _resid_final(xf, eg, vec(norm_ple[i]), tr)
    return xf.reshape(B, S, D)
```

```python
import functools
import math

import jax
import jax.numpy as jnp
from jax import lax
from jax.experimental import pallas as pl
from jax.experimental.pallas import tpu as pltpu

F32 = jnp.float32
BF16 = jnp.bfloat16

EPS = 1e-6
HEAD_DIM = 128
GQA_GROUP = 4
WINDOW = 128
ATTN_BLOCK = 128
CHUNK = 64
HGRN_ROWS = 256
HGRN_UNROLL = 4
ROPE_THETA = 10000.0
MXU_COLS = 256
SUBLANES = 8
BF16_ROWS = 16
VMEM_LIMIT_BYTES = 60 * 1024 * 1024


def _params(*semantics):
    return pltpu.CompilerParams(dimension_semantics=semantics, vmem_limit_bytes=VMEM_LIMIT_BYTES)


def _pick(n, *cands):
    for c in cands:
        if n % c == 0:
            return c
    raise ValueError(f"no tile in {cands} divides {n}")


def _dot(a, b):
    return jnp.dot(a, b, preferred_element_type=F32)


def _dot_nt(a, b):
    return lax.dot_general(a, b, (((1,), (1,)), ((), ())), preferred_element_type=F32)


def _dot_tn(a, b):
    return lax.dot_general(a, b, (((0,), (0,)), ((), ())), preferred_element_type=F32)


def _sigmoid(x):
    return 1.0 / (1.0 + jnp.exp(-x))


def _rms(x, w):
    return x * lax.rsqrt(jnp.mean(x * x, axis=-1, keepdims=True) + EPS) * w


def _norm_cast_kernel(x_ref, w_ref, o_ref):
    o_ref[...] = _rms(x_ref[...], w_ref[...]).astype(o_ref.dtype)


def _norm_cast(x, w, tm):
    T, D = x.shape
    return pl.pallas_call(
        _norm_cast_kernel,
        grid=(T // tm,),
        in_specs=[pl.BlockSpec((tm, D), lambda i: (i, 0)), pl.BlockSpec((1, D), lambda i: (0, 0))],
        out_specs=pl.BlockSpec((tm, D), lambda i: (i, 0)),
        out_shape=jax.ShapeDtypeStruct((T, D), BF16),
        compiler_params=_params("parallel"),
        name="norm_cast",
    )(x, w)


def _resid_norm_kernel(x_ref, d_ref, w_ref, w2_ref, o_ref, h_ref, *, renorm):
    y = x_ref[...] + _rms(d_ref[...], w_ref[...])
    o_ref[...] = y
    h_ref[...] = (_rms(y, w2_ref[...]) if renorm else y).astype(h_ref.dtype)


def _resid_norm(x, d, w, w2, tm, renorm):
    T, D = x.shape
    row = pl.BlockSpec((tm, D), lambda i: (i, 0))
    vec = pl.BlockSpec((1, D), lambda i: (0, 0))
    return pl.pallas_call(
        functools.partial(_resid_norm_kernel, renorm=renorm),
        grid=(T // tm,),
        in_specs=[row, row, vec, vec],
        out_specs=[row, row],
        out_shape=[jax.ShapeDtypeStruct((T, D), F32), jax.ShapeDtypeStruct((T, D), BF16)],
        compiler_params=_params("parallel"),
        name="resid_norm",
    )(x, d, w, w2)


def _resid_final_kernel(x_ref, d_ref, w_ref, o_ref):
    o_ref[...] = x_ref[...] + _rms(d_ref[...], w_ref[...])


def _resid_final(x, d, w, tm):
    T, D = x.shape
    row = pl.BlockSpec((tm, D), lambda i: (i, 0))
    vec = pl.BlockSpec((1, D), lambda i: (0, 0))
    return pl.pallas_call(
        _resid_final_kernel,
        grid=(T // tm,),
        in_specs=[row, row, vec],
        out_specs=row,
        out_shape=jax.ShapeDtypeStruct((T, D), F32),
        compiler_params=_params("parallel"),
        name="resid_final",
    )(x, d, w)


def _col_slabs(width):
    step = min(width, MXU_COLS)
    return [slice(c, c + step) for c in range(0, width, step)]


def _mm_kernel(lhs_ref, w_ref, *rest, epilogue, n_casts):
    n_extra = len(rest) - 2 * n_casts - 1
    extra_refs, cast_in = rest[:n_extra], rest[n_extra:n_extra + n_casts]
    o_ref, cast_out = rest[n_extra + n_casts], rest[n_extra + n_casts + 1:]
    lhs = lhs_ref[...]
    for cols in _col_slabs(o_ref.shape[1]):
        o_ref[:, cols] = epilogue(_dot(lhs, w_ref[:, cols]), cols.start, *extra_refs).astype(o_ref.dtype)
    for src, dst in zip(cast_in, cast_out):
        dst[...] = src[...].astype(dst.dtype)


def _cast_rows_per_step(a, steps):
    rows = a.shape[0] // steps
    return rows if rows * steps == a.shape[0] and rows % BF16_ROWS == 0 else None


def _mm(lhs, w, col_off, n_cols, epilogue, out_dtype, tm, tn, extras=(), extra_specs=(), casts=(), name="mm"):
    T, K = lhs.shape
    off = col_off // tn
    nj = n_cols // tn
    steps = (T // tm) * nj
    cast_specs = [pl.BlockSpec((_cast_rows_per_step(a, steps), a.shape[1]), lambda i, j: (i * nj + j, 0))
                  for a in casts]
    res = pl.pallas_call(
        functools.partial(_mm_kernel, epilogue=epilogue, n_casts=len(casts)),
        grid=(T // tm, nj),
        in_specs=[pl.BlockSpec((tm, K), lambda i, j: (i, 0)),
                  pl.BlockSpec((K, tn), lambda i, j: (0, j + off))] + list(extra_specs) + cast_specs,
        out_specs=[pl.BlockSpec((tm, tn), lambda i, j: (i, j))] + cast_specs,
        out_shape=[jax.ShapeDtypeStruct((T, n_cols), out_dtype)]
        + [jax.ShapeDtypeStruct(a.shape, BF16) for a in casts],
        compiler_params=_params("parallel", "arbitrary"),
        name=name,
    )(lhs, w, *extras, *casts)
    return (res[0], list(res[1:])) if casts else res[0]


def _epi_sigmoid(acc, col0):
    return _sigmoid(acc)


def _epi_identity(acc, col0):
    return acc


def _epi_qkv(acc, col0, cos_ref, sin_ref, *, tn, n_q, n_k):
    cos, sin = cos_ref[...], sin_ref[...]
    heads = []
    for h in range(acc.shape[1] // HEAD_DIM):
        col = pl.program_id(1) * tn + col0 + h * HEAD_DIM
        xh = acc[:, h * HEAD_DIM:(h + 1) * HEAD_DIM]
        rot = (xh * cos + pltpu.roll(xh, HEAD_DIM // 2, 1) * sin) * jnp.where(col < n_q, HEAD_DIM ** -0.5, 1.0)
        heads.append(jnp.where(col < n_q + n_k, rot, xh))
    return jnp.concatenate(heads, axis=1)


def _chunk_cumprod(x, reverse):
    n = SUBLANES
    rid = lax.broadcasted_iota(jnp.int32, (n, x.shape[1]), 0)
    slabs = []
    for g in range(x.shape[0] // n):
        y = x[g * n:(g + 1) * n]
        s = 1
        while s < n:
            if reverse:
                y = y * jnp.where(rid < n - s, pltpu.roll(y, n - s, 0), 1.0)
            else:
                y = y * jnp.where(rid >= s, pltpu.roll(y, s, 0), 1.0)
            s *= 2
        slabs.append(y)
    order = range(len(slabs) - 1, -1, -1) if reverse else range(len(slabs))
    total = None
    for g in order:
        if total is not None:
            slabs[g] = slabs[g] * total
        total = slabs[g][0:1] if reverse else slabs[g][n - 1:n]
    return jnp.concatenate(slabs, axis=0)


def _hgrn_prep_kernel(h_ref, wq_ref, wff_ref, wfb_ref, wv_ref, wg_ref, lbf_ref, lbb_ref,
                      qf_ref, kf_ref, ktf_ref, qb_ref, kb_ref, ktb_ref, v_ref, og_ref, df_ref, db_ref, *, layer):
    h = h_ref[...]
    rows, width = qf_ref.shape
    xf = [_dot(h, wff_ref[...]), _dot(h, wfb_ref[...])]
    xq = _dot(h, wq_ref[...])
    xg = _dot(h, wg_ref[...])
    og_ref[...] = (xg * _sigmoid(xg)).astype(og_ref.dtype)
    v_ref[...] = _dot(h, wv_ref[...]).astype(v_ref.dtype)
    lbs = []
    for lb_ref in (lbf_ref, lbb_ref):
        lbp = lb_ref[...]
        e = jnp.exp(lbp - jnp.max(lbp, axis=0, keepdims=True))
        lbs.append(jnp.sum(e[:layer + 1], axis=0, keepdims=True) / jnp.sum(e, axis=0, keepdims=True))
    for c in range(rows // CHUNK):
        rs = slice(c * CHUNK, (c + 1) * CHUNK)
        q = xq[rs] * _sigmoid(xq[rs]) * (HEAD_DIM ** -0.5)
        for x, lb, reverse, qo, ko, kto, do in (
                (xf[0], lbs[0], False, qf_ref, kf_ref, ktf_ref, df_ref),
                (xf[1], lbs[1], True, qb_ref, kb_ref, ktb_ref, db_ref)):
            f = lb + (1.0 - lb) * _sigmoid(x[rs])
            eb = _chunk_cumprod(f, reverse)
            decay = eb[0:1, :] if reverse else eb[CHUNK - 1:CHUNK, :]
            k_hat = (1.0 - f) / eb
            qo[rs, :] = (q * eb).astype(qo.dtype)
            ko[rs, :] = k_hat.astype(ko.dtype)
            kto[rs, :] = (k_hat * decay).astype(kto.dtype)
            do[c:c + 1, :] = decay


def _hgrn_prep(h, w, lbp, hw, layer, tm, tn):
    T, K = h.shape
    nseg = hw // tn
    wspec = lambda seg: pl.BlockSpec((K, tn), lambda i, j: (0, seg * nseg + j))
    lbspec = lambda seg: pl.BlockSpec((lbp.shape[0], tn), lambda i, j: (0, seg * nseg + j))
    ospec = pl.BlockSpec((tm, tn), lambda i, j: (i, j))
    dspec = pl.BlockSpec((tm // CHUNK, tn), lambda i, j: (i, j))
    act = jax.ShapeDtypeStruct((T, hw), BF16)
    dec = jax.ShapeDtypeStruct((T // CHUNK, hw), F32)
    return pl.pallas_call(
        functools.partial(_hgrn_prep_kernel, layer=layer),
        grid=(T // tm, nseg),
        in_specs=[pl.BlockSpec((tm, K), lambda i, j: (i, 0)), wspec(0), wspec(1), wspec(2), wspec(3), wspec(4),
                  lbspec(0), lbspec(1)],
        out_specs=[ospec] * 8 + [dspec] * 2,
        out_shape=[act] * 8 + [dec] * 2,
        compiler_params=_params("parallel", "arbitrary"),
        name="in_hgrn",
    )(h, w, w, w, w, w, lbp, lbp)


def _hgrn_kernel(qf_ref, kf_ref, ktf_ref, qb_ref, kb_ref, ktb_ref, v_ref, og_ref, df_ref, db_ref, nw_ref, o_ref,
                 accf_ref, accb_ref, mf_ref, mb_ref, sf_ref, sb_ref, *, seq):
    R = HGRN_ROWS
    nsb = seq // R
    ncs = R // CHUNK

    row = lax.broadcasted_iota(jnp.int32, (R, R), 0)
    col = lax.broadcasted_iota(jnp.int32, (R, R), 1)
    same = (row // CHUNK) == (col // CHUNK)
    mf_ref[...] = jnp.where(same & (col <= row), 1.0, 0.0).astype(BF16)
    mb_ref[...] = jnp.where(same & (col >= row), 1.0, 0.0).astype(BF16)
    sf_ref[...] = jnp.zeros_like(sf_ref)
    sb_ref[...] = jnp.zeros_like(sb_ref)

    U = HGRN_UNROLL if nsb % HGRN_UNROLL == 0 else 1
    W = U * R
    nw = U * ncs
    blocks = [slice(u * R, (u + 1) * R) for u in range(U)]
    chunks = [slice(c * CHUNK, (c + 1) * CHUNK) for c in range(nw)]

    nt = nsb // U

    def finish_rows(r0, n):
        o = accf_ref[pl.ds(r0, n), :] + accb_ref[pl.ds(r0, n), :]
        y = _rms(o, nw_ref[...]) * og_ref[pl.ds(r0, n), :].astype(F32)
        o_ref[pl.ds(r0, n), :] = y.astype(o_ref.dtype)

    def body(t, carry, finalize):
        dirs = ((t, qf_ref, kf_ref, ktf_ref, df_ref, mf_ref, sf_ref, accf_ref, range(nw)),
                (nt - 1 - t, qb_ref, kb_ref, ktb_ref, db_ref, mb_ref, sb_ref, accb_ref, range(nw - 1, -1, -1)))
        r0s = [pl.multiple_of(d[0] * W, W) for d in dirs]
        qs = [d[1][pl.ds(r0, W), :] for d, r0 in zip(dirs, r0s)]
        vs = [v_ref[pl.ds(r0, W), :] for r0 in r0s]
        scores = []
        for d, r0, q in zip(dirs, r0s, qs):
            k = d[2][pl.ds(r0, W), :]
            scores.append([_dot_nt(q[bl], k[bl]) for bl in blocks])
        updates = []
        for d, r0, v in zip(dirs, r0s, vs):
            kt = d[3][pl.ds(r0, W), :]
            updates.append([_dot_tn(v[sl], kt[sl]) for sl in chunks])
        entering = []
        for d, upd in zip(dirs, updates):
            idx, d_ref, s_ref = d[0], d[4], d[6]
            st = s_ref[...]
            ent = [None] * nw
            for c in d[8]:
                ent[c] = st.astype(BF16)
                st = st * d_ref[pl.ds(idx * nw + c, 1), :] + upd[c]
            s_ref[...] = st
            entering.append(ent)
        inter = [[_dot_nt(q[sl], ent[c]) for c, sl in enumerate(chunks)] for q, ent in zip(qs, entering)]
        for d, r0, sc, v, o_inter in zip(dirs, r0s, scores, vs, inter):
            mask = d[5][...] > 0
            for u, bl in enumerate(blocks):
                o = _dot(jnp.where(mask, sc[u], 0.0).astype(BF16), v[bl])
                d[7][pl.ds(r0 + u * R, R), :] = o + jnp.concatenate(o_inter[u * ncs:(u + 1) * ncs], axis=0)
        if finalize:
            for r0 in r0s:
                finish_rows(r0, W)
        return carry

    if nt % 2 == 0:
        lax.fori_loop(0, nt // 2, functools.partial(body, finalize=False), 0)
        lax.fori_loop(nt // 2, nt, functools.partial(body, finalize=True), 0)
    else:
        lax.fori_loop(0, nt, functools.partial(body, finalize=False), 0)

        def finish(t, carry):
            finish_rows(pl.multiple_of(t * W, W), W)
            return carry

        lax.fori_loop(0, nt, finish, 0)


def _hgrn(prep, nw, batch, seq):
    *acts, df, db = prep
    T, HW = acts[0].shape
    H = HW // HEAD_DIM
    blk = pl.BlockSpec((seq, HEAD_DIM), lambda b, h: (b, h))
    dblk = pl.BlockSpec((seq // CHUNK, HEAD_DIM), lambda b, h: (b, h))
    return pl.pallas_call(
        functools.partial(_hgrn_kernel, seq=seq),
        grid=(batch, H),
        in_specs=[blk] * 8 + [dblk] * 2 + [pl.BlockSpec((1, HEAD_DIM), lambda b, h: (0, 0))],
        out_specs=blk,
        out_shape=jax.ShapeDtypeStruct((T, HW), BF16),
        scratch_shapes=[pltpu.VMEM((seq, HEAD_DIM), F32), pltpu.VMEM((seq, HEAD_DIM), F32),
                        pltpu.VMEM((HGRN_ROWS, HGRN_ROWS), BF16), pltpu.VMEM((HGRN_ROWS, HGRN_ROWS), BF16),
                        pltpu.VMEM((HEAD_DIM, HEAD_DIM), F32), pltpu.VMEM((HEAD_DIM, HEAD_DIM), F32)],
        compiler_params=_params("parallel", "parallel"),
        name="hgrn2",
    )(*acts, df, db, nw)


def _attn_kernel(sink_ref, q_ref, kp_ref, kc_ref, kn_ref, vp_ref, vc_ref, vn_ref, o_ref, *, seq):
    G, D, BLK = GQA_GROUP, HEAD_DIM, ATTN_BLOCK
    rows = q_ref.shape[0]
    kvh = pl.program_id(1)
    n = pl.program_id(2)
    k = jnp.concatenate([kp_ref[...], kc_ref[...], kn_ref[...]], axis=0)
    v = jnp.concatenate([vp_ref[...], vc_ref[...], vn_ref[...]], axis=0)
    v1 = jnp.concatenate([v, jnp.ones_like(v)], axis=1)
    r = lax.broadcasted_iota(jnp.int32, (G * BLK, 3 * BLK), 0) % BLK
    c = lax.broadcasted_iota(jnp.int32, (G * BLK, 3 * BLK), 1)
    band = jnp.where(jnp.abs(c - BLK - r) <= WINDOW, 0.0, -jnp.inf)
    kpos = n * rows - BLK + lax.broadcasted_iota(jnp.int32, (1, rows + 2 * BLK), 1)
    inside = jnp.where((kpos >= 0) & (kpos < seq), 0.0, -jnp.inf)
    sk = jnp.concatenate([jnp.full((BLK, BLK), sink_ref[kvh * G + h], F32) for h in range(G)], axis=0)
    nsub = rows // BLK
    scores = []
    for j in range(nsub):
        q = q_ref[j * BLK:(j + 1) * BLK, :]
        q4 = jnp.concatenate([q[:, h * D:(h + 1) * D] for h in range(G)], axis=0)
        scores.append(_dot_nt(q4, k[j * BLK:(j + 3) * BLK]))
    probs, sink_terms = [], []
    for j in range(nsub):
        s = scores[j] + band
        if j == 0 or j == nsub - 1:
            s = s + inside[:, j * BLK:(j + 3) * BLK]
        m = jnp.maximum(jnp.broadcast_to(jnp.max(s, axis=-1, keepdims=True), (G * BLK, BLK)), sk)
        probs.append(jnp.concatenate(
            [jnp.exp(s[:, i * BLK:(i + 1) * BLK] - m) for i in range(3)], axis=1).astype(BF16))
        sink_terms.append(jnp.exp(sk - m))
    for j in range(nsub):
        pv = _dot(probs[j], v1[j * BLK:(j + 3) * BLK])
        o = pv[:, :D] / (pv[:, D:] + sink_terms[j])
        o_ref[j * BLK:(j + 1) * BLK, :] = jnp.concatenate(
            [o[h * BLK:(h + 1) * BLK] for h in range(G)], axis=1).astype(o_ref.dtype)


def _attn(qkv, sink, batch, seq, n_q_cols, n_kv_cols, rows):
    T = qkv.shape[0]
    G, D, BLK = GQA_GROUP, HEAD_DIM, ATTN_BLOCK
    kvh = n_kv_cols // D
    nb = seq // BLK
    nq = seq // rows
    sub = rows // BLK
    k_off = n_q_cols // D
    v_off = (n_q_cols + n_kv_cols) // D
    qspec = pl.BlockSpec((rows, G * D), lambda b, h, n: (b * nq + n, h))
    prev = lambda b, n: b * nb + jnp.maximum(n * sub - 1, 0)
    nxt = lambda b, n: b * nb + jnp.minimum((n + 1) * sub, nb - 1)
    edge = lambda f, off: pl.BlockSpec((BLK, D), lambda b, h, n: (f(b, n), off + h))
    cur = lambda off: pl.BlockSpec((rows, D), lambda b, h, n: (b * nq + n, off + h))
    return pl.pallas_call(
        functools.partial(_attn_kernel, seq=seq),
        grid=(batch, kvh, nq),
        in_specs=[pl.BlockSpec(memory_space=pltpu.SMEM), qspec,
                  edge(prev, k_off), cur(k_off), edge(nxt, k_off), edge(prev, v_off), cur(v_off), edge(nxt, v_off)],
        out_specs=qspec,
        out_shape=jax.ShapeDtypeStruct((T, n_q_cols), BF16),
        compiler_params=_params("parallel", "parallel", "arbitrary"),
        name="swa_sink",
    )(sink, *([qkv] * 7))


def _merge_kernel(oh_ref, oa_ref, wh_ref, wa_ref, ga_ref, gb_ref, o_ref):
    oh, oa = oh_ref[...], oa_ref[...]
    for cols in _col_slabs(o_ref.shape[1]):
        ya = _dot(oh, wh_ref[:, cols])
        yb = _dot(oa, wa_ref[:, cols])
        o_ref[:, cols] = (ga_ref[:, cols].astype(F32) * ya + gb_ref[:, cols].astype(F32) * yb).astype(o_ref.dtype)


def _merge(oh, oa, wh, wa, gates, tm, tn):
    T, KH = oh.shape
    KA = oa.shape[1]
    D = wh.shape[1]
    nj = D // tn
    return pl.pallas_call(
        _merge_kernel,
        grid=(T // tm, nj),
        in_specs=[pl.BlockSpec((tm, KH), lambda i, j: (i, 0)), pl.BlockSpec((tm, KA), lambda i, j: (i, 0)),
                  pl.BlockSpec((KH, tn), lambda i, j: (0, j)), pl.BlockSpec((KA, tn), lambda i, j: (0, j)),
                  pl.BlockSpec((tm, tn), lambda i, j: (i, j)), pl.BlockSpec((tm, tn), lambda i, j: (i, j + nj))],
        out_specs=pl.BlockSpec((tm, tn), lambda i, j: (i, j)),
        out_shape=jax.ShapeDtypeStruct((T, D), BF16),
        compiler_params=_params("parallel", "arbitrary"),
        name="gated_merge",
    )(oh, oa, wh, wa, gates, gates)


def _mlp_kernel(h_ref, wu_ref, wd_ref, o_ref):
    @pl.when(pl.program_id(1) == 0)
    def _():
        o_ref[...] = jnp.zeros_like(o_ref)

    u = jnp.square(jnp.maximum(_dot(h_ref[...], wu_ref[...]), 0.0)).astype(BF16)
    o_ref[...] += _dot(u, wd_ref[...])


def _mlp(h, wu, wd, tm, tf):
    T, D = h.shape
    FF = wu.shape[1]
    return pl.pallas_call(
        _mlp_kernel,
        grid=(T // tm, FF // tf),
        in_specs=[pl.BlockSpec((tm, D), lambda i, j: (i, 0)),
                  pl.BlockSpec((D, tf), lambda i, j: (0, j)),
                  pl.BlockSpec((tf, D), lambda i, j: (j, 0))],
        out_specs=pl.BlockSpec((tm, D), lambda i, j: (i, 0)),
        out_shape=jax.ShapeDtypeStruct((T, D), F32),
        compiler_params=_params("parallel", "arbitrary"),
        name="relu2_mlp",
    )(h, wu, wd)


def _ple_kernel(x_ref, p_ref, wg_ref, wp_ref, o_ref):
    x, p = x_ref[...], p_ref[...].astype(BF16)
    for cols in _col_slabs(o_ref.shape[1]):
        o_ref[:, cols] = _dot(p, wp_ref[:, cols]) * _sigmoid(_dot(x, wg_ref[:, cols]))


def _ple(xb, p, wg, wp, tm, tn):
    T, D = xb.shape
    P = p.shape[1]
    return pl.pallas_call(
        _ple_kernel,
        grid=(T // tm, D // tn),
        in_specs=[pl.BlockSpec((tm, D), lambda i, j: (i, 0)), pl.BlockSpec((tm, P), lambda i, j: (i, 0)),
                  pl.BlockSpec((D, tn), lambda i, j: (0, j)), pl.BlockSpec((P, tn), lambda i, j: (0, j))],
        out_specs=pl.BlockSpec((tm, tn), lambda i, j: (i, j)),
        out_shape=jax.ShapeDtypeStruct((T, D), F32),
        compiler_params=_params("parallel", "arbitrary"),
        name="ple_gate",
    )(xb, p, wg, wp)


def _rope_tables(seq):
    half = HEAD_DIM // 2
    inv_freq = ROPE_THETA ** (-jnp.arange(0, HEAD_DIM, 2, dtype=F32) / HEAD_DIM)
    ang = jnp.arange(seq, dtype=F32)[:, None] * inv_freq[None, :]
    cos, sin = jnp.cos(ang), jnp.sin(ang)
    assert cos.shape == (seq, half)
    return jnp.concatenate([cos, cos], axis=1), jnp.concatenate([-sin, sin], axis=1)


def kernel(x, p, norm_mix_pre, norm_mix_post, w_in, lb_fwd, lb_bwd, hgrn_norm, attn_sink, w_hgrn_proj,
           w_attn_proj, w_out, norm_mlp_pre, norm_mlp_post, w_mlp_up, w_mlp_down, w_ple, w_ple_gate, norm_ple):
    B, S, D = x.shape
    T = B * S
    depth = w_in.shape[0]
    HW = w_hgrn_proj.shape[1]
    AW = w_attn_proj.shape[1]
    KVW = AW // GQA_GROUP
    assert S % HGRN_ROWS == 0 and S % ATTN_BLOCK == 0 and hgrn_norm.shape[-1] == HEAD_DIM
    assert w_in.shape[2] == 5 * HW + AW + 2 * KVW + 2 * D

    tm = _pick(S, 1024, 512, 256, 128)
    tm_mlp = _pick(T, 512, 256, 128)
    tr = _pick(T, 256, 128)
    cos_t, sin_t = _rope_tables(S)
    vec = lambda a: a.reshape(1, -1)

    xf = x.reshape(T, D)
    for i in range(depth):
        w_in_b = w_in[i].astype(BF16)
        off_aq, off_gate = 5 * HW, 5 * HW + AW + 2 * KVW

        h = _norm_cast(xf, vec(norm_mix_pre[i]), tr)
        tn = lambda off, n: _pick(math.gcd(off, n), 1024, 512, 256, 128)
        mm = functools.partial(_mm, h, w_in_b, tm=tm)
        prep = _hgrn_prep(h, w_in_b, jnp.concatenate([lb_fwd, lb_bwd], axis=1), HW, i, tm, 2 * HEAD_DIM)
        tn_a = tn(off_aq, AW + 2 * KVW)
        nrb = S // tm
        rope_spec = pl.BlockSpec((tm, HEAD_DIM), lambda r, j: (r % nrb, 0))
        qkv = mm(off_aq, AW + 2 * KVW, functools.partial(_epi_qkv, tn=tn_a, n_q=AW, n_k=KVW), BF16, tn=tn_a,
                 extras=(cos_t, sin_t), extra_specs=(rope_spec, rope_spec), name="in_qkv")
        tn_g = tn(off_gate, 2 * D)
        later = [w_out[i], w_mlp_up[i], w_mlp_down[i], w_ple_gate[i]]
        steps = (T // tm) * (2 * D // tn_g)
        on_side = [a for a in later if _cast_rows_per_step(a, steps) is not None]
        gates, side = mm(off_gate, 2 * D, _epi_sigmoid, BF16, tn=tn_g, casts=on_side, name="in_gates") \
            if on_side else (mm(off_gate, 2 * D, _epi_sigmoid, BF16, tn=tn_g, name="in_gates"), [])
        side = iter(side)
        w_o, w_up, w_dn, w_pg = [
            next(side) if _cast_rows_per_step(a, steps) is not None else a.astype(BF16) for a in later]
        w_hp, w_ap = w_hgrn_proj[i].astype(BF16), w_attn_proj[i].astype(BF16)

        o_h = _hgrn(prep, vec(hgrn_norm[i]), B, S)
        o_a = _attn(qkv, attn_sink[i], B, S, AW, KVW, _pick(S, 512, 256, 128))
        y = _merge(o_h, o_a, w_hp, w_ap, gates, tm, _pick(D, 1024, 512, 256, 128))
        mix = _mm(y, w_o, 0, D, _epi_identity, F32, tm, _pick(D, 1024, 512, 256, 128), name="w_out")
        xf, h2 = _resid_norm(xf, mix, vec(norm_mix_post[i]), vec(norm_mlp_pre[i]), tr, renorm=True)

        d = _mlp(h2, w_up, w_dn, tm_mlp, _pick(w_mlp_up.shape[2], 512, 256))
        xf, xb = _resid_norm(xf, d, vec(norm_mlp_post[i]), vec(norm_mlp_post[i]), tr, renorm=False)

        eg = _ple(xb, p[i].reshape(T, -1), w_pg, w_ple[i].astype(BF16), tm, _pick(D, 1024, 512, 256, 128))
        xf = _resid_final(xf, eg, vec(norm_ple[i]), tr)
    return xf.reshape(B, S, D)
```

```python
import functools
import math

import jax
import jax.numpy as jnp
from jax import lax
from jax.experimental import pallas as pl
from jax.experimental.pallas import tpu as pltpu

F32 = jnp.float32
BF16 = jnp.bfloat16

EPS = 1e-6
HEAD_DIM = 128
GQA_GROUP = 4
WINDOW = 128
ATTN_BLOCK = 128
CHUNK = 64
HGRN_ROWS = 256
HGRN_UNROLL = 4
ROPE_THETA = 10000.0
MXU_COLS = 256
SUBLANES = 8
BF16_ROWS = 16
VMEM_LIMIT_BYTES = 62 * 1024 * 1024


def _params(*semantics):
    return pltpu.CompilerParams(dimension_semantics=semantics, vmem_limit_bytes=VMEM_LIMIT_BYTES)


def _pick(n, *cands):
    for c in cands:
        if n % c == 0:
            return c
    raise ValueError(f"no tile in {cands} divides {n}")


def _dot(a, b):
    return jnp.dot(a, b, preferred_element_type=F32)


def _dot_nt(a, b):
    return lax.dot_general(a, b, (((1,), (1,)), ((), ())), preferred_element_type=F32)


def _dot_tn(a, b):
    return lax.dot_general(a, b, (((0,), (0,)), ((), ())), preferred_element_type=F32)


def _sigmoid(x):
    return 1.0 / (1.0 + jnp.exp(-x))


def _rms(x, w):
    return x * lax.rsqrt(jnp.mean(x * x, axis=-1, keepdims=True) + EPS) * w


def _norm_cast_kernel(x_ref, w_ref, o_ref):
    o_ref[...] = _rms(x_ref[...], w_ref[...]).astype(o_ref.dtype)


def _norm_cast(x, w, tm):
    T, D = x.shape
    return pl.pallas_call(
        _norm_cast_kernel,
        grid=(T // tm,),
        in_specs=[pl.BlockSpec((tm, D), lambda i: (i, 0)), pl.BlockSpec((1, D), lambda i: (0, 0))],
        out_specs=pl.BlockSpec((tm, D), lambda i: (i, 0)),
        out_shape=jax.ShapeDtypeStruct((T, D), BF16),
        compiler_params=_params("parallel"),
        name="norm_cast",
    )(x, w)


def _resid_norm_kernel(x_ref, d_ref, w_ref, w2_ref, o_ref, h_ref, *, renorm):
    y = x_ref[...] + _rms(d_ref[...], w_ref[...])
    o_ref[...] = y
    h_ref[...] = (_rms(y, w2_ref[...]) if renorm else y).astype(h_ref.dtype)


def _resid_norm(x, d, w, w2, tm, renorm):
    T, D = x.shape
    row = pl.BlockSpec((tm, D), lambda i: (i, 0))
    vec = pl.BlockSpec((1, D), lambda i: (0, 0))
    return pl.pallas_call(
        functools.partial(_resid_norm_kernel, renorm=renorm),
        grid=(T // tm,),
        in_specs=[row, row, vec, vec],
        out_specs=[row, row],
        out_shape=[jax.ShapeDtypeStruct((T, D), F32), jax.ShapeDtypeStruct((T, D), BF16)],
        compiler_params=_params("parallel"),
        name="resid_norm",
    )(x, d, w, w2)


def _resid_final_kernel(x_ref, d_ref, w_ref, o_ref):
    o_ref[...] = x_ref[...] + _rms(d_ref[...], w_ref[...])


def _resid_final(x, d, w, tm):
    T, D = x.shape
    row = pl.BlockSpec((tm, D), lambda i: (i, 0))
    vec = pl.BlockSpec((1, D), lambda i: (0, 0))
    return pl.pallas_call(
        _resid_final_kernel,
        grid=(T // tm,),
        in_specs=[row, row, vec],
        out_specs=row,
        out_shape=jax.ShapeDtypeStruct((T, D), F32),
        compiler_params=_params("parallel"),
        name="resid_final",
    )(x, d, w)


def _col_slabs(width):
    step = min(width, MXU_COLS)
    return [slice(c, c + step) for c in range(0, width, step)]


def _mm_kernel(lhs_ref, w_ref, *rest, epilogue, n_casts):
    n_extra = len(rest) - 2 * n_casts - 1
    extra_refs, cast_in = rest[:n_extra], rest[n_extra:n_extra + n_casts]
    o_ref, cast_out = rest[n_extra + n_casts], rest[n_extra + n_casts + 1:]
    lhs = lhs_ref[...]
    for cols in _col_slabs(o_ref.shape[1]):
        o_ref[:, cols] = epilogue(_dot(lhs, w_ref[:, cols]), cols.start, *extra_refs).astype(o_ref.dtype)
    for src, dst in zip(cast_in, cast_out):
        dst[...] = src[...].astype(dst.dtype)


def _cast_rows_per_step(a, steps):
    rows = a.shape[0] // steps
    return rows if rows * steps == a.shape[0] and rows % BF16_ROWS == 0 else None


def _mm(lhs, w, col_off, n_cols, epilogue, out_dtype, tm, tn, extras=(), extra_specs=(), casts=(), name="mm"):
    T, K = lhs.shape
    off = col_off // tn
    nj = n_cols // tn
    steps = (T // tm) * nj
    cast_specs = [pl.BlockSpec((_cast_rows_per_step(a, steps), a.shape[1]), lambda i, j: (i * nj + j, 0))
                  for a in casts]
    res = pl.pallas_call(
        functools.partial(_mm_kernel, epilogue=epilogue, n_casts=len(casts)),
        grid=(T // tm, nj),
        in_specs=[pl.BlockSpec((tm, K), lambda i, j: (i, 0)),
                  pl.BlockSpec((K, tn), lambda i, j: (0, j + off))] + list(extra_specs) + cast_specs,
        out_specs=[pl.BlockSpec((tm, tn), lambda i, j: (i, j))] + cast_specs,
        out_shape=[jax.ShapeDtypeStruct((T, n_cols), out_dtype)]
        + [jax.ShapeDtypeStruct(a.shape, BF16) for a in casts],
        compiler_params=_params("parallel", "arbitrary"),
        name=name,
    )(lhs, w, *extras, *casts)
    return (res[0], list(res[1:])) if casts else res[0]


def _epi_sigmoid(acc, col0):
    return _sigmoid(acc)


def _epi_identity(acc, col0):
    return acc


def _epi_qkv(acc, col0, cos_ref, sin_ref, *, tn, n_q, n_k):
    cos, sin = cos_ref[...], sin_ref[...]
    heads = []
    for h in range(acc.shape[1] // HEAD_DIM):
        col = pl.program_id(1) * tn + col0 + h * HEAD_DIM
        xh = acc[:, h * HEAD_DIM:(h + 1) * HEAD_DIM]
        rot = (xh * cos + pltpu.roll(xh, HEAD_DIM // 2, 1) * sin) * jnp.where(col < n_q, HEAD_DIM ** -0.5, 1.0)
        heads.append(jnp.where(col < n_q + n_k, rot, xh))
    return jnp.concatenate(heads, axis=1)


def _chunk_cumprod(x, reverse):
    n = SUBLANES
    rid = lax.broadcasted_iota(jnp.int32, (n, x.shape[1]), 0)
    slabs = []
    for g in range(x.shape[0] // n):
        y = x[g * n:(g + 1) * n]
        s = 1
        while s < n:
            if reverse:
                y = y * jnp.where(rid < n - s, pltpu.roll(y, n - s, 0), 1.0)
            else:
                y = y * jnp.where(rid >= s, pltpu.roll(y, s, 0), 1.0)
            s *= 2
        slabs.append(y)
    order = range(len(slabs) - 1, -1, -1) if reverse else range(len(slabs))
    total = None
    for g in order:
        if total is not None:
            slabs[g] = slabs[g] * total
        total = slabs[g][0:1] if reverse else slabs[g][n - 1:n]
    return jnp.concatenate(slabs, axis=0)


def _hgrn_prep_kernel(h_ref, wq_ref, wff_ref, wfb_ref, wv_ref, wg_ref, lbf_ref, lbb_ref,
                      qf_ref, kf_ref, ktf_ref, qb_ref, kb_ref, ktb_ref, v_ref, og_ref, df_ref, db_ref, *, layer):
    h = h_ref[...]
    rows, width = qf_ref.shape
    xf = [_dot(h, wff_ref[...]), _dot(h, wfb_ref[...])]
    xq = _dot(h, wq_ref[...])
    xg = _dot(h, wg_ref[...])
    og_ref[...] = (xg * _sigmoid(xg)).astype(og_ref.dtype)
    v_ref[...] = _dot(h, wv_ref[...]).astype(v_ref.dtype)
    lbs = []
    for lb_ref in (lbf_ref, lbb_ref):
        lbp = lb_ref[...]
        e = jnp.exp(lbp - jnp.max(lbp, axis=0, keepdims=True))
        lbs.append(jnp.sum(e[:layer + 1], axis=0, keepdims=True) / jnp.sum(e, axis=0, keepdims=True))
    for c in range(rows // CHUNK):
        rs = slice(c * CHUNK, (c + 1) * CHUNK)
        q = xq[rs] * _sigmoid(xq[rs]) * (HEAD_DIM ** -0.5)
        for x, lb, reverse, qo, ko, kto, do in (
                (xf[0], lbs[0], False, qf_ref, kf_ref, ktf_ref, df_ref),
                (xf[1], lbs[1], True, qb_ref, kb_ref, ktb_ref, db_ref)):
            f = lb + (1.0 - lb) * _sigmoid(x[rs])
            eb = _chunk_cumprod(f, reverse)
            decay = eb[0:1, :] if reverse else eb[CHUNK - 1:CHUNK, :]
            k_hat = (1.0 - f) / eb
            qo[rs, :] = (q * eb).astype(qo.dtype)
            ko[rs, :] = k_hat.astype(ko.dtype)
            kto[rs, :] = (k_hat * decay).astype(kto.dtype)
            do[c:c + 1, :] = decay


def _hgrn_prep(h, w, lbp, hw, layer, tm, tn):
    T, K = h.shape
    nseg = hw // tn
    wspec = lambda seg: pl.BlockSpec((K, tn), lambda i, j: (0, seg * nseg + j))
    lbspec = lambda seg: pl.BlockSpec((lbp.shape[0], tn), lambda i, j: (0, seg * nseg + j))
    ospec = pl.BlockSpec((tm, tn), lambda i, j: (i, j))
    dspec = pl.BlockSpec((tm // CHUNK, tn), lambda i, j: (i, j))
    act = jax.ShapeDtypeStruct((T, hw), BF16)
    dec = jax.ShapeDtypeStruct((T // CHUNK, hw), F32)
    return pl.pallas_call(
        functools.partial(_hgrn_prep_kernel, layer=layer),
        grid=(T // tm, nseg),
        in_specs=[pl.BlockSpec((tm, K), lambda i, j: (i, 0)), wspec(0), wspec(1), wspec(2), wspec(3), wspec(4),
                  lbspec(0), lbspec(1)],
        out_specs=[ospec] * 8 + [dspec] * 2,
        out_shape=[act] * 8 + [dec] * 2,
        compiler_params=_params("parallel", "arbitrary"),
        name="in_hgrn",
    )(h, w, w, w, w, w, lbp, lbp)


def _hgrn_kernel(qf_ref, kf_ref, ktf_ref, qb_ref, kb_ref, ktb_ref, v_ref, og_ref, df_ref, db_ref, nw_ref, o_ref,
                 accf_ref, accb_ref, mf_ref, mb_ref, sf_ref, sb_ref, *, seq):
    R = HGRN_ROWS
    nsb = seq // R
    ncs = R // CHUNK

    row = lax.broadcasted_iota(jnp.int32, (R, R), 0)
    col = lax.broadcasted_iota(jnp.int32, (R, R), 1)
    same = (row // CHUNK) == (col // CHUNK)
    mf_ref[...] = jnp.where(same & (col <= row), 1.0, 0.0).astype(BF16)
    mb_ref[...] = jnp.where(same & (col >= row), 1.0, 0.0).astype(BF16)
    sf_ref[...] = jnp.zeros_like(sf_ref)
    sb_ref[...] = jnp.zeros_like(sb_ref)

    U = HGRN_UNROLL if nsb % HGRN_UNROLL == 0 else 1
    W = U * R
    nw = U * ncs
    blocks = [slice(u * R, (u + 1) * R) for u in range(U)]
    chunks = [slice(c * CHUNK, (c + 1) * CHUNK) for c in range(nw)]

    nt = nsb // U

    def finish_rows(r0, n):
        o = accf_ref[pl.ds(r0, n), :] + accb_ref[pl.ds(r0, n), :]
        y = _rms(o, nw_ref[...]) * og_ref[pl.ds(r0, n), :].astype(F32)
        o_ref[pl.ds(r0, n), :] = y.astype(o_ref.dtype)

    def body(t, carry, finalize):
        dirs = ((t, qf_ref, kf_ref, ktf_ref, df_ref, mf_ref, sf_ref, accf_ref, range(nw)),
                (nt - 1 - t, qb_ref, kb_ref, ktb_ref, db_ref, mb_ref, sb_ref, accb_ref, range(nw - 1, -1, -1)))
        r0s = [pl.multiple_of(d[0] * W, W) for d in dirs]
        qs = [d[1][pl.ds(r0, W), :] for d, r0 in zip(dirs, r0s)]
        vs = [v_ref[pl.ds(r0, W), :] for r0 in r0s]
        scores = []
        for d, r0, q in zip(dirs, r0s, qs):
            k = d[2][pl.ds(r0, W), :]
            scores.append([_dot_nt(q[bl], k[bl]) for bl in blocks])
        updates = []
        for d, r0, v in zip(dirs, r0s, vs):
            kt = d[3][pl.ds(r0, W), :]
            updates.append([_dot_tn(v[sl], kt[sl]) for sl in chunks])
        entering = []
        for d, upd in zip(dirs, updates):
            idx, d_ref, s_ref = d[0], d[4], d[6]
            st = s_ref[...]
            ent = [None] * nw
            for c in d[8]:
                ent[c] = st.astype(BF16)
                st = st * d_ref[pl.ds(idx * nw + c, 1), :] + upd[c]
            s_ref[...] = st
            entering.append(ent)
        inter = [[_dot_nt(q[sl], ent[c]) for c, sl in enumerate(chunks)] for q, ent in zip(qs, entering)]
        for d, r0, sc, v, o_inter in zip(dirs, r0s, scores, vs, inter):
            mask = d[5][...] > 0
            for u, bl in enumerate(blocks):
                o = _dot(jnp.where(mask, sc[u], 0.0).astype(BF16), v[bl])
                d[7][pl.ds(r0 + u * R, R), :] = o + jnp.concatenate(o_inter[u * ncs:(u + 1) * ncs], axis=0)
        if finalize:
            for r0 in r0s:
                finish_rows(r0, W)
        return carry

    if nt % 2 == 0:
        lax.fori_loop(0, nt // 2, functools.partial(body, finalize=False), 0)
        lax.fori_loop(nt // 2, nt, functools.partial(body, finalize=True), 0)
    else:
        lax.fori_loop(0, nt, functools.partial(body, finalize=False), 0)

        def finish(t, carry):
            finish_rows(pl.multiple_of(t * W, W), W)
            return carry

        lax.fori_loop(0, nt, finish, 0)


def _hgrn(prep, nw, batch, seq):
    *acts, df, db = prep
    T, HW = acts[0].shape
    H = HW // HEAD_DIM
    blk = pl.BlockSpec((seq, HEAD_DIM), lambda b, h: (b, h))
    dblk = pl.BlockSpec((seq // CHUNK, HEAD_DIM), lambda b, h: (b, h))
    return pl.pallas_call(
        functools.partial(_hgrn_kernel, seq=seq),
        grid=(batch, H),
        in_specs=[blk] * 8 + [dblk] * 2 + [pl.BlockSpec((1, HEAD_DIM), lambda b, h: (0, 0))],
        out_specs=blk,
        out_shape=jax.ShapeDtypeStruct((T, HW), BF16),
        scratch_shapes=[pltpu.VMEM((seq, HEAD_DIM), F32), pltpu.VMEM((seq, HEAD_DIM), F32),
                        pltpu.VMEM((HGRN_ROWS, HGRN_ROWS), BF16), pltpu.VMEM((HGRN_ROWS, HGRN_ROWS), BF16),
                        pltpu.VMEM((HEAD_DIM, HEAD_DIM), F32), pltpu.VMEM((HEAD_DIM, HEAD_DIM), F32)],
        compiler_params=_params("parallel", "parallel"),
        name="hgrn2",
    )(*acts, df, db, nw)


def _attn_kernel(sink_ref, q_ref, kp_ref, kc_ref, kn_ref, vp_ref, vc_ref, vn_ref, o_ref, *, seq):
    G, D, BLK = GQA_GROUP, HEAD_DIM, ATTN_BLOCK
    rows = q_ref.shape[0]
    kvh = pl.program_id(1)
    n = pl.program_id(2)
    k = jnp.concatenate([kp_ref[...], kc_ref[...], kn_ref[...]], axis=0)
    v = jnp.concatenate([vp_ref[...], vc_ref[...], vn_ref[...]], axis=0)
    v1 = jnp.concatenate([v, jnp.ones_like(v)], axis=1)
    r = lax.broadcasted_iota(jnp.int32, (G * BLK, 3 * BLK), 0) % BLK
    c = lax.broadcasted_iota(jnp.int32, (G * BLK, 3 * BLK), 1)
    band = jnp.where(jnp.abs(c - BLK - r) <= WINDOW, 0.0, -jnp.inf)
    kpos = n * rows - BLK + lax.broadcasted_iota(jnp.int32, (1, rows + 2 * BLK), 1)
    inside = jnp.where((kpos >= 0) & (kpos < seq), 0.0, -jnp.inf)
    sk = jnp.concatenate([jnp.full((BLK, BLK), sink_ref[kvh * G + h], F32) for h in range(G)], axis=0)
    nsub = rows // BLK
    scores = []
    for j in range(nsub):
        q = q_ref[j * BLK:(j + 1) * BLK, :]
        q4 = jnp.concatenate([q[:, h * D:(h + 1) * D] for h in range(G)], axis=0)
        scores.append(_dot_nt(q4, k[j * BLK:(j + 3) * BLK]))
    probs, sink_terms = [], []
    for j in range(nsub):
        s = scores[j] + band
        if j == 0 or j == nsub - 1:
            s = s + inside[:, j * BLK:(j + 3) * BLK]
        m = jnp.maximum(jnp.broadcast_to(jnp.max(s, axis=-1, keepdims=True), (G * BLK, BLK)), sk)
        probs.append(jnp.concatenate(
            [jnp.exp(s[:, i * BLK:(i + 1) * BLK] - m) for i in range(3)], axis=1).astype(BF16))
        sink_terms.append(jnp.exp(sk - m))
    for j in range(nsub):
        pv = _dot(probs[j], v1[j * BLK:(j + 3) * BLK])
        o = pv[:, :D] / (pv[:, D:] + sink_terms[j])
        o_ref[j * BLK:(j + 1) * BLK, :] = jnp.concatenate(
            [o[h * BLK:(h + 1) * BLK] for h in range(G)], axis=1).astype(o_ref.dtype)


def _attn(qkv, sink, batch, seq, n_q_cols, n_kv_cols, rows):
    T = qkv.shape[0]
    G, D, BLK = GQA_GROUP, HEAD_DIM, ATTN_BLOCK
    kvh = n_kv_cols // D
    nb = seq // BLK
    nq = seq // rows
    sub = rows // BLK
    k_off = n_q_cols // D
    v_off = (n_q_cols + n_kv_cols) // D
    qspec = pl.BlockSpec((rows, G * D), lambda b, h, n: (b * nq + n, h))
    prev = lambda b, n: b * nb + jnp.maximum(n * sub - 1, 0)
    nxt = lambda b, n: b * nb + jnp.minimum((n + 1) * sub, nb - 1)
    edge = lambda f, off: pl.BlockSpec((BLK, D), lambda b, h, n: (f(b, n), off + h))
    cur = lambda off: pl.BlockSpec((rows, D), lambda b, h, n: (b * nq + n, off + h))
    return pl.pallas_call(
        functools.partial(_attn_kernel, seq=seq),
        grid=(batch, kvh, nq),
        in_specs=[pl.BlockSpec(memory_space=pltpu.SMEM), qspec,
                  edge(prev, k_off), cur(k_off), edge(nxt, k_off), edge(prev, v_off), cur(v_off), edge(nxt, v_off)],
        out_specs=qspec,
        out_shape=jax.ShapeDtypeStruct((T, n_q_cols), BF16),
        compiler_params=_params("parallel", "parallel", "arbitrary"),
        name="swa_sink",
    )(sink, *([qkv] * 7))


def _merge_kernel(oh_ref, oa_ref, wh_ref, wa_ref, ga_ref, gb_ref, o_ref):
    oh, oa = oh_ref[...], oa_ref[...]
    for cols in _col_slabs(o_ref.shape[1]):
        ya = _dot(oh, wh_ref[:, cols])
        yb = _dot(oa, wa_ref[:, cols])
        o_ref[:, cols] = (ga_ref[:, cols].astype(F32) * ya + gb_ref[:, cols].astype(F32) * yb).astype(o_ref.dtype)


def _merge(oh, oa, wh, wa, gates, tm, tn):
    T, KH = oh.shape
    KA = oa.shape[1]
    D = wh.shape[1]
    nj = D // tn
    return pl.pallas_call(
        _merge_kernel,
        grid=(T // tm, nj),
        in_specs=[pl.BlockSpec((tm, KH), lambda i, j: (i, 0)), pl.BlockSpec((tm, KA), lambda i, j: (i, 0)),
                  pl.BlockSpec((KH, tn), lambda i, j: (0, j)), pl.BlockSpec((KA, tn), lambda i, j: (0, j)),
                  pl.BlockSpec((tm, tn), lambda i, j: (i, j)), pl.BlockSpec((tm, tn), lambda i, j: (i, j + nj))],
        out_specs=pl.BlockSpec((tm, tn), lambda i, j: (i, j)),
        out_shape=jax.ShapeDtypeStruct((T, D), BF16),
        compiler_params=_params("parallel", "arbitrary"),
        name="gated_merge",
    )(oh, oa, wh, wa, gates, gates)


def _mlp_kernel(h_ref, wu_ref, wd_ref, o_ref):
    @pl.when(pl.program_id(1) == 0)
    def _():
        o_ref[...] = jnp.zeros_like(o_ref)

    u = jnp.square(jnp.maximum(_dot(h_ref[...], wu_ref[...]), 0.0)).astype(BF16)
    o_ref[...] += _dot(u, wd_ref[...])


def _mlp(h, wu, wd, tm, tf):
    T, D = h.shape
    FF = wu.shape[1]
    return pl.pallas_call(
        _mlp_kernel,
        grid=(T // tm, FF // tf),
        in_specs=[pl.BlockSpec((tm, D), lambda i, j: (i, 0)),
                  pl.BlockSpec((D, tf), lambda i, j: (0, j)),
                  pl.BlockSpec((tf, D), lambda i, j: (j, 0))],
        out_specs=pl.BlockSpec((tm, D), lambda i, j: (i, 0)),
        out_shape=jax.ShapeDtypeStruct((T, D), F32),
        compiler_params=_params("parallel", "arbitrary"),
        name="relu2_mlp",
    )(h, wu, wd)


def _ple_kernel(x_ref, p_ref, wg_ref, wp_ref, o_ref):
    x, p = x_ref[...], p_ref[...].astype(BF16)
    for cols in _col_slabs(o_ref.shape[1]):
        o_ref[:, cols] = _dot(p, wp_ref[:, cols]) * _sigmoid(_dot(x, wg_ref[:, cols]))


def _ple(xb, p, wg, wp, tm, tn):
    T, D = xb.shape
    P = p.shape[1]
    return pl.pallas_call(
        _ple_kernel,
        grid=(T // tm, D // tn),
        in_specs=[pl.BlockSpec((tm, D), lambda i, j: (i, 0)), pl.BlockSpec((tm, P), lambda i, j: (i, 0)),
                  pl.BlockSpec((D, tn), lambda i, j: (0, j)), pl.BlockSpec((P, tn), lambda i, j: (0, j))],
        out_specs=pl.BlockSpec((tm, tn), lambda i, j: (i, j)),
        out_shape=jax.ShapeDtypeStruct((T, D), F32),
        compiler_params=_params("parallel", "arbitrary"),
        name="ple_gate",
    )(xb, p, wg, wp)


def _rope_tables(seq):
    half = HEAD_DIM // 2
    inv_freq = ROPE_THETA ** (-jnp.arange(0, HEAD_DIM, 2, dtype=F32) / HEAD_DIM)
    ang = jnp.arange(seq, dtype=F32)[:, None] * inv_freq[None, :]
    cos, sin = jnp.cos(ang), jnp.sin(ang)
    assert cos.shape == (seq, half)
    return jnp.concatenate([cos, cos], axis=1), jnp.concatenate([-sin, sin], axis=1)


def kernel(x, p, norm_mix_pre, norm_mix_post, w_in, lb_fwd, lb_bwd, hgrn_norm, attn_sink, w_hgrn_proj,
           w_attn_proj, w_out, norm_mlp_pre, norm_mlp_post, w_mlp_up, w_mlp_down, w_ple, w_ple_gate, norm_ple):
    B, S, D = x.shape
    T = B * S
    depth = w_in.shape[0]
    HW = w_hgrn_proj.shape[1]
    AW = w_attn_proj.shape[1]
    KVW = AW // GQA_GROUP
    assert S % HGRN_ROWS == 0 and S % ATTN_BLOCK == 0 and hgrn_norm.shape[-1] == HEAD_DIM
    assert w_in.shape[2] == 5 * HW + AW + 2 * KVW + 2 * D

    tm = _pick(S, 1024, 512, 256, 128)
    tm_mlp = _pick(T, 512, 256, 128)
    tr = _pick(T, 256, 128)
    cos_t, sin_t = _rope_tables(S)
    vec = lambda a: a.reshape(1, -1)

    xf = x.reshape(T, D)
    for i in range(depth):
        w_in_b = w_in[i].astype(BF16)
        off_aq, off_gate = 5 * HW, 5 * HW + AW + 2 * KVW

        h = _norm_cast(xf, vec(norm_mix_pre[i]), tr)
        tn = lambda off, n: _pick(math.gcd(off, n), 1024, 512, 256, 128)
        mm = functools.partial(_mm, h, w_in_b, tm=tm)
        prep = _hgrn_prep(h, w_in_b, jnp.concatenate([lb_fwd, lb_bwd], axis=1), HW, i, tm, 2 * HEAD_DIM)
        tn_a = tn(off_aq, AW + 2 * KVW)
        nrb = S // tm
        rope_spec = pl.BlockSpec((tm, HEAD_DIM), lambda r, j: (r % nrb, 0))
        qkv = mm(off_aq, AW + 2 * KVW, functools.partial(_epi_qkv, tn=tn_a, n_q=AW, n_k=KVW), BF16, tn=tn_a,
                 extras=(cos_t, sin_t), extra_specs=(rope_spec, rope_spec), name="in_qkv")
        tn_g = tn(off_gate, 2 * D)
        later = [w_out[i], w_mlp_up[i], w_mlp_down[i], w_ple_gate[i]]
        steps = (T // tm) * (2 * D // tn_g)
        on_side = [a for a in later if _cast_rows_per_step(a, steps) is not None]
        gates, side = mm(off_gate, 2 * D, _epi_sigmoid, BF16, tn=tn_g, casts=on_side, name="in_gates") \
            if on_side else (mm(off_gate, 2 * D, _epi_sigmoid, BF16, tn=tn_g, name="in_gates"), [])
        side = iter(side)
        w_o, w_up, w_dn, w_pg = [
            next(side) if _cast_rows_per_step(a, steps) is not None else a.astype(BF16) for a in later]
        w_hp, w_ap = w_hgrn_proj[i].astype(BF16), w_attn_proj[i].astype(BF16)

        o_h = _hgrn(prep, vec(hgrn_norm[i]), B, S)
        o_a = _attn(qkv, attn_sink[i], B, S, AW, KVW, _pick(S, 512, 256, 128))
        y = _merge(o_h, o_a, w_hp, w_ap, gates, tm, _pick(D, 1024, 512, 256, 128))
        mix = _mm(y, w_o, 0, D, _epi_identity, F32, tm, _pick(D, 1024, 512, 256, 128), name="w_out")
        xf, h2 = _resid_norm(xf, mix, vec(norm_mix_post[i]), vec(norm_mlp_pre[i]), tr, renorm=True)

        d = _mlp(h2, w_up, w_dn, tm_mlp, _pick(w_mlp_up.shape[2], 1024, 512, 256))
        xf, xb = _resid_norm(xf, d, vec(norm_mlp_post[i]), vec(norm_mlp_post[i]), tr, renorm=False)

        eg = _ple(xb, p[i].reshape(T, -1), w_pg, w_ple[i].astype(BF16), tm, _pick(D, 1024, 512, 256, 128))
        xf = _resid_final(xf, eg, vec(norm_ple[i]), tr)
    return xf.reshape(B, S, D)
```

```python
import functools
import math

import jax
import jax.numpy as jnp
from jax import lax
from jax.experimental import pallas as pl
from jax.experimental.pallas import tpu as pltpu

F32 = jnp.float32
BF16 = jnp.bfloat16

EPS = 1e-6
HEAD_DIM = 128
GQA_GROUP = 4
WINDOW = 128
ATTN_BLOCK = 128
CHUNK = 64
HGRN_ROWS = 256
HGRN_UNROLL = 4
ROPE_THETA = 10000.0
MXU_COLS = 256
SUBLANES = 8
BF16_ROWS = 16
VMEM_LIMIT_BYTES = 62 * 1024 * 1024


def _params(*semantics):
    return pltpu.CompilerParams(dimension_semantics=semantics, vmem_limit_bytes=VMEM_LIMIT_BYTES)


def _pick(n, *cands):
    for c in cands:
        if n % c == 0:
            return c
    raise ValueError(f"no tile in {cands} divides {n}")


def _dot(a, b):
    return jnp.dot(a, b, preferred_element_type=F32)


def _dot_nt(a, b):
    return lax.dot_general(a, b, (((1,), (1,)), ((), ())), preferred_element_type=F32)


def _dot_tn(a, b):
    return lax.dot_general(a, b, (((0,), (0,)), ((), ())), preferred_element_type=F32)


def _sigmoid(x):
    return 1.0 / (1.0 + jnp.exp(-x))


def _rms(x, w):
    return x * lax.rsqrt(jnp.mean(x * x, axis=-1, keepdims=True) + EPS) * w


def _norm_cast_kernel(x_ref, w_ref, o_ref):
    o_ref[...] = _rms(x_ref[...], w_ref[...]).astype(o_ref.dtype)


def _norm_cast(x, w, tm):
    T, D = x.shape
    return pl.pallas_call(
        _norm_cast_kernel,
        grid=(T // tm,),
        in_specs=[pl.BlockSpec((tm, D), lambda i: (i, 0)), pl.BlockSpec((1, D), lambda i: (0, 0))],
        out_specs=pl.BlockSpec((tm, D), lambda i: (i, 0)),
        out_shape=jax.ShapeDtypeStruct((T, D), BF16),
        compiler_params=_params("parallel"),
        name="norm_cast",
    )(x, w)


def _resid_norm_kernel(x_ref, d_ref, w_ref, w2_ref, o_ref, h_ref, *, renorm):
    y = x_ref[...] + _rms(d_ref[...], w_ref[...])
    o_ref[...] = y
    h_ref[...] = (_rms(y, w2_ref[...]) if renorm else y).astype(h_ref.dtype)


def _resid_norm(x, d, w, w2, tm, renorm):
    T, D = x.shape
    row = pl.BlockSpec((tm, D), lambda i: (i, 0))
    vec = pl.BlockSpec((1, D), lambda i: (0, 0))
    return pl.pallas_call(
        functools.partial(_resid_norm_kernel, renorm=renorm),
        grid=(T // tm,),
        in_specs=[row, row, vec, vec],
        out_specs=[row, row],
        out_shape=[jax.ShapeDtypeStruct((T, D), F32), jax.ShapeDtypeStruct((T, D), BF16)],
        compiler_params=_params("parallel"),
        name="resid_norm",
    )(x, d, w, w2)


def _resid_final_kernel(x_ref, d_ref, w_ref, o_ref):
    o_ref[...] = x_ref[...] + _rms(d_ref[...], w_ref[...])


def _resid_final(x, d, w, tm):
    T, D = x.shape
    row = pl.BlockSpec((tm, D), lambda i: (i, 0))
    vec = pl.BlockSpec((1, D), lambda i: (0, 0))
    return pl.pallas_call(
        _resid_final_kernel,
        grid=(T // tm,),
        in_specs=[row, row, vec],
        out_specs=row,
        out_shape=jax.ShapeDtypeStruct((T, D), F32),
        compiler_params=_params("parallel"),
        name="resid_final",
    )(x, d, w)


def _col_slabs(width):
    step = min(width, MXU_COLS)
    return [slice(c, c + step) for c in range(0, width, step)]


def _mm_kernel(lhs_ref, w_ref, *rest, epilogue, n_casts):
    n_extra = len(rest) - 2 * n_casts - 1
    extra_refs, cast_in = rest[:n_extra], rest[n_extra:n_extra + n_casts]
    o_ref, cast_out = rest[n_extra + n_casts], rest[n_extra + n_casts + 1:]
    lhs = lhs_ref[...]
    for cols in _col_slabs(o_ref.shape[1]):
        o_ref[:, cols] = epilogue(_dot(lhs, w_ref[:, cols]), cols.start, *extra_refs).astype(o_ref.dtype)
    for src, dst in zip(cast_in, cast_out):
        dst[...] = src[...].astype(dst.dtype)


def _cast_rows_per_step(a, steps):
    rows = a.shape[0] // steps
    return rows if rows * steps == a.shape[0] and rows % BF16_ROWS == 0 else None


def _mm(lhs, w, col_off, n_cols, epilogue, out_dtype, tm, tn, extras=(), extra_specs=(), casts=(), name="mm"):
    T, K = lhs.shape
    off = col_off // tn
    nj = n_cols // tn
    steps = (T // tm) * nj
    cast_specs = [pl.BlockSpec((_cast_rows_per_step(a, steps), a.shape[1]), lambda i, j: (i * nj + j, 0))
                  for a in casts]
    res = pl.pallas_call(
        functools.partial(_mm_kernel, epilogue=epilogue, n_casts=len(casts)),
        grid=(T // tm, nj),
        in_specs=[pl.BlockSpec((tm, K), lambda i, j: (i, 0)),
                  pl.BlockSpec((K, tn), lambda i, j: (0, j + off))] + list(extra_specs) + cast_specs,
        out_specs=[pl.BlockSpec((tm, tn), lambda i, j: (i, j))] + cast_specs,
        out_shape=[jax.ShapeDtypeStruct((T, n_cols), out_dtype)]
        + [jax.ShapeDtypeStruct(a.shape, BF16) for a in casts],
        compiler_params=_params("parallel", "arbitrary"),
        name=name,
    )(lhs, w, *extras, *casts)
    return (res[0], list(res[1:])) if casts else res[0]


def _epi_sigmoid(acc, col0):
    return _sigmoid(acc)


def _epi_identity(acc, col0):
    return acc


def _epi_qkv(acc, col0, cos_ref, sin_ref, *, tn, n_q, n_k):
    cos, sin = cos_ref[...], sin_ref[...]
    heads = []
    for h in range(acc.shape[1] // HEAD_DIM):
        col = pl.program_id(1) * tn + col0 + h * HEAD_DIM
        xh = acc[:, h * HEAD_DIM:(h + 1) * HEAD_DIM]
        rot = (xh * cos + pltpu.roll(xh, HEAD_DIM // 2, 1) * sin) * jnp.where(col < n_q, HEAD_DIM ** -0.5, 1.0)
        heads.append(jnp.where(col < n_q + n_k, rot, xh))
    return jnp.concatenate(heads, axis=1)


def _chunk_cumprod(x, reverse):
    n = SUBLANES
    rid = lax.broadcasted_iota(jnp.int32, (n, x.shape[1]), 0)
    slabs = []
    for g in range(x.shape[0] // n):
        y = x[g * n:(g + 1) * n]
        s = 1
        while s < n:
            if reverse:
                y = y * jnp.where(rid < n - s, pltpu.roll(y, n - s, 0), 1.0)
            else:
                y = y * jnp.where(rid >= s, pltpu.roll(y, s, 0), 1.0)
            s *= 2
        slabs.append(y)
    order = range(len(slabs) - 1, -1, -1) if reverse else range(len(slabs))
    total = None
    for g in order:
        if total is not None:
            slabs[g] = slabs[g] * total
        total = slabs[g][0:1] if reverse else slabs[g][n - 1:n]
    return jnp.concatenate(slabs, axis=0)


def _hgrn_prep_kernel(h_ref, wq_ref, wff_ref, wfb_ref, wv_ref, wg_ref, lbf_ref, lbb_ref, *rest, layer):
    n_pieces = len(rest) - 10 - (1 if len(rest) > 10 else 0)
    pieces = rest[:n_pieces]
    qf_ref, kf_ref, ktf_ref, qb_ref, kb_ref, ktb_ref, v_ref, og_ref, df_ref, db_ref = rest[n_pieces:n_pieces + 10]
    if n_pieces:
        slab = rest[-1]
        cw = pieces[0].shape[1]
        for p, src in enumerate(pieces):
            slab[:, p * cw:(p + 1) * cw] = src[...].astype(slab.dtype)
    h = h_ref[...]
    rows, width = qf_ref.shape
    xf = [_dot(h, wff_ref[...]), _dot(h, wfb_ref[...])]
    xq = _dot(h, wq_ref[...])
    xg = _dot(h, wg_ref[...])
    og_ref[...] = (xg * _sigmoid(xg)).astype(og_ref.dtype)
    v_ref[...] = _dot(h, wv_ref[...]).astype(v_ref.dtype)
    lbs = []
    for lb_ref in (lbf_ref, lbb_ref):
        lbp = lb_ref[...]
        e = jnp.exp(lbp - jnp.max(lbp, axis=0, keepdims=True))
        lbs.append(jnp.sum(e[:layer + 1], axis=0, keepdims=True) / jnp.sum(e, axis=0, keepdims=True))
    for c in range(rows // CHUNK):
        rs = slice(c * CHUNK, (c + 1) * CHUNK)
        q = xq[rs] * _sigmoid(xq[rs]) * (HEAD_DIM ** -0.5)
        for x, lb, reverse, qo, ko, kto, do in (
                (xf[0], lbs[0], False, qf_ref, kf_ref, ktf_ref, df_ref),
                (xf[1], lbs[1], True, qb_ref, kb_ref, ktb_ref, db_ref)):
            f = lb + (1.0 - lb) * _sigmoid(x[rs])
            eb = _chunk_cumprod(f, reverse)
            decay = eb[0:1, :] if reverse else eb[CHUNK - 1:CHUNK, :]
            k_hat = (1.0 - f) / eb
            qo[rs, :] = (q * eb).astype(qo.dtype)
            ko[rs, :] = k_hat.astype(ko.dtype)
            kto[rs, :] = (k_hat * decay).astype(kto.dtype)
            do[c:c + 1, :] = decay


def _hgrn_prep(h, w, lbp, hw, layer, tm, tn, w_full=None):
    T, K = h.shape
    nseg = hw // tn
    steps = (T // tm) * nseg
    wspec = lambda seg: pl.BlockSpec((K, tn), lambda i, j: (0, seg * nseg + j))
    lbspec = lambda seg: pl.BlockSpec((lbp.shape[0], tn), lambda i, j: (0, seg * nseg + j))
    ospec = pl.BlockSpec((tm, tn), lambda i, j: (i, j))
    dspec = pl.BlockSpec((tm // CHUNK, tn), lambda i, j: (i, j))
    act = jax.ShapeDtypeStruct((T, hw), BF16)
    dec = jax.ShapeDtypeStruct((T // CHUNK, hw), F32)
    side_in, side_specs, side_out_spec, side_out_shape = [], [], [], []
    rows = None if w_full is None else _cast_rows_per_step(w_full, steps)
    if rows is not None:
        first, rest_cols = 5 * hw, w_full.shape[1] - 5 * hw
        cw = math.gcd(first, rest_cols)
        if cw % HEAD_DIM == 0:
            for p in range(rest_cols // cw):
                side_in.append(w_full)
                side_specs.append(pl.BlockSpec((rows, cw), lambda i, j, p=p: (i * nseg + j, first // cw + p)))
            side_out_spec = [pl.BlockSpec((rows, rest_cols), lambda i, j: (i * nseg + j, 0))]
            side_out_shape = [jax.ShapeDtypeStruct((K, rest_cols), BF16)]
    res = pl.pallas_call(
        functools.partial(_hgrn_prep_kernel, layer=layer),
        grid=(T // tm, nseg),
        in_specs=[pl.BlockSpec((tm, K), lambda i, j: (i, 0)), wspec(0), wspec(1), wspec(2), wspec(3), wspec(4),
                  lbspec(0), lbspec(1)] + side_specs,
        out_specs=[ospec] * 8 + [dspec] * 2 + side_out_spec,
        out_shape=[act] * 8 + [dec] * 2 + side_out_shape,
        compiler_params=_params("parallel", "arbitrary"),
        name="in_hgrn",
    )(h, w, w, w, w, w, lbp, lbp, *side_in)
    return res[:10], (res[10] if side_in else None)


def _hgrn_kernel(qf_ref, kf_ref, ktf_ref, qb_ref, kb_ref, ktb_ref, v_ref, og_ref, df_ref, db_ref, nw_ref, o_ref,
                 accf_ref, accb_ref, mf_ref, mb_ref, sf_ref, sb_ref, *, seq):
    R = HGRN_ROWS
    nsb = seq // R
    ncs = R // CHUNK

    row = lax.broadcasted_iota(jnp.int32, (R, R), 0)
    col = lax.broadcasted_iota(jnp.int32, (R, R), 1)
    same = (row // CHUNK) == (col // CHUNK)
    mf_ref[...] = jnp.where(same & (col <= row), 1.0, 0.0).astype(BF16)
    mb_ref[...] = jnp.where(same & (col >= row), 1.0, 0.0).astype(BF16)
    sf_ref[...] = jnp.zeros_like(sf_ref)
    sb_ref[...] = jnp.zeros_like(sb_ref)

    U = HGRN_UNROLL if nsb % HGRN_UNROLL == 0 else 1
    W = U * R
    nw = U * ncs
    blocks = [slice(u * R, (u + 1) * R) for u in range(U)]
    chunks = [slice(c * CHUNK, (c + 1) * CHUNK) for c in range(nw)]

    nt = nsb // U

    def finish_rows(r0, n):
        o = accf_ref[pl.ds(r0, n), :] + accb_ref[pl.ds(r0, n), :]
        y = _rms(o, nw_ref[...]) * og_ref[pl.ds(r0, n), :].astype(F32)
        o_ref[pl.ds(r0, n), :] = y.astype(o_ref.dtype)

    def body(t, carry, finalize):
        dirs = ((t, qf_ref, kf_ref, ktf_ref, df_ref, mf_ref, sf_ref, accf_ref, range(nw)),
                (nt - 1 - t, qb_ref, kb_ref, ktb_ref, db_ref, mb_ref, sb_ref, accb_ref, range(nw - 1, -1, -1)))
        r0s = [pl.multiple_of(d[0] * W, W) for d in dirs]
        qs = [d[1][pl.ds(r0, W), :] for d, r0 in zip(dirs, r0s)]
        vs = [v_ref[pl.ds(r0, W), :] for r0 in r0s]
        scores = []
        for d, r0, q in zip(dirs, r0s, qs):
            k = d[2][pl.ds(r0, W), :]
            scores.append([_dot_nt(q[bl], k[bl]) for bl in blocks])
        updates = []
        for d, r0, v in zip(dirs, r0s, vs):
            kt = d[3][pl.ds(r0, W), :]
            updates.append([_dot_tn(v[sl], kt[sl]) for sl in chunks])
        entering = []
        for d, upd in zip(dirs, updates):
            idx, d_ref, s_ref = d[0], d[4], d[6]
            st = s_ref[...]
            ent = [None] * nw
            for c in d[8]:
                ent[c] = st.astype(BF16)
                st = st * d_ref[pl.ds(idx * nw + c, 1), :] + upd[c]
            s_ref[...] = st
            entering.append(ent)
        inter = [[_dot_nt(q[sl], ent[c]) for c, sl in enumerate(chunks)] for q, ent in zip(qs, entering)]
        for d, r0, sc, v, o_inter in zip(dirs, r0s, scores, vs, inter):
            mask = d[5][...] > 0
            for u, bl in enumerate(blocks):
                o = _dot(jnp.where(mask, sc[u], 0.0).astype(BF16), v[bl])
                d[7][pl.ds(r0 + u * R, R), :] = o + jnp.concatenate(o_inter[u * ncs:(u + 1) * ncs], axis=0)
        if finalize:
            for r0 in r0s:
                finish_rows(r0, W)
        return carry

    if nt % 2 == 0:
        lax.fori_loop(0, nt // 2, functools.partial(body, finalize=False), 0)
        lax.fori_loop(nt // 2, nt, functools.partial(body, finalize=True), 0)
    else:
        lax.fori_loop(0, nt, functools.partial(body, finalize=False), 0)

        def finish(t, carry):
            finish_rows(pl.multiple_of(t * W, W), W)
            return carry

        lax.fori_loop(0, nt, finish, 0)


def _hgrn(prep, nw, batch, seq):
    *acts, df, db = prep
    T, HW = acts[0].shape
    H = HW // HEAD_DIM
    blk = pl.BlockSpec((seq, HEAD_DIM), lambda b, h: (b, h))
    dblk = pl.BlockSpec((seq // CHUNK, HEAD_DIM), lambda b, h: (b, h))
    return pl.pallas_call(
        functools.partial(_hgrn_kernel, seq=seq),
        grid=(batch, H),
        in_specs=[blk] * 8 + [dblk] * 2 + [pl.BlockSpec((1, HEAD_DIM), lambda b, h: (0, 0))],
        out_specs=blk,
        out_shape=jax.ShapeDtypeStruct((T, HW), BF16),
        scratch_shapes=[pltpu.VMEM((seq, HEAD_DIM), F32), pltpu.VMEM((seq, HEAD_DIM), F32),
                        pltpu.VMEM((HGRN_ROWS, HGRN_ROWS), BF16), pltpu.VMEM((HGRN_ROWS, HGRN_ROWS), BF16),
                        pltpu.VMEM((HEAD_DIM, HEAD_DIM), F32), pltpu.VMEM((HEAD_DIM, HEAD_DIM), F32)],
        compiler_params=_params("parallel", "parallel"),
        name="hgrn2",
    )(*acts, df, db, nw)


def _attn_kernel(sink_ref, q_ref, kp_ref, kc_ref, kn_ref, vp_ref, vc_ref, vn_ref, o_ref, *, seq):
    G, D, BLK = GQA_GROUP, HEAD_DIM, ATTN_BLOCK
    rows = q_ref.shape[0]
    kvh = pl.program_id(1)
    n = pl.program_id(2)
    k = jnp.concatenate([kp_ref[...], kc_ref[...], kn_ref[...]], axis=0)
    v = jnp.concatenate([vp_ref[...], vc_ref[...], vn_ref[...]], axis=0)
    v1 = jnp.concatenate([v, jnp.ones_like(v)], axis=1)
    r = lax.broadcasted_iota(jnp.int32, (G * BLK, 3 * BLK), 0) % BLK
    c = lax.broadcasted_iota(jnp.int32, (G * BLK, 3 * BLK), 1)
    band = jnp.where(jnp.abs(c - BLK - r) <= WINDOW, 0.0, -jnp.inf)
    kpos = n * rows - BLK + lax.broadcasted_iota(jnp.int32, (1, rows + 2 * BLK), 1)
    inside = jnp.where((kpos >= 0) & (kpos < seq), 0.0, -jnp.inf)
    sk = jnp.concatenate([jnp.full((BLK, BLK), sink_ref[kvh * G + h], F32) for h in range(G)], axis=0)
    nsub = rows // BLK
    scores = []
    for j in range(nsub):
        q = q_ref[j * BLK:(j + 1) * BLK, :]
        q4 = jnp.concatenate([q[:, h * D:(h + 1) * D] for h in range(G)], axis=0)
        scores.append(_dot_nt(q4, k[j * BLK:(j + 3) * BLK]))
    probs, sink_terms = [], []
    for j in range(nsub):
        s = scores[j] + band
        if j == 0 or j == nsub - 1:
            s = s + inside[:, j * BLK:(j + 3) * BLK]
        m = jnp.maximum(jnp.broadcast_to(jnp.max(s, axis=-1, keepdims=True), (G * BLK, BLK)), sk)
        probs.append(jnp.concatenate(
            [jnp.exp(s[:, i * BLK:(i + 1) * BLK] - m) for i in range(3)], axis=1).astype(BF16))
        sink_terms.append(jnp.exp(sk - m))
    for j in range(nsub):
        pv = _dot(probs[j], v1[j * BLK:(j + 3) * BLK])
        o = pv[:, :D] / (pv[:, D:] + sink_terms[j])
        o_ref[j * BLK:(j + 1) * BLK, :] = jnp.concatenate(
            [o[h * BLK:(h + 1) * BLK] for h in range(G)], axis=1).astype(o_ref.dtype)


def _attn(qkv, sink, batch, seq, n_q_cols, n_kv_cols, rows):
    T = qkv.shape[0]
    G, D, BLK = GQA_GROUP, HEAD_DIM, ATTN_BLOCK
    kvh = n_kv_cols // D
    nb = seq // BLK
    nq = seq // rows
    sub = rows // BLK
    k_off = n_q_cols // D
    v_off = (n_q_cols + n_kv_cols) // D
    qspec = pl.BlockSpec((rows, G * D), lambda b, h, n: (b * nq + n, h))
    prev = lambda b, n: b * nb + jnp.maximum(n * sub - 1, 0)
    nxt = lambda b, n: b * nb + jnp.minimum((n + 1) * sub, nb - 1)
    edge = lambda f, off: pl.BlockSpec((BLK, D), lambda b, h, n: (f(b, n), off + h))
    cur = lambda off: pl.BlockSpec((rows, D), lambda b, h, n: (b * nq + n, off + h))
    return pl.pallas_call(
        functools.partial(_attn_kernel, seq=seq),
        grid=(batch, kvh, nq),
        in_specs=[pl.BlockSpec(memory_space=pltpu.SMEM), qspec,
                  edge(prev, k_off), cur(k_off), edge(nxt, k_off), edge(prev, v_off), cur(v_off), edge(nxt, v_off)],
        out_specs=qspec,
        out_shape=jax.ShapeDtypeStruct((T, n_q_cols), BF16),
        compiler_params=_params("parallel", "parallel", "arbitrary"),
        name="swa_sink",
    )(sink, *([qkv] * 7))


def _merge_kernel(oh_ref, oa_ref, wh_ref, wa_ref, ga_ref, gb_ref, o_ref):
    oh, oa = oh_ref[...], oa_ref[...]
    for cols in _col_slabs(o_ref.shape[1]):
        ya = _dot(oh, wh_ref[:, cols])
        yb = _dot(oa, wa_ref[:, cols])
        o_ref[:, cols] = (ga_ref[:, cols].astype(F32) * ya + gb_ref[:, cols].astype(F32) * yb).astype(o_ref.dtype)


def _merge(oh, oa, wh, wa, gates, tm, tn):
    T, KH = oh.shape
    KA = oa.shape[1]
    D = wh.shape[1]
    nj = D // tn
    return pl.pallas_call(
        _merge_kernel,
        grid=(T // tm, nj),
        in_specs=[pl.BlockSpec((tm, KH), lambda i, j: (i, 0)), pl.BlockSpec((tm, KA), lambda i, j: (i, 0)),
                  pl.BlockSpec((KH, tn), lambda i, j: (0, j)), pl.BlockSpec((KA, tn), lambda i, j: (0, j)),
                  pl.BlockSpec((tm, tn), lambda i, j: (i, j)), pl.BlockSpec((tm, tn), lambda i, j: (i, j + nj))],
        out_specs=pl.BlockSpec((tm, tn), lambda i, j: (i, j)),
        out_shape=jax.ShapeDtypeStruct((T, D), BF16),
        compiler_params=_params("parallel", "arbitrary"),
        name="gated_merge",
    )(oh, oa, wh, wa, gates, gates)


def _mlp_kernel(h_ref, wu_ref, wd_ref, o_ref):
    @pl.when(pl.program_id(1) == 0)
    def _():
        o_ref[...] = jnp.zeros_like(o_ref)

    u = jnp.square(jnp.maximum(_dot(h_ref[...], wu_ref[...]), 0.0)).astype(BF16)
    o_ref[...] += _dot(u, wd_ref[...])


def _mlp(h, wu, wd, tm, tf):
    T, D = h.shape
    FF = wu.shape[1]
    return pl.pallas_call(
        _mlp_kernel,
        grid=(T // tm, FF // tf),
        in_specs=[pl.BlockSpec((tm, D), lambda i, j: (i, 0)),
                  pl.BlockSpec((D, tf), lambda i, j: (0, j)),
                  pl.BlockSpec((tf, D), lambda i, j: (j, 0))],
        out_specs=pl.BlockSpec((tm, D), lambda i, j: (i, 0)),
        out_shape=jax.ShapeDtypeStruct((T, D), F32),
        compiler_params=_params("parallel", "arbitrary"),
        name="relu2_mlp",
    )(h, wu, wd)


def _ple_kernel(x_ref, p_ref, wg_ref, wp_ref, o_ref):
    x, p = x_ref[...], p_ref[...].astype(BF16)
    for cols in _col_slabs(o_ref.shape[1]):
        o_ref[:, cols] = _dot(p, wp_ref[:, cols]) * _sigmoid(_dot(x, wg_ref[:, cols]))


def _ple(xb, p, wg, wp, tm, tn):
    T, D = xb.shape
    P = p.shape[1]
    return pl.pallas_call(
        _ple_kernel,
        grid=(T // tm, D // tn),
        in_specs=[pl.BlockSpec((tm, D), lambda i, j: (i, 0)), pl.BlockSpec((tm, P), lambda i, j: (i, 0)),
                  pl.BlockSpec((D, tn), lambda i, j: (0, j)), pl.BlockSpec((P, tn), lambda i, j: (0, j))],
        out_specs=pl.BlockSpec((tm, tn), lambda i, j: (i, j)),
        out_shape=jax.ShapeDtypeStruct((T, D), F32),
        compiler_params=_params("parallel", "arbitrary"),
        name="ple_gate",
    )(xb, p, wg, wp)


def _rope_tables(seq):
    half = HEAD_DIM // 2
    inv_freq = ROPE_THETA ** (-jnp.arange(0, HEAD_DIM, 2, dtype=F32) / HEAD_DIM)
    ang = jnp.arange(seq, dtype=F32)[:, None] * inv_freq[None, :]
    cos, sin = jnp.cos(ang), jnp.sin(ang)
    assert cos.shape == (seq, half)
    return jnp.concatenate([cos, cos], axis=1), jnp.concatenate([-sin, sin], axis=1)


def kernel(x, p, norm_mix_pre, norm_mix_post, w_in, lb_fwd, lb_bwd, hgrn_norm, attn_sink, w_hgrn_proj,
           w_attn_proj, w_out, norm_mlp_pre, norm_mlp_post, w_mlp_up, w_mlp_down, w_ple, w_ple_gate, norm_ple):
    B, S, D = x.shape
    T = B * S
    depth = w_in.shape[0]
    HW = w_hgrn_proj.shape[1]
    AW = w_attn_proj.shape[1]
    KVW = AW // GQA_GROUP
    assert S % HGRN_ROWS == 0 and S % ATTN_BLOCK == 0 and hgrn_norm.shape[-1] == HEAD_DIM
    assert w_in.shape[2] == 5 * HW + AW + 2 * KVW + 2 * D

    tm = _pick(S, 1024, 512, 256, 128)
    tm_mlp = _pick(T, 512, 256, 128)
    tr = _pick(T, 256, 128)
    cos_t, sin_t = _rope_tables(S)
    vec = lambda a: a.reshape(1, -1)

    xf = x.reshape(T, D)
    for i in range(depth):
        off_aq, off_gate = 0, AW + 2 * KVW

        h = _norm_cast(xf, vec(norm_mix_pre[i]), tr)
        tn = lambda off, n: _pick(math.gcd(off, n), 1024, 512, 256, 128)
        prep, w_rest = _hgrn_prep(h, w_in[i][:, :5 * HW].astype(BF16), jnp.concatenate([lb_fwd, lb_bwd], axis=1),
                                  HW, i, tm, 2 * HEAD_DIM, w_full=w_in[i])
        if w_rest is None:
            w_rest = w_in[i][:, 5 * HW:].astype(BF16)
        mm = functools.partial(_mm, h, w_rest, tm=tm)
        tn_a = tn(off_aq, AW + 2 * KVW)
        nrb = S // tm
        rope_spec = pl.BlockSpec((tm, HEAD_DIM), lambda r, j: (r % nrb, 0))
        qkv = mm(off_aq, AW + 2 * KVW, functools.partial(_epi_qkv, tn=tn_a, n_q=AW, n_k=KVW), BF16, tn=tn_a,
                 extras=(cos_t, sin_t), extra_specs=(rope_spec, rope_spec), name="in_qkv")
        tn_g = tn(off_gate, 2 * D)
        later = [w_out[i], w_mlp_up[i], w_mlp_down[i], w_ple_gate[i], w_hgrn_proj[i], w_attn_proj[i]]
        steps = (T // tm) * (2 * D // tn_g)
        on_side = [a for a in later if _cast_rows_per_step(a, steps) is not None]
        gates, side = mm(off_gate, 2 * D, _epi_sigmoid, BF16, tn=tn_g, casts=on_side, name="in_gates") \
            if on_side else (mm(off_gate, 2 * D, _epi_sigmoid, BF16, tn=tn_g, name="in_gates"), [])
        side = iter(side)
        w_o, w_up, w_dn, w_pg, w_hp, w_ap = [
            next(side) if _cast_rows_per_step(a, steps) is not None else a.astype(BF16) for a in later]

        o_h = _hgrn(prep, vec(hgrn_norm[i]), B, S)
        o_a = _attn(qkv, attn_sink[i], B, S, AW, KVW, _pick(S, 512, 256, 128))
        y = _merge(o_h, o_a, w_hp, w_ap, gates, tm, _pick(D, 1024, 512, 256, 128))
        mix = _mm(y, w_o, 0, D, _epi_identity, F32, tm, _pick(D, 1024, 512, 256, 128), name="w_out")
        xf, h2 = _resid_norm(xf, mix, vec(norm_mix_post[i]), vec(norm_mlp_pre[i]), tr, renorm=True)

        d = _mlp(h2, w_up, w_dn, tm_mlp, _pick(w_mlp_up.shape[2], 1024, 512, 256))
        xf, xb = _resid_norm(xf, d, vec(norm_mlp_post[i]), vec(norm_mlp_post[i]), tr, renorm=False)

        eg = _ple(xb, p[i].reshape(T, -1), w_pg, w_ple[i].astype(BF16), tm, _pick(D, 1024, 512, 256, 128))
        xf = _resid_final(xf, eg, vec(norm_ple[i]), tr)
    return xf.reshape(B, S, D)
```

```python
import functools
import math

import jax
import jax.numpy as jnp
from jax import lax
from jax.experimental import pallas as pl
from jax.experimental.pallas import tpu as pltpu

F32 = jnp.float32
BF16 = jnp.bfloat16

EPS = 1e-6
HEAD_DIM = 128
GQA_GROUP = 4
WINDOW = 128
ATTN_BLOCK = 128
CHUNK = 64
HGRN_ROWS = 256
HGRN_UNROLL = 4
ROPE_THETA = 10000.0
MXU_COLS = 256
SUBLANES = 8
BF16_ROWS = 16
RESID_ROWS = 64
VMEM_LIMIT_BYTES = 62 * 1024 * 1024


def _params(*semantics):
    return pltpu.CompilerParams(dimension_semantics=semantics, vmem_limit_bytes=VMEM_LIMIT_BYTES)


def _pick(n, *cands):
    for c in cands:
        if n % c == 0:
            return c
    raise ValueError(f"no tile in {cands} divides {n}")


def _dot(a, b):
    return jnp.dot(a, b, preferred_element_type=F32)


def _dot_nt(a, b):
    return lax.dot_general(a, b, (((1,), (1,)), ((), ())), preferred_element_type=F32)


def _dot_tn(a, b):
    return lax.dot_general(a, b, (((0,), (0,)), ((), ())), preferred_element_type=F32)


def _sigmoid(x):
    return 1.0 / (1.0 + jnp.exp(-x))


def _rms(x, w):
    return x * lax.rsqrt(jnp.mean(x * x, axis=-1, keepdims=True) + EPS) * w


def _norm_cast_kernel(x_ref, w_ref, o_ref):
    o_ref[...] = _rms(x_ref[...], w_ref[...]).astype(o_ref.dtype)


def _norm_cast(x, w, tm):
    T, D = x.shape
    return pl.pallas_call(
        _norm_cast_kernel,
        grid=(T // tm,),
        in_specs=[pl.BlockSpec((tm, D), lambda i: (i, 0)), pl.BlockSpec((1, D), lambda i: (0, 0))],
        out_specs=pl.BlockSpec((tm, D), lambda i: (i, 0)),
        out_shape=jax.ShapeDtypeStruct((T, D), BF16),
        compiler_params=_params("parallel"),
        name="norm_cast",
    )(x, w)


def _resid_norm_kernel(x_ref, d_ref, w_ref, w2_ref, o_ref, h_ref, *, renorm):
    y = x_ref[...] + _rms(d_ref[...], w_ref[...])
    o_ref[...] = y
    h_ref[...] = (_rms(y, w2_ref[...]) if renorm else y).astype(h_ref.dtype)


def _resid_norm(x, d, w, w2, tm, renorm):
    T, D = x.shape
    row = pl.BlockSpec((tm, D), lambda i: (i, 0))
    vec = pl.BlockSpec((1, D), lambda i: (0, 0))
    return pl.pallas_call(
        functools.partial(_resid_norm_kernel, renorm=renorm),
        grid=(T // tm,),
        in_specs=[row, row, vec, vec],
        out_specs=[row, row],
        out_shape=[jax.ShapeDtypeStruct((T, D), F32), jax.ShapeDtypeStruct((T, D), BF16)],
        compiler_params=_params("parallel"),
        name="resid_norm",
    )(x, d, w, w2)


def _resid_final_kernel(x_ref, d_ref, w_ref, o_ref):
    o_ref[...] = x_ref[...] + _rms(d_ref[...], w_ref[...])


def _resid_final(x, d, w, tm):
    T, D = x.shape
    row = pl.BlockSpec((tm, D), lambda i: (i, 0))
    vec = pl.BlockSpec((1, D), lambda i: (0, 0))
    return pl.pallas_call(
        _resid_final_kernel,
        grid=(T // tm,),
        in_specs=[row, row, vec],
        out_specs=row,
        out_shape=jax.ShapeDtypeStruct((T, D), F32),
        compiler_params=_params("parallel"),
        name="resid_final",
    )(x, d, w)


def _col_slabs(width):
    step = min(width, MXU_COLS)
    return [slice(c, c + step) for c in range(0, width, step)]


def _mm_kernel(lhs_ref, w_ref, *rest, epilogue, n_casts):
    n_extra = len(rest) - 2 * n_casts - 1
    extra_refs, cast_in = rest[:n_extra], rest[n_extra:n_extra + n_casts]
    o_ref, cast_out = rest[n_extra + n_casts], rest[n_extra + n_casts + 1:]
    lhs = lhs_ref[...]
    for cols in _col_slabs(o_ref.shape[1]):
        o_ref[:, cols] = epilogue(_dot(lhs, w_ref[:, cols]), cols.start, *extra_refs).astype(o_ref.dtype)
    for src, dst in zip(cast_in, cast_out):
        dst[...] = src[...].astype(dst.dtype)


def _cast_rows_per_step(a, steps):
    rows = a.shape[0] // steps
    return rows if rows * steps == a.shape[0] and rows % BF16_ROWS == 0 else None


def _mm(lhs, w, col_off, n_cols, epilogue, out_dtype, tm, tn, extras=(), extra_specs=(), casts=(), name="mm"):
    T, K = lhs.shape
    off = col_off // tn
    nj = n_cols // tn
    steps = (T // tm) * nj
    cast_specs = [pl.BlockSpec((_cast_rows_per_step(a, steps), a.shape[1]), lambda i, j: (i * nj + j, 0))
                  for a in casts]
    res = pl.pallas_call(
        functools.partial(_mm_kernel, epilogue=epilogue, n_casts=len(casts)),
        grid=(T // tm, nj),
        in_specs=[pl.BlockSpec((tm, K), lambda i, j: (i, 0)),
                  pl.BlockSpec((K, tn), lambda i, j: (0, j + off))] + list(extra_specs) + cast_specs,
        out_specs=[pl.BlockSpec((tm, tn), lambda i, j: (i, j))] + cast_specs,
        out_shape=[jax.ShapeDtypeStruct((T, n_cols), out_dtype)]
        + [jax.ShapeDtypeStruct(a.shape, BF16) for a in casts],
        compiler_params=_params("parallel", "arbitrary"),
        name=name,
    )(lhs, w, *extras, *casts)
    return (res[0], list(res[1:])) if casts else res[0]


def _epi_sigmoid(acc, col0):
    return _sigmoid(acc)


def _epi_identity(acc, col0):
    return acc


def _epi_qkv(acc, col0, cos_ref, sin_ref, *, tn, n_q, n_k):
    cos, sin = cos_ref[...], sin_ref[...]
    heads = []
    for h in range(acc.shape[1] // HEAD_DIM):
        col = pl.program_id(1) * tn + col0 + h * HEAD_DIM
        xh = acc[:, h * HEAD_DIM:(h + 1) * HEAD_DIM]
        rot = (xh * cos + pltpu.roll(xh, HEAD_DIM // 2, 1) * sin) * jnp.where(col < n_q, HEAD_DIM ** -0.5, 1.0)
        heads.append(jnp.where(col < n_q + n_k, rot, xh))
    return jnp.concatenate(heads, axis=1)


def _chunk_cumprod(x, reverse):
    n = SUBLANES
    rid = lax.broadcasted_iota(jnp.int32, (n, x.shape[1]), 0)
    slabs = []
    for g in range(x.shape[0] // n):
        y = x[g * n:(g + 1) * n]
        s = 1
        while s < n:
            if reverse:
                y = y * jnp.where(rid < n - s, pltpu.roll(y, n - s, 0), 1.0)
            else:
                y = y * jnp.where(rid >= s, pltpu.roll(y, s, 0), 1.0)
            s *= 2
        slabs.append(y)
    order = range(len(slabs) - 1, -1, -1) if reverse else range(len(slabs))
    total = None
    for g in order:
        if total is not None:
            slabs[g] = slabs[g] * total
        total = slabs[g][0:1] if reverse else slabs[g][n - 1:n]
    return jnp.concatenate(slabs, axis=0)


def _hgrn_prep_kernel(h_ref, wq_ref, wff_ref, wfb_ref, wv_ref, wg_ref, lbf_ref, lbb_ref, *rest, layer):
    n_pieces = len(rest) - 10 - (1 if len(rest) > 10 else 0)
    pieces = rest[:n_pieces]
    qf_ref, kf_ref, ktf_ref, qb_ref, kb_ref, ktb_ref, v_ref, og_ref, df_ref, db_ref = rest[n_pieces:n_pieces + 10]
    if n_pieces:
        slab = rest[-1]
        cw = pieces[0].shape[1]
        for p, src in enumerate(pieces):
            slab[:, p * cw:(p + 1) * cw] = src[...].astype(slab.dtype)
    h = h_ref[...]
    rows, width = qf_ref.shape
    xf = [_dot(h, wff_ref[...]), _dot(h, wfb_ref[...])]
    xq = _dot(h, wq_ref[...])
    xg = _dot(h, wg_ref[...])
    og_ref[...] = (xg * _sigmoid(xg)).astype(og_ref.dtype)
    v_ref[...] = _dot(h, wv_ref[...]).astype(v_ref.dtype)
    lbs = []
    for lb_ref in (lbf_ref, lbb_ref):
        lbp = lb_ref[...]
        e = jnp.exp(lbp - jnp.max(lbp, axis=0, keepdims=True))
        lbs.append(jnp.sum(e[:layer + 1], axis=0, keepdims=True) / jnp.sum(e, axis=0, keepdims=True))
    for c in range(rows // CHUNK):
        rs = slice(c * CHUNK, (c + 1) * CHUNK)
        q = xq[rs] * _sigmoid(xq[rs]) * (HEAD_DIM ** -0.5)
        for x, lb, reverse, qo, ko, kto, do in (
                (xf[0], lbs[0], False, qf_ref, kf_ref, ktf_ref, df_ref),
                (xf[1], lbs[1], True, qb_ref, kb_ref, ktb_ref, db_ref)):
            f = lb + (1.0 - lb) * _sigmoid(x[rs])
            eb = _chunk_cumprod(f, reverse)
            decay = eb[0:1, :] if reverse else eb[CHUNK - 1:CHUNK, :]
            k_hat = (1.0 - f) / eb
            qo[rs, :] = (q * eb).astype(qo.dtype)
            ko[rs, :] = k_hat.astype(ko.dtype)
            kto[rs, :] = (k_hat * decay).astype(kto.dtype)
            do[c:c + 1, :] = decay


def _hgrn_prep(h, w, lbp, hw, layer, tm, tn, w_full=None):
    T, K = h.shape
    nseg = hw // tn
    steps = (T // tm) * nseg
    wspec = lambda seg: pl.BlockSpec((K, tn), lambda i, j: (0, seg * nseg + j))
    lbspec = lambda seg: pl.BlockSpec((lbp.shape[0], tn), lambda i, j: (0, seg * nseg + j))
    ospec = pl.BlockSpec((tm, tn), lambda i, j: (i, j))
    dspec = pl.BlockSpec((tm // CHUNK, tn), lambda i, j: (i, j))
    act = jax.ShapeDtypeStruct((T, hw), BF16)
    dec = jax.ShapeDtypeStruct((T // CHUNK, hw), F32)
    side_in, side_specs, side_out_spec, side_out_shape = [], [], [], []
    rows = None if w_full is None else _cast_rows_per_step(w_full, steps)
    if rows is not None:
        first, rest_cols = 5 * hw, w_full.shape[1] - 5 * hw
        cw = math.gcd(first, rest_cols)
        if cw % HEAD_DIM == 0:
            for p in range(rest_cols // cw):
                side_in.append(w_full)
                side_specs.append(pl.BlockSpec((rows, cw), lambda i, j, p=p: (i * nseg + j, first // cw + p)))
            side_out_spec = [pl.BlockSpec((rows, rest_cols), lambda i, j: (i * nseg + j, 0))]
            side_out_shape = [jax.ShapeDtypeStruct((K, rest_cols), BF16)]
    res = pl.pallas_call(
        functools.partial(_hgrn_prep_kernel, layer=layer),
        grid=(T // tm, nseg),
        in_specs=[pl.BlockSpec((tm, K), lambda i, j: (i, 0)), wspec(0), wspec(1), wspec(2), wspec(3), wspec(4),
                  lbspec(0), lbspec(1)] + side_specs,
        out_specs=[ospec] * 8 + [dspec] * 2 + side_out_spec,
        out_shape=[act] * 8 + [dec] * 2 + side_out_shape,
        compiler_params=_params("parallel", "arbitrary"),
        name="in_hgrn",
    )(h, w, w, w, w, w, lbp, lbp, *side_in)
    return res[:10], (res[10] if side_in else None)


def _hgrn_kernel(qf_ref, kf_ref, ktf_ref, qb_ref, kb_ref, ktb_ref, v_ref, og_ref, df_ref, db_ref, nw_ref, o_ref,
                 accf_ref, accb_ref, mf_ref, mb_ref, sf_ref, sb_ref, *, seq):
    R = HGRN_ROWS
    nsb = seq // R
    ncs = R // CHUNK

    row = lax.broadcasted_iota(jnp.int32, (R, R), 0)
    col = lax.broadcasted_iota(jnp.int32, (R, R), 1)
    same = (row // CHUNK) == (col // CHUNK)
    mf_ref[...] = jnp.where(same & (col <= row), 1.0, 0.0).astype(BF16)
    mb_ref[...] = jnp.where(same & (col >= row), 1.0, 0.0).astype(BF16)
    sf_ref[...] = jnp.zeros_like(sf_ref)
    sb_ref[...] = jnp.zeros_like(sb_ref)

    U = HGRN_UNROLL if nsb % HGRN_UNROLL == 0 else 1
    W = U * R
    nw = U * ncs
    blocks = [slice(u * R, (u + 1) * R) for u in range(U)]
    chunks = [slice(c * CHUNK, (c + 1) * CHUNK) for c in range(nw)]

    nt = nsb // U

    def finish_rows(r0, n):
        o = accf_ref[pl.ds(r0, n), :] + accb_ref[pl.ds(r0, n), :]
        y = _rms(o, nw_ref[...]) * og_ref[pl.ds(r0, n), :].astype(F32)
        o_ref[pl.ds(r0, n), :] = y.astype(o_ref.dtype)

    def body(t, carry, finalize):
        dirs = ((t, qf_ref, kf_ref, ktf_ref, df_ref, mf_ref, sf_ref, accf_ref, range(nw)),
                (nt - 1 - t, qb_ref, kb_ref, ktb_ref, db_ref, mb_ref, sb_ref, accb_ref, range(nw - 1, -1, -1)))
        r0s = [pl.multiple_of(d[0] * W, W) for d in dirs]
        qs = [d[1][pl.ds(r0, W), :] for d, r0 in zip(dirs, r0s)]
        vs = [v_ref[pl.ds(r0, W), :] for r0 in r0s]
        scores = []
        for d, r0, q in zip(dirs, r0s, qs):
            k = d[2][pl.ds(r0, W), :]
            scores.append([_dot_nt(q[bl], k[bl]) for bl in blocks])
        updates = []
        for d, r0, v in zip(dirs, r0s, vs):
            kt = d[3][pl.ds(r0, W), :]
            updates.append([_dot_tn(v[sl], kt[sl]) for sl in chunks])
        entering = []
        for d, upd in zip(dirs, updates):
            idx, d_ref, s_ref = d[0], d[4], d[6]
            st = s_ref[...]
            ent = [None] * nw
            for c in d[8]:
                ent[c] = st.astype(BF16)
                st = st * d_ref[pl.ds(idx * nw + c, 1), :] + upd[c]
            s_ref[...] = st
            entering.append(ent)
        inter = [[_dot_nt(q[sl], ent[c]) for c, sl in enumerate(chunks)] for q, ent in zip(qs, entering)]
        for d, r0, sc, v, o_inter in zip(dirs, r0s, scores, vs, inter):
            mask = d[5][...] > 0
            for u, bl in enumerate(blocks):
                o = _dot(jnp.where(mask, sc[u], 0.0).astype(BF16), v[bl])
                d[7][pl.ds(r0 + u * R, R), :] = o + jnp.concatenate(o_inter[u * ncs:(u + 1) * ncs], axis=0)
        if finalize:
            for r0 in r0s:
                finish_rows(r0, W)
        return carry

    if nt % 2 == 0:
        lax.fori_loop(0, nt // 2, functools.partial(body, finalize=False), 0)
        lax.fori_loop(nt // 2, nt, functools.partial(body, finalize=True), 0)
    else:
        lax.fori_loop(0, nt, functools.partial(body, finalize=False), 0)

        def finish(t, carry):
            finish_rows(pl.multiple_of(t * W, W), W)
            return carry

        lax.fori_loop(0, nt, finish, 0)


def _hgrn(prep, nw, batch, seq):
    *acts, df, db = prep
    T, HW = acts[0].shape
    H = HW // HEAD_DIM
    blk = pl.BlockSpec((seq, HEAD_DIM), lambda b, h: (b, h))
    dblk = pl.BlockSpec((seq // CHUNK, HEAD_DIM), lambda b, h: (b, h))
    return pl.pallas_call(
        functools.partial(_hgrn_kernel, seq=seq),
        grid=(batch, H),
        in_specs=[blk] * 8 + [dblk] * 2 + [pl.BlockSpec((1, HEAD_DIM), lambda b, h: (0, 0))],
        out_specs=blk,
        out_shape=jax.ShapeDtypeStruct((T, HW), BF16),
        scratch_shapes=[pltpu.VMEM((seq, HEAD_DIM), F32), pltpu.VMEM((seq, HEAD_DIM), F32),
                        pltpu.VMEM((HGRN_ROWS, HGRN_ROWS), BF16), pltpu.VMEM((HGRN_ROWS, HGRN_ROWS), BF16),
                        pltpu.VMEM((HEAD_DIM, HEAD_DIM), F32), pltpu.VMEM((HEAD_DIM, HEAD_DIM), F32)],
        compiler_params=_params("parallel", "parallel"),
        name="hgrn2",
    )(*acts, df, db, nw)


def _attn_kernel(sink_ref, q_ref, kp_ref, kc_ref, kn_ref, vp_ref, vc_ref, vn_ref, o_ref, *, seq):
    G, D, BLK = GQA_GROUP, HEAD_DIM, ATTN_BLOCK
    rows = q_ref.shape[0]
    kvh = pl.program_id(1)
    n = pl.program_id(2)
    k = jnp.concatenate([kp_ref[...], kc_ref[...], kn_ref[...]], axis=0)
    v = jnp.concatenate([vp_ref[...], vc_ref[...], vn_ref[...]], axis=0)
    v1 = jnp.concatenate([v, jnp.ones_like(v)], axis=1)
    r = lax.broadcasted_iota(jnp.int32, (G * BLK, 3 * BLK), 0) % BLK
    c = lax.broadcasted_iota(jnp.int32, (G * BLK, 3 * BLK), 1)
    band = jnp.where(jnp.abs(c - BLK - r) <= WINDOW, 0.0, -jnp.inf)
    kpos = n * rows - BLK + lax.broadcasted_iota(jnp.int32, (1, rows + 2 * BLK), 1)
    inside = jnp.where((kpos >= 0) & (kpos < seq), 0.0, -jnp.inf)
    sk = jnp.concatenate([jnp.full((BLK, BLK), sink_ref[kvh * G + h], F32) for h in range(G)], axis=0)
    nsub = rows // BLK
    scores = []
    for j in range(nsub):
        q = q_ref[j * BLK:(j + 1) * BLK, :]
        q4 = jnp.concatenate([q[:, h * D:(h + 1) * D] for h in range(G)], axis=0)
        scores.append(_dot_nt(q4, k[j * BLK:(j + 3) * BLK]))
    probs, sink_terms = [], []
    for j in range(nsub):
        s = scores[j] + band
        if j == 0 or j == nsub - 1:
            s = s + inside[:, j * BLK:(j + 3) * BLK]
        m = jnp.maximum(jnp.broadcast_to(jnp.max(s, axis=-1, keepdims=True), (G * BLK, BLK)), sk)
        probs.append(jnp.concatenate(
            [jnp.exp(s[:, i * BLK:(i + 1) * BLK] - m) for i in range(3)], axis=1).astype(BF16))
        sink_terms.append(jnp.exp(sk - m))
    for j in range(nsub):
        pv = _dot(probs[j], v1[j * BLK:(j + 3) * BLK])
        o = pv[:, :D] / (pv[:, D:] + sink_terms[j])
        o_ref[j * BLK:(j + 1) * BLK, :] = jnp.concatenate(
            [o[h * BLK:(h + 1) * BLK] for h in range(G)], axis=1).astype(o_ref.dtype)


def _attn(qkv, sink, batch, seq, n_q_cols, n_kv_cols, rows):
    T = qkv.shape[0]
    G, D, BLK = GQA_GROUP, HEAD_DIM, ATTN_BLOCK
    kvh = n_kv_cols // D
    nb = seq // BLK
    nq = seq // rows
    sub = rows // BLK
    k_off = n_q_cols // D
    v_off = (n_q_cols + n_kv_cols) // D
    qspec = pl.BlockSpec((rows, G * D), lambda b, h, n: (b * nq + n, h))
    prev = lambda b, n: b * nb + jnp.maximum(n * sub - 1, 0)
    nxt = lambda b, n: b * nb + jnp.minimum((n + 1) * sub, nb - 1)
    edge = lambda f, off: pl.BlockSpec((BLK, D), lambda b, h, n: (f(b, n), off + h))
    cur = lambda off: pl.BlockSpec((rows, D), lambda b, h, n: (b * nq + n, off + h))
    return pl.pallas_call(
        functools.partial(_attn_kernel, seq=seq),
        grid=(batch, kvh, nq),
        in_specs=[pl.BlockSpec(memory_space=pltpu.SMEM), qspec,
                  edge(prev, k_off), cur(k_off), edge(nxt, k_off), edge(prev, v_off), cur(v_off), edge(nxt, v_off)],
        out_specs=qspec,
        out_shape=jax.ShapeDtypeStruct((T, n_q_cols), BF16),
        compiler_params=_params("parallel", "parallel", "arbitrary"),
        name="swa_sink",
    )(sink, *([qkv] * 7))


def _merge_kernel(oh_ref, oa_ref, wh_ref, wa_ref, ga_ref, gb_ref, o_ref):
    oh, oa = oh_ref[...], oa_ref[...]
    for cols in _col_slabs(o_ref.shape[1]):
        ya = _dot(oh, wh_ref[:, cols])
        yb = _dot(oa, wa_ref[:, cols])
        o_ref[:, cols] = (ga_ref[:, cols].astype(F32) * ya + gb_ref[:, cols].astype(F32) * yb).astype(o_ref.dtype)


def _merge(oh, oa, wh, wa, gates, tm, tn):
    T, KH = oh.shape
    KA = oa.shape[1]
    D = wh.shape[1]
    nj = D // tn
    return pl.pallas_call(
        _merge_kernel,
        grid=(T // tm, nj),
        in_specs=[pl.BlockSpec((tm, KH), lambda i, j: (i, 0)), pl.BlockSpec((tm, KA), lambda i, j: (i, 0)),
                  pl.BlockSpec((KH, tn), lambda i, j: (0, j)), pl.BlockSpec((KA, tn), lambda i, j: (0, j)),
                  pl.BlockSpec((tm, tn), lambda i, j: (i, j)), pl.BlockSpec((tm, tn), lambda i, j: (i, j + nj))],
        out_specs=pl.BlockSpec((tm, tn), lambda i, j: (i, j)),
        out_shape=jax.ShapeDtypeStruct((T, D), BF16),
        compiler_params=_params("parallel", "arbitrary"),
        name="gated_merge",
    )(oh, oa, wh, wa, gates, gates)


def _mlp_kernel(h_ref, wu_ref, wd_ref, o_ref):
    @pl.when(pl.program_id(1) == 0)
    def _():
        o_ref[...] = jnp.zeros_like(o_ref)

    u = jnp.square(jnp.maximum(_dot(h_ref[...], wu_ref[...]), 0.0)).astype(BF16)
    o_ref[...] += _dot(u, wd_ref[...])


def _mlp(h, wu, wd, tm, tf):
    T, D = h.shape
    FF = wu.shape[1]
    return pl.pallas_call(
        _mlp_kernel,
        grid=(T // tm, FF // tf),
        in_specs=[pl.BlockSpec((tm, D), lambda i, j: (i, 0)),
                  pl.BlockSpec((D, tf), lambda i, j: (0, j)),
                  pl.BlockSpec((tf, D), lambda i, j: (j, 0))],
        out_specs=pl.BlockSpec((tm, D), lambda i, j: (i, 0)),
        out_shape=jax.ShapeDtypeStruct((T, D), F32),
        compiler_params=_params("parallel", "arbitrary"),
        name="relu2_mlp",
    )(h, wu, wd)


def _ple_kernel(x_ref, p_ref, wg_ref, wp_ref, r_ref, g_ref, o_ref, *, tn):
    j = pl.program_id(1)
    x, p = x_ref[...], p_ref[...].astype(BF16)
    for cols in _col_slabs(tn):
        width = cols.stop - cols.start
        dst = pl.ds(pl.multiple_of(j * tn + cols.start, width), width)
        o_ref[:, dst] = _dot(p, wp_ref[:, cols]) * _sigmoid(_dot(x, wg_ref[:, cols]))

    @pl.when(j == pl.num_programs(1) - 1)
    def _():
        g = g_ref[...]

        def rows(t, carry):
            rs = pl.ds(pl.multiple_of(t * RESID_ROWS, RESID_ROWS), RESID_ROWS)
            o_ref[rs, :] = r_ref[rs, :] + _rms(o_ref[rs, :], g)
            return carry

        lax.fori_loop(0, o_ref.shape[0] // RESID_ROWS, rows, 0)


def _ple(xb, p, wg, wp, r, g, tm, tn):
    T, D = xb.shape
    P = p.shape[1]
    row = lambda i, j: (i, 0)
    return pl.pallas_call(
        functools.partial(_ple_kernel, tn=tn),
        grid=(T // tm, D // tn),
        in_specs=[pl.BlockSpec((tm, D), row), pl.BlockSpec((tm, P), row),
                  pl.BlockSpec((D, tn), lambda i, j: (0, j)), pl.BlockSpec((P, tn), lambda i, j: (0, j)),
                  pl.BlockSpec((tm, D), row), pl.BlockSpec((1, D), lambda i, j: (0, 0))],
        out_specs=pl.BlockSpec((tm, D), row),
        out_shape=jax.ShapeDtypeStruct((T, D), F32),
        compiler_params=_params("parallel", "arbitrary"),
        name="ple_gate",
    )(xb, p, wg, wp, r, g)


def _rope_tables(seq):
    half = HEAD_DIM // 2
    inv_freq = ROPE_THETA ** (-jnp.arange(0, HEAD_DIM, 2, dtype=F32) / HEAD_DIM)
    ang = jnp.arange(seq, dtype=F32)[:, None] * inv_freq[None, :]
    cos, sin = jnp.cos(ang), jnp.sin(ang)
    assert cos.shape == (seq, half)
    return jnp.concatenate([cos, cos], axis=1), jnp.concatenate([-sin, sin], axis=1)


def kernel(x, p, norm_mix_pre, norm_mix_post, w_in, lb_fwd, lb_bwd, hgrn_norm, attn_sink, w_hgrn_proj,
           w_attn_proj, w_out, norm_mlp_pre, norm_mlp_post, w_mlp_up, w_mlp_down, w_ple, w_ple_gate, norm_ple):
    B, S, D = x.shape
    T = B * S
    depth = w_in.shape[0]
    HW = w_hgrn_proj.shape[1]
    AW = w_attn_proj.shape[1]
    KVW = AW // GQA_GROUP
    assert S % HGRN_ROWS == 0 and S % ATTN_BLOCK == 0 and hgrn_norm.shape[-1] == HEAD_DIM
    assert w_in.shape[2] == 5 * HW + AW + 2 * KVW + 2 * D

    tm = _pick(S, 1024, 512, 256, 128)
    tm_mlp = _pick(T, 512, 256, 128)
    tr = _pick(T, 256, 128)
    cos_t, sin_t = _rope_tables(S)
    vec = lambda a: a.reshape(1, -1)

    xf = x.reshape(T, D)
    for i in range(depth):
        off_aq, off_gate = 0, AW + 2 * KVW

        h = _norm_cast(xf, vec(norm_mix_pre[i]), tr)
        tn = lambda off, n: _pick(math.gcd(off, n), 1024, 512, 256, 128)
        prep, w_rest = _hgrn_prep(h, w_in[i][:, :5 * HW].astype(BF16), jnp.concatenate([lb_fwd, lb_bwd], axis=1),
                                  HW, i, tm, 2 * HEAD_DIM, w_full=w_in[i])
        if w_rest is None:
            w_rest = w_in[i][:, 5 * HW:].astype(BF16)
        mm = functools.partial(_mm, h, w_rest, tm=tm)
        tn_a = tn(off_aq, AW + 2 * KVW)
        nrb = S // tm
        rope_spec = pl.BlockSpec((tm, HEAD_DIM), lambda r, j: (r % nrb, 0))
        qkv = mm(off_aq, AW + 2 * KVW, functools.partial(_epi_qkv, tn=tn_a, n_q=AW, n_k=KVW), BF16, tn=tn_a,
                 extras=(cos_t, sin_t), extra_specs=(rope_spec, rope_spec), name="in_qkv")
        tn_g = tn(off_gate, 2 * D)
        later = [w_out[i], w_mlp_up[i], w_mlp_down[i], w_ple_gate[i], w_hgrn_proj[i], w_attn_proj[i]]
        steps = (T // tm) * (2 * D // tn_g)
        on_side = [a for a in later if _cast_rows_per_step(a, steps) is not None]
        gates, side = mm(off_gate, 2 * D, _epi_sigmoid, BF16, tn=tn_g, casts=on_side, name="in_gates") \
            if on_side else (mm(off_gate, 2 * D, _epi_sigmoid, BF16, tn=tn_g, name="in_gates"), [])
        side = iter(side)
        w_o, w_up, w_dn, w_pg, w_hp, w_ap = [
            next(side) if _cast_rows_per_step(a, steps) is not None else a.astype(BF16) for a in later]

        o_h = _hgrn(prep, vec(hgrn_norm[i]), B, S)
        o_a = _attn(qkv, attn_sink[i], B, S, AW, KVW, _pick(S, 512, 256, 128))
        y = _merge(o_h, o_a, w_hp, w_ap, gates, tm, _pick(D, 1024, 512, 256, 128))
        mix = _mm(y, w_o, 0, D, _epi_identity, F32, tm, _pick(D, 1024, 512, 256, 128), name="w_out")
        xf, h2 = _resid_norm(xf, mix, vec(norm_mix_post[i]), vec(norm_mlp_pre[i]), tr, renorm=True)

        d = _mlp(h2, w_up, w_dn, tm_mlp, _pick(w_mlp_up.shape[2], 1024, 512, 256))
        xf, xb = _resid_norm(xf, d, vec(norm_mlp_post[i]), vec(norm_mlp_post[i]), tr, renorm=False)

        xf = _ple(xb, p[i].reshape(T, -1), w_pg, w_ple[i].astype(BF16), xf, vec(norm_ple[i]),
                  tm_mlp, _pick(D, 512, 256, 128))
    return xf.reshape(B, S, D)
```

```python
import functools
import math

import jax
import jax.numpy as jnp
from jax import lax
from jax.experimental import pallas as pl
from jax.experimental.pallas import tpu as pltpu

F32 = jnp.float32
BF16 = jnp.bfloat16

EPS = 1e-6
HEAD_DIM = 128
GQA_GROUP = 4
WINDOW = 128
ATTN_BLOCK = 128
CHUNK = 64
HGRN_ROWS = 256
HGRN_UNROLL = 4
ROPE_THETA = 10000.0
MXU_COLS = 256
SUBLANES = 8
BF16_ROWS = 16
RESID_ROWS = 64
VMEM_LIMIT_BYTES = 62 * 1024 * 1024


def _params(*semantics):
    return pltpu.CompilerParams(dimension_semantics=semantics, vmem_limit_bytes=VMEM_LIMIT_BYTES)


def _pick(n, *cands):
    for c in cands:
        if n % c == 0:
            return c
    raise ValueError(f"no tile in {cands} divides {n}")


def _dot(a, b):
    return jnp.dot(a, b, preferred_element_type=F32)


def _dot_nt(a, b):
    return lax.dot_general(a, b, (((1,), (1,)), ((), ())), preferred_element_type=F32)


def _dot_tn(a, b):
    return lax.dot_general(a, b, (((0,), (0,)), ((), ())), preferred_element_type=F32)


def _sigmoid(x):
    return 1.0 / (1.0 + jnp.exp(-x))


def _rms(x, w):
    return x * lax.rsqrt(jnp.mean(x * x, axis=-1, keepdims=True) + EPS) * w


def _norm_cast_kernel(x_ref, w_ref, o_ref):
    o_ref[...] = _rms(x_ref[...], w_ref[...]).astype(o_ref.dtype)


def _norm_cast(x, w, tm):
    T, D = x.shape
    return pl.pallas_call(
        _norm_cast_kernel,
        grid=(T // tm,),
        in_specs=[pl.BlockSpec((tm, D), lambda i: (i, 0)), pl.BlockSpec((1, D), lambda i: (0, 0))],
        out_specs=pl.BlockSpec((tm, D), lambda i: (i, 0)),
        out_shape=jax.ShapeDtypeStruct((T, D), BF16),
        compiler_params=_params("parallel"),
        name="norm_cast",
    )(x, w)


def _resid_norm_kernel(x_ref, d_ref, w_ref, w2_ref, o_ref, h_ref, *, renorm):
    y = x_ref[...] + _rms(d_ref[...], w_ref[...])
    o_ref[...] = y
    h_ref[...] = (_rms(y, w2_ref[...]) if renorm else y).astype(h_ref.dtype)


def _resid_norm(x, d, w, w2, tm, renorm):
    T, D = x.shape
    row = pl.BlockSpec((tm, D), lambda i: (i, 0))
    vec = pl.BlockSpec((1, D), lambda i: (0, 0))
    return pl.pallas_call(
        functools.partial(_resid_norm_kernel, renorm=renorm),
        grid=(T // tm,),
        in_specs=[row, row, vec, vec],
        out_specs=[row, row],
        out_shape=[jax.ShapeDtypeStruct((T, D), F32), jax.ShapeDtypeStruct((T, D), BF16)],
        compiler_params=_params("parallel"),
        name="resid_norm",
    )(x, d, w, w2)


def _resid_final_kernel(x_ref, d_ref, w_ref, o_ref):
    o_ref[...] = x_ref[...] + _rms(d_ref[...], w_ref[...])


def _resid_final(x, d, w, tm):
    T, D = x.shape
    row = pl.BlockSpec((tm, D), lambda i: (i, 0))
    vec = pl.BlockSpec((1, D), lambda i: (0, 0))
    return pl.pallas_call(
        _resid_final_kernel,
        grid=(T // tm,),
        in_specs=[row, row, vec],
        out_specs=row,
        out_shape=jax.ShapeDtypeStruct((T, D), F32),
        compiler_params=_params("parallel"),
        name="resid_final",
    )(x, d, w)


def _col_slabs(width):
    step = min(width, MXU_COLS)
    return [slice(c, c + step) for c in range(0, width, step)]


def _mm_kernel(lhs_ref, w_ref, *rest, epilogue, n_casts):
    n_extra = len(rest) - 2 * n_casts - 1
    extra_refs, cast_in = rest[:n_extra], rest[n_extra:n_extra + n_casts]
    o_ref, cast_out = rest[n_extra + n_casts], rest[n_extra + n_casts + 1:]
    lhs = lhs_ref[...]
    for cols in _col_slabs(o_ref.shape[1]):
        o_ref[:, cols] = epilogue(_dot(lhs, w_ref[:, cols]), cols.start, *extra_refs).astype(o_ref.dtype)
    for src, dst in zip(cast_in, cast_out):
        dst[...] = src[...].astype(dst.dtype)


def _cast_rows_per_step(a, steps):
    rows = a.shape[0] // steps
    return rows if rows * steps == a.shape[0] and rows % BF16_ROWS == 0 else None


def _mm(lhs, w, col_off, n_cols, epilogue, out_dtype, tm, tn, extras=(), extra_specs=(), casts=(), name="mm"):
    T, K = lhs.shape
    off = col_off // tn
    nj = n_cols // tn
    steps = (T // tm) * nj
    cast_specs = [pl.BlockSpec((_cast_rows_per_step(a, steps), a.shape[1]), lambda i, j: (i * nj + j, 0))
                  for a in casts]
    res = pl.pallas_call(
        functools.partial(_mm_kernel, epilogue=epilogue, n_casts=len(casts)),
        grid=(T // tm, nj),
        in_specs=[pl.BlockSpec((tm, K), lambda i, j: (i, 0)),
                  pl.BlockSpec((K, tn), lambda i, j: (0, j + off))] + list(extra_specs) + cast_specs,
        out_specs=[pl.BlockSpec((tm, tn), lambda i, j: (i, j))] + cast_specs,
        out_shape=[jax.ShapeDtypeStruct((T, n_cols), out_dtype)]
        + [jax.ShapeDtypeStruct(a.shape, BF16) for a in casts],
        compiler_params=_params("parallel", "arbitrary"),
        name=name,
    )(lhs, w, *extras, *casts)
    return (res[0], list(res[1:])) if casts else res[0]


def _epi_sigmoid(acc, col0):
    return _sigmoid(acc)


def _epi_identity(acc, col0):
    return acc


def _epi_qkv(acc, col0, cos_ref, sin_ref, *, tn, n_q, n_k):
    cos, sin = cos_ref[...], sin_ref[...]
    heads = []
    for h in range(acc.shape[1] // HEAD_DIM):
        col = pl.program_id(1) * tn + col0 + h * HEAD_DIM
        xh = acc[:, h * HEAD_DIM:(h + 1) * HEAD_DIM]
        rot = (xh * cos + pltpu.roll(xh, HEAD_DIM // 2, 1) * sin) * jnp.where(col < n_q, HEAD_DIM ** -0.5, 1.0)
        heads.append(jnp.where(col < n_q + n_k, rot, xh))
    return jnp.concatenate(heads, axis=1)


def _chunk_cumprod(x, reverse):
    n = SUBLANES
    rid = lax.broadcasted_iota(jnp.int32, (n, x.shape[1]), 0)
    slabs = []
    for g in range(x.shape[0] // n):
        y = x[g * n:(g + 1) * n]
        s = 1
        while s < n:
            if reverse:
                y = y * jnp.where(rid < n - s, pltpu.roll(y, n - s, 0), 1.0)
            else:
                y = y * jnp.where(rid >= s, pltpu.roll(y, s, 0), 1.0)
            s *= 2
        slabs.append(y)
    order = range(len(slabs) - 1, -1, -1) if reverse else range(len(slabs))
    total = None
    for g in order:
        if total is not None:
            slabs[g] = slabs[g] * total
        total = slabs[g][0:1] if reverse else slabs[g][n - 1:n]
    return jnp.concatenate(slabs, axis=0)


def _hgrn_prep_kernel(h_ref, wq_ref, wff_ref, wfb_ref, wv_ref, wg_ref, lbf_ref, lbb_ref, *rest, layer):
    n_pieces = len(rest) - 10 - (1 if len(rest) > 10 else 0)
    pieces = rest[:n_pieces]
    qf_ref, kf_ref, ktf_ref, qb_ref, kb_ref, ktb_ref, v_ref, og_ref, df_ref, db_ref = rest[n_pieces:n_pieces + 10]
    if n_pieces:
        slab = rest[-1]
        cw = pieces[0].shape[1]
        for p, src in enumerate(pieces):
            slab[:, p * cw:(p + 1) * cw] = src[...].astype(slab.dtype)
    h = h_ref[...]
    rows, width = qf_ref.shape
    xf = [_dot(h, wff_ref[...]), _dot(h, wfb_ref[...])]
    xq = _dot(h, wq_ref[...])
    xg = _dot(h, wg_ref[...])
    og_ref[...] = (xg * _sigmoid(xg)).astype(og_ref.dtype)
    v_ref[...] = _dot(h, wv_ref[...]).astype(v_ref.dtype)
    lbs = []
    for lb_ref in (lbf_ref, lbb_ref):
        lbp = lb_ref[...]
        e = jnp.exp(lbp - jnp.max(lbp, axis=0, keepdims=True))
        lbs.append(jnp.sum(e[:layer + 1], axis=0, keepdims=True) / jnp.sum(e, axis=0, keepdims=True))
    for c in range(rows // CHUNK):
        rs = slice(c * CHUNK, (c + 1) * CHUNK)
        q = xq[rs] * _sigmoid(xq[rs]) * (HEAD_DIM ** -0.5)
        for x, lb, reverse, qo, ko, kto, do in (
                (xf[0], lbs[0], False, qf_ref, kf_ref, ktf_ref, df_ref),
                (xf[1], lbs[1], True, qb_ref, kb_ref, ktb_ref, db_ref)):
            f = lb + (1.0 - lb) * _sigmoid(x[rs])
            eb = _chunk_cumprod(f, reverse)
            decay = eb[0:1, :] if reverse else eb[CHUNK - 1:CHUNK, :]
            k_hat = (1.0 - f) / eb
            qo[rs, :] = (q * eb).astype(qo.dtype)
            ko[rs, :] = k_hat.astype(ko.dtype)
            kto[rs, :] = (k_hat * decay).astype(kto.dtype)
            do[c:c + 1, :] = decay


def _hgrn_prep(h, w, lbp, hw, layer, tm, tn, w_full=None):
    T, K = h.shape
    nseg = hw // tn
    steps = (T // tm) * nseg
    wspec = lambda seg: pl.BlockSpec((K, tn), lambda i, j: (0, seg * nseg + j))
    lbspec = lambda seg: pl.BlockSpec((lbp.shape[0], tn), lambda i, j: (0, seg * nseg + j))
    ospec = pl.BlockSpec((tm, tn), lambda i, j: (i, j))
    dspec = pl.BlockSpec((tm // CHUNK, tn), lambda i, j: (i, j))
    act = jax.ShapeDtypeStruct((T, hw), BF16)
    dec = jax.ShapeDtypeStruct((T // CHUNK, hw), F32)
    side_in, side_specs, side_out_spec, side_out_shape = [], [], [], []
    rows = None if w_full is None else _cast_rows_per_step(w_full, steps)
    if rows is not None:
        first, rest_cols = 5 * hw, w_full.shape[1] - 5 * hw
        cw = math.gcd(first, rest_cols)
        if cw % HEAD_DIM == 0:
            for p in range(rest_cols // cw):
                side_in.append(w_full)
                side_specs.append(pl.BlockSpec((rows, cw), lambda i, j, p=p: (i * nseg + j, first // cw + p)))
            side_out_spec = [pl.BlockSpec((rows, rest_cols), lambda i, j: (i * nseg + j, 0))]
            side_out_shape = [jax.ShapeDtypeStruct((K, rest_cols), BF16)]
    res = pl.pallas_call(
        functools.partial(_hgrn_prep_kernel, layer=layer),
        grid=(T // tm, nseg),
        in_specs=[pl.BlockSpec((tm, K), lambda i, j: (i, 0)), wspec(0), wspec(1), wspec(2), wspec(3), wspec(4),
                  lbspec(0), lbspec(1)] + side_specs,
        out_specs=[ospec] * 8 + [dspec] * 2 + side_out_spec,
        out_shape=[act] * 8 + [dec] * 2 + side_out_shape,
        compiler_params=_params("parallel", "arbitrary"),
        name="in_hgrn",
    )(h, w, w, w, w, w, lbp, lbp, *side_in)
    return res[:10], (res[10] if side_in else None)


def _hgrn_kernel(qf_ref, kf_ref, ktf_ref, qb_ref, kb_ref, ktb_ref, v_ref, og_ref, df_ref, db_ref, nw_ref, o_ref,
                 accf_ref, accb_ref, mf_ref, mb_ref, sf_ref, sb_ref, *, seq):
    R = HGRN_ROWS
    nsb = seq // R
    ncs = R // CHUNK

    row = lax.broadcasted_iota(jnp.int32, (R, R), 0)
    col = lax.broadcasted_iota(jnp.int32, (R, R), 1)
    same = (row // CHUNK) == (col // CHUNK)
    mf_ref[...] = jnp.where(same & (col <= row), 1.0, 0.0).astype(BF16)
    mb_ref[...] = jnp.where(same & (col >= row), 1.0, 0.0).astype(BF16)
    sf_ref[...] = jnp.zeros_like(sf_ref)
    sb_ref[...] = jnp.zeros_like(sb_ref)

    U = HGRN_UNROLL if nsb % HGRN_UNROLL == 0 else 1
    W = U * R
    nw = U * ncs
    blocks = [slice(u * R, (u + 1) * R) for u in range(U)]
    chunks = [slice(c * CHUNK, (c + 1) * CHUNK) for c in range(nw)]

    nt = nsb // U

    def finish_rows(r0, n):
        o = accf_ref[pl.ds(r0, n), :] + accb_ref[pl.ds(r0, n), :]
        y = _rms(o, nw_ref[...]) * og_ref[pl.ds(r0, n), :].astype(F32)
        o_ref[pl.ds(r0, n), :] = y.astype(o_ref.dtype)

    def body(t, carry, finalize):
        dirs = ((t, qf_ref, kf_ref, ktf_ref, df_ref, mf_ref, sf_ref, accf_ref, range(nw)),
                (nt - 1 - t, qb_ref, kb_ref, ktb_ref, db_ref, mb_ref, sb_ref, accb_ref, range(nw - 1, -1, -1)))
        r0s = [pl.multiple_of(d[0] * W, W) for d in dirs]
        qs = [d[1][pl.ds(r0, W), :] for d, r0 in zip(dirs, r0s)]
        vs = [v_ref[pl.ds(r0, W), :] for r0 in r0s]
        scores = []
        for d, r0, q in zip(dirs, r0s, qs):
            k = d[2][pl.ds(r0, W), :]
            scores.append([_dot_nt(q[bl], k[bl]) for bl in blocks])
        updates = []
        for d, r0, v in zip(dirs, r0s, vs):
            kt = d[3][pl.ds(r0, W), :]
            updates.append([_dot_tn(v[sl], kt[sl]) for sl in chunks])
        entering = []
        for d, upd in zip(dirs, updates):
            idx, d_ref, s_ref = d[0], d[4], d[6]
            st = s_ref[...]
            ent = [None] * nw
            for c in d[8]:
                ent[c] = st.astype(BF16)
                st = st * d_ref[pl.ds(idx * nw + c, 1), :] + upd[c]
            s_ref[...] = st
            entering.append(ent)
        inter = [[_dot_nt(q[sl], ent[c]) for c, sl in enumerate(chunks)] for q, ent in zip(qs, entering)]
        for d, r0, sc, v, o_inter in zip(dirs, r0s, scores, vs, inter):
            mask = d[5][...] > 0
            for u, bl in enumerate(blocks):
                o = _dot(jnp.where(mask, sc[u], 0.0).astype(BF16), v[bl])
                d[7][pl.ds(r0 + u * R, R), :] = o + jnp.concatenate(o_inter[u * ncs:(u + 1) * ncs], axis=0)
        if finalize:
            for r0 in r0s:
                finish_rows(r0, W)
        return carry

    if nt % 2 == 0:
        lax.fori_loop(0, nt // 2, functools.partial(body, finalize=False), 0)
        lax.fori_loop(nt // 2, nt, functools.partial(body, finalize=True), 0)
    else:
        lax.fori_loop(0, nt, functools.partial(body, finalize=False), 0)

        def finish(t, carry):
            finish_rows(pl.multiple_of(t * W, W), W)
            return carry

        lax.fori_loop(0, nt, finish, 0)


def _hgrn(prep, nw, batch, seq):
    *acts, df, db = prep
    T, HW = acts[0].shape
    H = HW // HEAD_DIM
    blk = pl.BlockSpec((seq, HEAD_DIM), lambda b, h: (b, h))
    dblk = pl.BlockSpec((seq // CHUNK, HEAD_DIM), lambda b, h: (b, h))
    return pl.pallas_call(
        functools.partial(_hgrn_kernel, seq=seq),
        grid=(batch, H),
        in_specs=[blk] * 8 + [dblk] * 2 + [pl.BlockSpec((1, HEAD_DIM), lambda b, h: (0, 0))],
        out_specs=blk,
        out_shape=jax.ShapeDtypeStruct((T, HW), BF16),
        scratch_shapes=[pltpu.VMEM((seq, HEAD_DIM), F32), pltpu.VMEM((seq, HEAD_DIM), F32),
                        pltpu.VMEM((HGRN_ROWS, HGRN_ROWS), BF16), pltpu.VMEM((HGRN_ROWS, HGRN_ROWS), BF16),
                        pltpu.VMEM((HEAD_DIM, HEAD_DIM), F32), pltpu.VMEM((HEAD_DIM, HEAD_DIM), F32)],
        compiler_params=_params("parallel", "parallel"),
        name="hgrn2",
    )(*acts, df, db, nw)


def _attn_kernel(sink_ref, q_ref, kp_ref, kc_ref, kn_ref, vp_ref, vc_ref, vn_ref, o_ref, *, seq):
    G, D, BLK = GQA_GROUP, HEAD_DIM, ATTN_BLOCK
    rows = q_ref.shape[0]
    kvh = pl.program_id(1)
    n = pl.program_id(2)
    k = jnp.concatenate([kp_ref[...], kc_ref[...], kn_ref[...]], axis=0)
    v = jnp.concatenate([vp_ref[...], vc_ref[...], vn_ref[...]], axis=0)
    v1 = jnp.concatenate([v, jnp.ones_like(v)], axis=1)
    r = lax.broadcasted_iota(jnp.int32, (G * BLK, 3 * BLK), 0) % BLK
    c = lax.broadcasted_iota(jnp.int32, (G * BLK, 3 * BLK), 1)
    band = jnp.where(jnp.abs(c - BLK - r) <= WINDOW, 0.0, -jnp.inf)
    kpos = n * rows - BLK + lax.broadcasted_iota(jnp.int32, (1, rows + 2 * BLK), 1)
    inside = jnp.where((kpos >= 0) & (kpos < seq), 0.0, -jnp.inf)
    sk = jnp.concatenate([jnp.full((BLK, BLK), sink_ref[kvh * G + h], F32) for h in range(G)], axis=0)
    nsub = rows // BLK
    scores = []
    for j in range(nsub):
        q = q_ref[j * BLK:(j + 1) * BLK, :]
        q4 = jnp.concatenate([q[:, h * D:(h + 1) * D] for h in range(G)], axis=0)
        scores.append(_dot_nt(q4, k[j * BLK:(j + 3) * BLK]))
    probs, sink_terms = [], []
    for j in range(nsub):
        s = scores[j] + band
        if j == 0 or j == nsub - 1:
            s = s + inside[:, j * BLK:(j + 3) * BLK]
        m = jnp.maximum(jnp.broadcast_to(jnp.max(s, axis=-1, keepdims=True), (G * BLK, BLK)), sk)
        probs.append(jnp.concatenate(
            [jnp.exp(s[:, i * BLK:(i + 1) * BLK] - m) for i in range(3)], axis=1).astype(BF16))
        sink_terms.append(jnp.exp(sk - m))
    for j in range(nsub):
        pv = _dot(probs[j], v1[j * BLK:(j + 3) * BLK])
        o = pv[:, :D] / (pv[:, D:] + sink_terms[j])
        o_ref[j * BLK:(j + 1) * BLK, :] = jnp.concatenate(
            [o[h * BLK:(h + 1) * BLK] for h in range(G)], axis=1).astype(o_ref.dtype)


def _attn(qkv, sink, batch, seq, n_q_cols, n_kv_cols, rows):
    T = qkv.shape[0]
    G, D, BLK = GQA_GROUP, HEAD_DIM, ATTN_BLOCK
    kvh = n_kv_cols // D
    nb = seq // BLK
    nq = seq // rows
    sub = rows // BLK
    k_off = n_q_cols // D
    v_off = (n_q_cols + n_kv_cols) // D
    qspec = pl.BlockSpec((rows, G * D), lambda b, h, n: (b * nq + n, h))
    prev = lambda b, n: b * nb + jnp.maximum(n * sub - 1, 0)
    nxt = lambda b, n: b * nb + jnp.minimum((n + 1) * sub, nb - 1)
    edge = lambda f, off: pl.BlockSpec((BLK, D), lambda b, h, n: (f(b, n), off + h))
    cur = lambda off: pl.BlockSpec((rows, D), lambda b, h, n: (b * nq + n, off + h))
    return pl.pallas_call(
        functools.partial(_attn_kernel, seq=seq),
        grid=(batch, kvh, nq),
        in_specs=[pl.BlockSpec(memory_space=pltpu.SMEM), qspec,
                  edge(prev, k_off), cur(k_off), edge(nxt, k_off), edge(prev, v_off), cur(v_off), edge(nxt, v_off)],
        out_specs=qspec,
        out_shape=jax.ShapeDtypeStruct((T, n_q_cols), BF16),
        compiler_params=_params("parallel", "parallel", "arbitrary"),
        name="swa_sink",
    )(sink, *([qkv] * 7))


def _merge_kernel(oh_ref, oa_ref, wh_ref, wa_ref, ga_ref, gb_ref, o_ref):
    oh, oa = oh_ref[...], oa_ref[...]
    for cols in _col_slabs(o_ref.shape[1]):
        ya = _dot(oh, wh_ref[:, cols])
        yb = _dot(oa, wa_ref[:, cols])
        o_ref[:, cols] = (ga_ref[:, cols].astype(F32) * ya + gb_ref[:, cols].astype(F32) * yb).astype(o_ref.dtype)


def _merge(oh, oa, wh, wa, gates, tm, tn):
    T, KH = oh.shape
    KA = oa.shape[1]
    D = wh.shape[1]
    nj = D // tn
    return pl.pallas_call(
        _merge_kernel,
        grid=(T // tm, nj),
        in_specs=[pl.BlockSpec((tm, KH), lambda i, j: (i, 0)), pl.BlockSpec((tm, KA), lambda i, j: (i, 0)),
                  pl.BlockSpec((KH, tn), lambda i, j: (0, j)), pl.BlockSpec((KA, tn), lambda i, j: (0, j)),
                  pl.BlockSpec((tm, tn), lambda i, j: (i, j)), pl.BlockSpec((tm, tn), lambda i, j: (i, j + nj))],
        out_specs=pl.BlockSpec((tm, tn), lambda i, j: (i, j)),
        out_shape=jax.ShapeDtypeStruct((T, D), BF16),
        compiler_params=_params("parallel", "arbitrary"),
        name="gated_merge",
    )(oh, oa, wh, wa, gates, gates)


def _mlp_kernel(h_ref, wu_ref, wd_ref, o_ref):
    @pl.when(pl.program_id(1) == 0)
    def _():
        o_ref[...] = jnp.zeros_like(o_ref)

    u = jnp.square(jnp.maximum(_dot(h_ref[...], wu_ref[...]), 0.0)).astype(BF16)
    o_ref[...] += _dot(u, wd_ref[...])


def _mlp(h, wu, wd, tm, tf):
    T, D = h.shape
    FF = wu.shape[1]
    return pl.pallas_call(
        _mlp_kernel,
        grid=(T // tm, FF // tf),
        in_specs=[pl.BlockSpec((tm, D), lambda i, j: (i, 0)),
                  pl.BlockSpec((D, tf), lambda i, j: (0, j)),
                  pl.BlockSpec((tf, D), lambda i, j: (j, 0))],
        out_specs=pl.BlockSpec((tm, D), lambda i, j: (i, 0)),
        out_shape=jax.ShapeDtypeStruct((T, D), F32),
        compiler_params=_params("parallel", "arbitrary"),
        name="relu2_mlp",
    )(h, wu, wd)


def _ple_kernel(r_ref, p_ref, wg_ref, wp_ref, g_ref, o_ref, xb_ref, *, tn):
    j = pl.program_id(1)

    @pl.when(j == 0)
    def _():
        xb_ref[...] = r_ref[...].astype(xb_ref.dtype)

    x, p = xb_ref[...], p_ref[...].astype(BF16)
    for cols in _col_slabs(tn):
        width = cols.stop - cols.start
        dst = pl.ds(pl.multiple_of(j * tn + cols.start, width), width)
        o_ref[:, dst] = _dot(p, wp_ref[:, cols]) * _sigmoid(_dot(x, wg_ref[:, cols]))

    @pl.when(j == pl.num_programs(1) - 1)
    def _():
        g = g_ref[...]

        def rows(t, carry):
            rs = pl.ds(pl.multiple_of(t * RESID_ROWS, RESID_ROWS), RESID_ROWS)
            o_ref[rs, :] = r_ref[rs, :] + _rms(o_ref[rs, :], g)
            return carry

        lax.fori_loop(0, o_ref.shape[0] // RESID_ROWS, rows, 0)


def _ple(r, p, wg, wp, g, tm, tn):
    T, D = r.shape
    P = p.shape[1]
    row = lambda i, j: (i, 0)
    return pl.pallas_call(
        functools.partial(_ple_kernel, tn=tn),
        grid=(T // tm, D // tn),
        in_specs=[pl.BlockSpec((tm, D), row), pl.BlockSpec((tm, P), row),
                  pl.BlockSpec((D, tn), lambda i, j: (0, j)), pl.BlockSpec((P, tn), lambda i, j: (0, j)),
                  pl.BlockSpec((1, D), lambda i, j: (0, 0))],
        out_specs=pl.BlockSpec((tm, D), row),
        out_shape=jax.ShapeDtypeStruct((T, D), F32),
        scratch_shapes=[pltpu.VMEM((tm, D), BF16)],
        compiler_params=_params("parallel", "arbitrary"),
        name="ple_gate",
    )(r, p, wg, wp, g)


def _rope_tables(seq):
    half = HEAD_DIM // 2
    inv_freq = ROPE_THETA ** (-jnp.arange(0, HEAD_DIM, 2, dtype=F32) / HEAD_DIM)
    ang = jnp.arange(seq, dtype=F32)[:, None] * inv_freq[None, :]
    cos, sin = jnp.cos(ang), jnp.sin(ang)
    assert cos.shape == (seq, half)
    return jnp.concatenate([cos, cos], axis=1), jnp.concatenate([-sin, sin], axis=1)


def kernel(x, p, norm_mix_pre, norm_mix_post, w_in, lb_fwd, lb_bwd, hgrn_norm, attn_sink, w_hgrn_proj,
           w_attn_proj, w_out, norm_mlp_pre, norm_mlp_post, w_mlp_up, w_mlp_down, w_ple, w_ple_gate, norm_ple):
    B, S, D = x.shape
    T = B * S
    depth = w_in.shape[0]
    HW = w_hgrn_proj.shape[1]
    AW = w_attn_proj.shape[1]
    KVW = AW // GQA_GROUP
    assert S % HGRN_ROWS == 0 and S % ATTN_BLOCK == 0 and hgrn_norm.shape[-1] == HEAD_DIM
    assert w_in.shape[2] == 5 * HW + AW + 2 * KVW + 2 * D

    tm = _pick(S, 1024, 512, 256, 128)
    tm_mlp = _pick(T, 512, 256, 128)
    tr = _pick(T, 256, 128)
    cos_t, sin_t = _rope_tables(S)
    vec = lambda a: a.reshape(1, -1)

    xf = x.reshape(T, D)
    for i in range(depth):
        off_aq, off_gate = 0, AW + 2 * KVW

        h = _norm_cast(xf, vec(norm_mix_pre[i]), tr)
        tn = lambda off, n: _pick(math.gcd(off, n), 1024, 512, 256, 128)
        prep, w_rest = _hgrn_prep(h, w_in[i][:, :5 * HW].astype(BF16), jnp.concatenate([lb_fwd, lb_bwd], axis=1),
                                  HW, i, tm, 2 * HEAD_DIM, w_full=w_in[i])
        if w_rest is None:
            w_rest = w_in[i][:, 5 * HW:].astype(BF16)
        mm = functools.partial(_mm, h, w_rest, tm=tm)
        tn_a = tn(off_aq, AW + 2 * KVW)
        nrb = S // tm
        rope_spec = pl.BlockSpec((tm, HEAD_DIM), lambda r, j: (r % nrb, 0))
        qkv = mm(off_aq, AW + 2 * KVW, functools.partial(_epi_qkv, tn=tn_a, n_q=AW, n_k=KVW), BF16, tn=tn_a,
                 extras=(cos_t, sin_t), extra_specs=(rope_spec, rope_spec), name="in_qkv")
        tn_g = tn(off_gate, 2 * D)
        later = [w_out[i], w_mlp_up[i], w_mlp_down[i], w_ple_gate[i], w_hgrn_proj[i], w_attn_proj[i]]
        steps = (T // tm) * (2 * D // tn_g)
        on_side = [a for a in later if _cast_rows_per_step(a, steps) is not None]
        gates, side = mm(off_gate, 2 * D, _epi_sigmoid, BF16, tn=tn_g, casts=on_side, name="in_gates") \
            if on_side else (mm(off_gate, 2 * D, _epi_sigmoid, BF16, tn=tn_g, name="in_gates"), [])
        side = iter(side)
        w_o, w_up, w_dn, w_pg, w_hp, w_ap = [
            next(side) if _cast_rows_per_step(a, steps) is not None else a.astype(BF16) for a in later]

        o_h = _hgrn(prep, vec(hgrn_norm[i]), B, S)
        o_a = _attn(qkv, attn_sink[i], B, S, AW, KVW, _pick(S, 512, 256, 128))
        y = _merge(o_h, o_a, w_hp, w_ap, gates, tm, _pick(D, 1024, 512, 256, 128))
        mix = _mm(y, w_o, 0, D, _epi_identity, F32, tm, _pick(D, 1024, 512, 256, 128), name="w_out")
        xf, h2 = _resid_norm(xf, mix, vec(norm_mix_post[i]), vec(norm_mlp_pre[i]), tr, renorm=True)

        d = _mlp(h2, w_up, w_dn, tm_mlp, _pick(w_mlp_up.shape[2], 1024, 512, 256))
        xf = _resid_final(xf, d, vec(norm_mlp_post[i]), tr)

        xf = _ple(xf, p[i].reshape(T, -1), w_pg, w_ple[i].astype(BF16), vec(norm_ple[i]),
                  tm_mlp, _pick(D, 1024, 512, 256, 128))
    return xf.reshape(B, S, D)
```

```python
import functools
import math

import jax
import jax.numpy as jnp
from jax import lax
from jax.experimental import pallas as pl
from jax.experimental.pallas import tpu as pltpu

F32 = jnp.float32
BF16 = jnp.bfloat16

EPS = 1e-6
HEAD_DIM = 128
GQA_GROUP = 4
WINDOW = 128
ATTN_BLOCK = 128
CHUNK = 64
HGRN_ROWS = 256
HGRN_UNROLL = 4
ROPE_THETA = 10000.0
MXU_COLS = 256
SUBLANES = 8
BF16_ROWS = 16
RESID_ROWS = 64
VMEM_LIMIT_BYTES = 62 * 1024 * 1024


def _params(*semantics):
    return pltpu.CompilerParams(dimension_semantics=semantics, vmem_limit_bytes=VMEM_LIMIT_BYTES)


def _pick(n, *cands):
    for c in cands:
        if n % c == 0:
            return c
    raise ValueError(f"no tile in {cands} divides {n}")


def _dot(a, b):
    return jnp.dot(a, b, preferred_element_type=F32)


def _dot_nt(a, b):
    return lax.dot_general(a, b, (((1,), (1,)), ((), ())), preferred_element_type=F32)


def _dot_tn(a, b):
    return lax.dot_general(a, b, (((0,), (0,)), ((), ())), preferred_element_type=F32)


def _sigmoid(x):
    return 1.0 / (1.0 + jnp.exp(-x))


def _rms(x, w):
    return x * lax.rsqrt(jnp.mean(x * x, axis=-1, keepdims=True) + EPS) * w


def _norm_cast_kernel(x_ref, w_ref, o_ref):
    o_ref[...] = _rms(x_ref[...], w_ref[...]).astype(o_ref.dtype)


def _norm_cast(x, w, tm):
    T, D = x.shape
    return pl.pallas_call(
        _norm_cast_kernel,
        grid=(T // tm,),
        in_specs=[pl.BlockSpec((tm, D), lambda i: (i, 0)), pl.BlockSpec((1, D), lambda i: (0, 0))],
        out_specs=pl.BlockSpec((tm, D), lambda i: (i, 0)),
        out_shape=jax.ShapeDtypeStruct((T, D), BF16),
        compiler_params=_params("parallel"),
        name="norm_cast",
    )(x, w)


def _resid_norm_kernel(x_ref, d_ref, w_ref, w2_ref, o_ref, h_ref):
    y = x_ref[...] + _rms(d_ref[...], w_ref[...])
    o_ref[...] = y
    h_ref[...] = _rms(y, w2_ref[...]).astype(h_ref.dtype)


def _resid_norm(x, d, w, w2, tm):
    T, D = x.shape
    row = pl.BlockSpec((tm, D), lambda i: (i, 0))
    vec = pl.BlockSpec((1, D), lambda i: (0, 0))
    return pl.pallas_call(
        _resid_norm_kernel,
        grid=(T // tm,),
        in_specs=[row, row, vec, vec],
        out_specs=[row, row],
        out_shape=[jax.ShapeDtypeStruct((T, D), F32), jax.ShapeDtypeStruct((T, D), BF16)],
        compiler_params=_params("parallel"),
        name="resid_norm",
    )(x, d, w, w2)


def _resid_kernel(x_ref, d_ref, w_ref, o_ref):
    o_ref[...] = x_ref[...] + _rms(d_ref[...], w_ref[...])


def _resid(x, d, w, tm):
    T, D = x.shape
    row = pl.BlockSpec((tm, D), lambda i: (i, 0))
    vec = pl.BlockSpec((1, D), lambda i: (0, 0))
    return pl.pallas_call(
        _resid_kernel,
        grid=(T // tm,),
        in_specs=[row, row, vec],
        out_specs=row,
        out_shape=jax.ShapeDtypeStruct((T, D), F32),
        compiler_params=_params("parallel"),
        name="resid",
    )(x, d, w)


def _col_slabs(width):
    step = min(width, MXU_COLS)
    return [slice(c, c + step) for c in range(0, width, step)]


def _mm_kernel(lhs_ref, w_ref, *rest, epilogue, n_casts):
    n_extra = len(rest) - 2 * n_casts - 1
    extra_refs, cast_in = rest[:n_extra], rest[n_extra:n_extra + n_casts]
    o_ref, cast_out = rest[n_extra + n_casts], rest[n_extra + n_casts + 1:]
    lhs = lhs_ref[...]
    for cols in _col_slabs(o_ref.shape[1]):
        o_ref[:, cols] = epilogue(_dot(lhs, w_ref[:, cols]), cols.start, *extra_refs).astype(o_ref.dtype)
    for src, dst in zip(cast_in, cast_out):
        dst[...] = src[...].astype(dst.dtype)


def _cast_rows_per_step(a, steps):
    rows = a.shape[0] // steps
    return rows if rows * steps == a.shape[0] and rows % BF16_ROWS == 0 else None


def _mm(lhs, w, col_off, n_cols, epilogue, out_dtype, tm, tn, extras=(), extra_specs=(), casts=None, name="mm"):
    with_casts, casts = casts is not None, list(casts or ())
    T, K = lhs.shape
    off = col_off // tn
    nj = n_cols // tn
    steps = (T // tm) * nj
    cast_specs = [pl.BlockSpec((_cast_rows_per_step(a, steps), a.shape[1]), lambda i, j: (i * nj + j, 0))
                  for a in casts]
    res = pl.pallas_call(
        functools.partial(_mm_kernel, epilogue=epilogue, n_casts=len(casts)),
        grid=(T // tm, nj),
        in_specs=[pl.BlockSpec((tm, K), lambda i, j: (i, 0)),
                  pl.BlockSpec((K, tn), lambda i, j: (0, j + off))] + list(extra_specs) + cast_specs,
        out_specs=[pl.BlockSpec((tm, tn), lambda i, j: (i, j))] + cast_specs,
        out_shape=[jax.ShapeDtypeStruct((T, n_cols), out_dtype)]
        + [jax.ShapeDtypeStruct(a.shape, BF16) for a in casts],
        compiler_params=_params("parallel", "arbitrary"),
        name=name,
    )(lhs, w, *extras, *casts)
    return (res[0], list(res[1:])) if with_casts else res[0]


def _epi_sigmoid(acc, col0):
    return _sigmoid(acc)


def _epi_identity(acc, col0):
    return acc


def _epi_qkv(acc, col0, cos_ref, sin_ref, *, tn, n_q, n_k):
    cos, sin = cos_ref[...], sin_ref[...]
    heads = []
    for h in range(acc.shape[1] // HEAD_DIM):
        col = pl.program_id(1) * tn + col0 + h * HEAD_DIM
        xh = acc[:, h * HEAD_DIM:(h + 1) * HEAD_DIM]
        rot = (xh * cos + pltpu.roll(xh, HEAD_DIM // 2, 1) * sin) * jnp.where(col < n_q, HEAD_DIM ** -0.5, 1.0)
        heads.append(jnp.where(col < n_q + n_k, rot, xh))
    return jnp.concatenate(heads, axis=1)


def _chunk_cumprod(x, reverse):
    n = SUBLANES
    rid = lax.broadcasted_iota(jnp.int32, (n, x.shape[1]), 0)
    slabs = []
    for g in range(x.shape[0] // n):
        y = x[g * n:(g + 1) * n]
        s = 1
        while s < n:
            if reverse:
                y = y * jnp.where(rid < n - s, pltpu.roll(y, n - s, 0), 1.0)
            else:
                y = y * jnp.where(rid >= s, pltpu.roll(y, s, 0), 1.0)
            s *= 2
        slabs.append(y)
    order = range(len(slabs) - 1, -1, -1) if reverse else range(len(slabs))
    total = None
    for g in order:
        if total is not None:
            slabs[g] = slabs[g] * total
        total = slabs[g][0:1] if reverse else slabs[g][n - 1:n]
    return jnp.concatenate(slabs, axis=0)


def _hgrn_prep_kernel(h_ref, wq_ref, wff_ref, wfb_ref, wv_ref, wg_ref, lbf_ref, lbb_ref, *rest, layer, n_pieces):
    pieces = rest[:n_pieces]
    qf_ref, kf_ref, ktf_ref, qb_ref, kb_ref, ktb_ref, v_ref, og_ref, df_ref, db_ref = rest[n_pieces:n_pieces + 10]
    if n_pieces:
        slab = rest[-1]
        cw = pieces[0].shape[1]
        for p, src in enumerate(pieces):
            slab[:, p * cw:(p + 1) * cw] = src[...].astype(slab.dtype)
    h = h_ref[...]
    rows, width = qf_ref.shape
    xf = [_dot(h, wff_ref[...]), _dot(h, wfb_ref[...])]
    xq = _dot(h, wq_ref[...])
    xg = _dot(h, wg_ref[...])
    og_ref[...] = (xg * _sigmoid(xg)).astype(og_ref.dtype)
    v_ref[...] = _dot(h, wv_ref[...]).astype(v_ref.dtype)
    lbs = []
    for lb_ref in (lbf_ref, lbb_ref):
        lbp = lb_ref[...]
        e = jnp.exp(lbp - jnp.max(lbp, axis=0, keepdims=True))
        lbs.append(jnp.sum(e[:layer + 1], axis=0, keepdims=True) / jnp.sum(e, axis=0, keepdims=True))
    for c in range(rows // CHUNK):
        rs = slice(c * CHUNK, (c + 1) * CHUNK)
        q = xq[rs] * _sigmoid(xq[rs]) * (HEAD_DIM ** -0.5)
        for x, lb, reverse, qo, ko, kto, do in (
                (xf[0], lbs[0], False, qf_ref, kf_ref, ktf_ref, df_ref),
                (xf[1], lbs[1], True, qb_ref, kb_ref, ktb_ref, db_ref)):
            f = lb + (1.0 - lb) * _sigmoid(x[rs])
            eb = _chunk_cumprod(f, reverse)
            decay = eb[0:1, :] if reverse else eb[CHUNK - 1:CHUNK, :]
            k_hat = (1.0 - f) / eb
            qo[rs, :] = (q * eb).astype(qo.dtype)
            ko[rs, :] = k_hat.astype(ko.dtype)
            kto[rs, :] = (k_hat * decay).astype(kto.dtype)
            do[c:c + 1, :] = decay


def _hgrn_prep(h, w, lbp, hw, layer, tm, tn, w_full=None):
    T, K = h.shape
    nseg = hw // tn
    steps = (T // tm) * nseg
    wspec = lambda seg: pl.BlockSpec((K, tn), lambda i, j: (0, seg * nseg + j))
    lbspec = lambda seg: pl.BlockSpec((lbp.shape[0], tn), lambda i, j: (0, seg * nseg + j))
    ospec = pl.BlockSpec((tm, tn), lambda i, j: (i, j))
    dspec = pl.BlockSpec((tm // CHUNK, tn), lambda i, j: (i, j))
    act = jax.ShapeDtypeStruct((T, hw), BF16)
    dec = jax.ShapeDtypeStruct((T // CHUNK, hw), F32)
    side_in, side_specs, side_out_spec, side_out_shape = [], [], [], []
    rows = None if w_full is None else _cast_rows_per_step(w_full, steps)
    if rows is not None:
        first, rest_cols = 5 * hw, w_full.shape[1] - 5 * hw
        cw = math.gcd(first, rest_cols)
        if cw % HEAD_DIM == 0:
            for p in range(rest_cols // cw):
                side_in.append(w_full)
                side_specs.append(pl.BlockSpec((rows, cw), lambda i, j, p=p: (i * nseg + j, first // cw + p)))
            side_out_spec = [pl.BlockSpec((rows, rest_cols), lambda i, j: (i * nseg + j, 0))]
            side_out_shape = [jax.ShapeDtypeStruct((K, rest_cols), BF16)]
    res = pl.pallas_call(
        functools.partial(_hgrn_prep_kernel, layer=layer, n_pieces=len(side_in)),
        grid=(T // tm, nseg),
        in_specs=[pl.BlockSpec((tm, K), lambda i, j: (i, 0)), wspec(0), wspec(1), wspec(2), wspec(3), wspec(4),
                  lbspec(0), lbspec(1)] + side_specs,
        out_specs=[ospec] * 8 + [dspec] * 2 + side_out_spec,
        out_shape=[act] * 8 + [dec] * 2 + side_out_shape,
        compiler_params=_params("parallel", "arbitrary"),
        name="in_hgrn",
    )(h, w, w, w, w, w, lbp, lbp, *side_in)
    return res[:10], (res[10] if side_in else None)


def _hgrn_kernel(qf_ref, kf_ref, ktf_ref, qb_ref, kb_ref, ktb_ref, v_ref, og_ref, df_ref, db_ref, nw_ref, o_ref,
                 accf_ref, accb_ref, mf_ref, mb_ref, sf_ref, sb_ref, *, seq):
    R = HGRN_ROWS
    nsb = seq // R
    ncs = R // CHUNK

    row = lax.broadcasted_iota(jnp.int32, (R, R), 0)
    col = lax.broadcasted_iota(jnp.int32, (R, R), 1)
    same = (row // CHUNK) == (col // CHUNK)
    mf_ref[...] = jnp.where(same & (col <= row), 1.0, 0.0).astype(BF16)
    mb_ref[...] = jnp.where(same & (col >= row), 1.0, 0.0).astype(BF16)
    sf_ref[...] = jnp.zeros_like(sf_ref)
    sb_ref[...] = jnp.zeros_like(sb_ref)

    U = HGRN_UNROLL if nsb % HGRN_UNROLL == 0 else 1
    W = U * R
    nw = U * ncs
    blocks = [slice(u * R, (u + 1) * R) for u in range(U)]
    chunks = [slice(c * CHUNK, (c + 1) * CHUNK) for c in range(nw)]

    nt = nsb // U

    def finish_rows(r0, n):
        o = accf_ref[pl.ds(r0, n), :] + accb_ref[pl.ds(r0, n), :]
        y = _rms(o, nw_ref[...]) * og_ref[pl.ds(r0, n), :].astype(F32)
        o_ref[pl.ds(r0, n), :] = y.astype(o_ref.dtype)

    def body(t, carry, finalize):
        dirs = ((t, qf_ref, kf_ref, ktf_ref, df_ref, mf_ref, sf_ref, accf_ref, range(nw)),
                (nt - 1 - t, qb_ref, kb_ref, ktb_ref, db_ref, mb_ref, sb_ref, accb_ref, range(nw - 1, -1, -1)))
        r0s = [pl.multiple_of(d[0] * W, W) for d in dirs]
        qs = [d[1][pl.ds(r0, W), :] for d, r0 in zip(dirs, r0s)]
        vs = [v_ref[pl.ds(r0, W), :] for r0 in r0s]
        scores = []
        for d, r0, q in zip(dirs, r0s, qs):
            k = d[2][pl.ds(r0, W), :]
            scores.append([_dot_nt(q[bl], k[bl]) for bl in blocks])
        updates = []
        for d, r0, v in zip(dirs, r0s, vs):
            kt = d[3][pl.ds(r0, W), :]
            updates.append([_dot_tn(v[sl], kt[sl]) for sl in chunks])
        entering = []
        for d, upd in zip(dirs, updates):
            idx, d_ref, s_ref = d[0], d[4], d[6]
            st = s_ref[...]
            ent = [None] * nw
            for c in d[8]:
                ent[c] = st.astype(BF16)
                st = st * d_ref[pl.ds(idx * nw + c, 1), :] + upd[c]
            s_ref[...] = st
            entering.append(ent)
        inter = [[_dot_nt(q[sl], ent[c]) for c, sl in enumerate(chunks)] for q, ent in zip(qs, entering)]
        for d, r0, sc, v, o_inter in zip(dirs, r0s, scores, vs, inter):
            mask = d[5][...] > 0
            for u, bl in enumerate(blocks):
                o = _dot(jnp.where(mask, sc[u], 0.0).astype(BF16), v[bl])
                d[7][pl.ds(r0 + u * R, R), :] = o + jnp.concatenate(o_inter[u * ncs:(u + 1) * ncs], axis=0)
        if finalize:
            for r0 in r0s:
                finish_rows(r0, W)
        return carry

    if nt % 2 == 0:
        lax.fori_loop(0, nt // 2, functools.partial(body, finalize=False), 0)
        lax.fori_loop(nt // 2, nt, functools.partial(body, finalize=True), 0)
    else:
        lax.fori_loop(0, nt, functools.partial(body, finalize=False), 0)

        def finish(t, carry):
            finish_rows(pl.multiple_of(t * W, W), W)
            return carry

        lax.fori_loop(0, nt, finish, 0)


def _hgrn(prep, nw, batch, seq):
    *acts, df, db = prep
    T, HW = acts[0].shape
    H = HW // HEAD_DIM
    blk = pl.BlockSpec((seq, HEAD_DIM), lambda b, h: (b, h))
    dblk = pl.BlockSpec((seq // CHUNK, HEAD_DIM), lambda b, h: (b, h))
    return pl.pallas_call(
        functools.partial(_hgrn_kernel, seq=seq),
        grid=(batch, H),
        in_specs=[blk] * 8 + [dblk] * 2 + [pl.BlockSpec((1, HEAD_DIM), lambda b, h: (0, 0))],
        out_specs=blk,
        out_shape=jax.ShapeDtypeStruct((T, HW), BF16),
        scratch_shapes=[pltpu.VMEM((seq, HEAD_DIM), F32), pltpu.VMEM((seq, HEAD_DIM), F32),
                        pltpu.VMEM((HGRN_ROWS, HGRN_ROWS), BF16), pltpu.VMEM((HGRN_ROWS, HGRN_ROWS), BF16),
                        pltpu.VMEM((HEAD_DIM, HEAD_DIM), F32), pltpu.VMEM((HEAD_DIM, HEAD_DIM), F32)],
        compiler_params=_params("parallel", "parallel"),
        name="hgrn2",
    )(*acts, df, db, nw)


def _attn_kernel(sink_ref, q_ref, kp_ref, kc_ref, kn_ref, vp_ref, vc_ref, vn_ref, o_ref, *, seq):
    G, D, BLK = GQA_GROUP, HEAD_DIM, ATTN_BLOCK
    rows = q_ref.shape[0]
    kvh = pl.program_id(1)
    n = pl.program_id(2)
    k = jnp.concatenate([kp_ref[...], kc_ref[...], kn_ref[...]], axis=0)
    v = jnp.concatenate([vp_ref[...], vc_ref[...], vn_ref[...]], axis=0)
    v1 = jnp.concatenate([v, jnp.ones_like(v)], axis=1)
    r = lax.broadcasted_iota(jnp.int32, (G * BLK, 3 * BLK), 0) % BLK
    c = lax.broadcasted_iota(jnp.int32, (G * BLK, 3 * BLK), 1)
    band = jnp.where(jnp.abs(c - BLK - r) <= WINDOW, 0.0, -jnp.inf)
    kpos = n * rows - BLK + lax.broadcasted_iota(jnp.int32, (1, rows + 2 * BLK), 1)
    inside = jnp.where((kpos >= 0) & (kpos < seq), 0.0, -jnp.inf)
    sk = jnp.concatenate([jnp.full((BLK, BLK), sink_ref[kvh * G + h], F32) for h in range(G)], axis=0)
    nsub = rows // BLK
    scores = []
    for j in range(nsub):
        q = q_ref[j * BLK:(j + 1) * BLK, :]
        q4 = jnp.concatenate([q[:, h * D:(h + 1) * D] for h in range(G)], axis=0)
        scores.append(_dot_nt(q4, k[j * BLK:(j + 3) * BLK]))
    probs, sink_terms = [], []
    for j in range(nsub):
        s = scores[j] + band
        if j == 0 or j == nsub - 1:
            s = s + inside[:, j * BLK:(j + 3) * BLK]
        m = jnp.maximum(jnp.broadcast_to(jnp.max(s, axis=-1, keepdims=True), (G * BLK, BLK)), sk)
        probs.append(jnp.concatenate(
            [jnp.exp(s[:, i * BLK:(i + 1) * BLK] - m) for i in range(3)], axis=1).astype(BF16))
        sink_terms.append(jnp.exp(sk - m))
    for j in range(nsub):
        pv = _dot(probs[j], v1[j * BLK:(j + 3) * BLK])
        o = pv[:, :D] / (pv[:, D:] + sink_terms[j])
        o_ref[j * BLK:(j + 1) * BLK, :] = jnp.concatenate(
            [o[h * BLK:(h + 1) * BLK] for h in range(G)], axis=1).astype(o_ref.dtype)


def _attn(qkv, sink, batch, seq, n_q_cols, n_kv_cols, rows):
    T = qkv.shape[0]
    G, D, BLK = GQA_GROUP, HEAD_DIM, ATTN_BLOCK
    kvh = n_kv_cols // D
    nb = seq // BLK
    nq = seq // rows
    sub = rows // BLK
    k_off = n_q_cols // D
    v_off = (n_q_cols + n_kv_cols) // D
    qspec = pl.BlockSpec((rows, G * D), lambda b, h, n: (b * nq + n, h))
    prev = lambda b, n: b * nb + jnp.maximum(n * sub - 1, 0)
    nxt = lambda b, n: b * nb + jnp.minimum((n + 1) * sub, nb - 1)
    edge = lambda f, off: pl.BlockSpec((BLK, D), lambda b, h, n: (f(b, n), off + h))
    cur = lambda off: pl.BlockSpec((rows, D), lambda b, h, n: (b * nq + n, off + h))
    return pl.pallas_call(
        functools.partial(_attn_kernel, seq=seq),
        grid=(batch, kvh, nq),
        in_specs=[pl.BlockSpec(memory_space=pltpu.SMEM), qspec,
                  edge(prev, k_off), cur(k_off), edge(nxt, k_off), edge(prev, v_off), cur(v_off), edge(nxt, v_off)],
        out_specs=qspec,
        out_shape=jax.ShapeDtypeStruct((T, n_q_cols), BF16),
        compiler_params=_params("parallel", "parallel", "arbitrary"),
        name="swa_sink",
    )(sink, *([qkv] * 7))


def _merge_kernel(oh_ref, oa_ref, wh_ref, wa_ref, ga_ref, gb_ref, o_ref):
    oh, oa = oh_ref[...], oa_ref[...]
    for cols in _col_slabs(o_ref.shape[1]):
        ya = _dot(oh, wh_ref[:, cols])
        yb = _dot(oa, wa_ref[:, cols])
        o_ref[:, cols] = (ga_ref[:, cols].astype(F32) * ya + gb_ref[:, cols].astype(F32) * yb).astype(o_ref.dtype)


def _merge(oh, oa, wh, wa, gates, tm, tn):
    T, KH = oh.shape
    KA = oa.shape[1]
    D = wh.shape[1]
    nj = D // tn
    return pl.pallas_call(
        _merge_kernel,
        grid=(T // tm, nj),
        in_specs=[pl.BlockSpec((tm, KH), lambda i, j: (i, 0)), pl.BlockSpec((tm, KA), lambda i, j: (i, 0)),
                  pl.BlockSpec((KH, tn), lambda i, j: (0, j)), pl.BlockSpec((KA, tn), lambda i, j: (0, j)),
                  pl.BlockSpec((tm, tn), lambda i, j: (i, j)), pl.BlockSpec((tm, tn), lambda i, j: (i, j + nj))],
        out_specs=pl.BlockSpec((tm, tn), lambda i, j: (i, j)),
        out_shape=jax.ShapeDtypeStruct((T, D), BF16),
        compiler_params=_params("parallel", "arbitrary"),
        name="gated_merge",
    )(oh, oa, wh, wa, gates, gates)


def _mlp_kernel(h_ref, wu_ref, wd_ref, o_ref):
    @pl.when(pl.program_id(1) == 0)
    def _():
        o_ref[...] = jnp.zeros_like(o_ref)

    u = jnp.square(jnp.maximum(_dot(h_ref[...], wu_ref[...]), 0.0)).astype(BF16)
    o_ref[...] += _dot(u, wd_ref[...])


def _mlp(h, wu, wd, tm, tf):
    T, D = h.shape
    FF = wu.shape[1]
    return pl.pallas_call(
        _mlp_kernel,
        grid=(T // tm, FF // tf),
        in_specs=[pl.BlockSpec((tm, D), lambda i, j: (i, 0)),
                  pl.BlockSpec((D, tf), lambda i, j: (0, j)),
                  pl.BlockSpec((tf, D), lambda i, j: (j, 0))],
        out_specs=pl.BlockSpec((tm, D), lambda i, j: (i, 0)),
        out_shape=jax.ShapeDtypeStruct((T, D), F32),
        compiler_params=_params("parallel", "arbitrary"),
        name="relu2_mlp",
    )(h, wu, wd)


def _ple_kernel(r_ref, p_ref, wg_ref, wp_ref, g_ref, o_ref, xb_ref, *, tn):
    j = pl.program_id(1)

    @pl.when(j == 0)
    def _():
        xb_ref[...] = r_ref[...].astype(xb_ref.dtype)

    x, p = xb_ref[...], p_ref[...].astype(BF16)
    for cols in _col_slabs(tn):
        width = cols.stop - cols.start
        dst = pl.ds(pl.multiple_of(j * tn + cols.start, width), width)
        o_ref[:, dst] = _dot(p, wp_ref[:, cols]) * _sigmoid(_dot(x, wg_ref[:, cols]))

    @pl.when(j == pl.num_programs(1) - 1)
    def _():
        g = g_ref[...]

        def rows(t, carry):
            rs = pl.ds(pl.multiple_of(t * RESID_ROWS, RESID_ROWS), RESID_ROWS)
            o_ref[rs, :] = r_ref[rs, :] + _rms(o_ref[rs, :], g)
            return carry

        lax.fori_loop(0, o_ref.shape[0] // RESID_ROWS, rows, 0)


def _ple(r, p, wg, wp, g, tm, tn):
    T, D = r.shape
    P = p.shape[1]
    row = lambda i, j: (i, 0)
    return pl.pallas_call(
        functools.partial(_ple_kernel, tn=tn),
        grid=(T // tm, D // tn),
        in_specs=[pl.BlockSpec((tm, D), row), pl.BlockSpec((tm, P), row),
                  pl.BlockSpec((D, tn), lambda i, j: (0, j)), pl.BlockSpec((P, tn), lambda i, j: (0, j)),
                  pl.BlockSpec((1, D), lambda i, j: (0, 0))],
        out_specs=pl.BlockSpec((tm, D), row),
        out_shape=jax.ShapeDtypeStruct((T, D), F32),
        scratch_shapes=[pltpu.VMEM((tm, D), BF16)],
        compiler_params=_params("parallel", "arbitrary"),
        name="ple_gate",
    )(r, p, wg, wp, g)


def _rope_tables(seq):
    half = HEAD_DIM // 2
    inv_freq = ROPE_THETA ** (-jnp.arange(0, HEAD_DIM, 2, dtype=F32) / HEAD_DIM)
    ang = jnp.arange(seq, dtype=F32)[:, None] * inv_freq[None, :]
    cos, sin = jnp.cos(ang), jnp.sin(ang)
    assert cos.shape == (seq, half)
    return jnp.concatenate([cos, cos], axis=1), jnp.concatenate([-sin, sin], axis=1)


def kernel(x, p, norm_mix_pre, norm_mix_post, w_in, lb_fwd, lb_bwd, hgrn_norm, attn_sink, w_hgrn_proj,
           w_attn_proj, w_out, norm_mlp_pre, norm_mlp_post, w_mlp_up, w_mlp_down, w_ple, w_ple_gate, norm_ple):
    B, S, D = x.shape
    T = B * S
    depth = w_in.shape[0]
    HW = w_hgrn_proj.shape[1]
    AW = w_attn_proj.shape[1]
    KVW = AW // GQA_GROUP
    assert S % HGRN_ROWS == 0 and S % ATTN_BLOCK == 0 and hgrn_norm.shape[-1] == HEAD_DIM
    assert w_in.shape[2] == 5 * HW + AW + 2 * KVW + 2 * D

    tm = _pick(S, 1024, 512, 256, 128)
    tm_mlp = _pick(T, 512, 256, 128)
    tr = _pick(T, 256, 128)
    cos_t, sin_t = _rope_tables(S)
    vec = lambda a: a.reshape(1, -1)

    xf = x.reshape(T, D)
    for i in range(depth):
        off_aq, off_gate = 0, AW + 2 * KVW

        h = _norm_cast(xf, vec(norm_mix_pre[i]), tr)
        tn = lambda off, n: _pick(math.gcd(off, n), 1024, 512, 256, 128)
        prep, w_rest = _hgrn_prep(h, w_in[i][:, :5 * HW].astype(BF16), jnp.concatenate([lb_fwd, lb_bwd], axis=1),
                                  HW, i, tm, 2 * HEAD_DIM, w_full=w_in[i])
        if w_rest is None:
            w_rest = w_in[i][:, 5 * HW:].astype(BF16)
        mm = functools.partial(_mm, h, w_rest, tm=tm)
        tn_a = tn(off_aq, AW + 2 * KVW)
        nrb = S // tm
        rope_spec = pl.BlockSpec((tm, HEAD_DIM), lambda r, j: (r % nrb, 0))
        qkv = mm(off_aq, AW + 2 * KVW, functools.partial(_epi_qkv, tn=tn_a, n_q=AW, n_k=KVW), BF16, tn=tn_a,
                 extras=(cos_t, sin_t), extra_specs=(rope_spec, rope_spec), name="in_qkv")
        tn_g = tn(off_gate, 2 * D)
        later = [w_out[i], w_mlp_up[i], w_mlp_down[i], w_ple_gate[i], w_hgrn_proj[i], w_attn_proj[i]]
        steps = (T // tm) * (2 * D // tn_g)
        on_side = [a for a in later if _cast_rows_per_step(a, steps) is not None]
        gates, side = mm(off_gate, 2 * D, _epi_sigmoid, BF16, tn=tn_g, casts=on_side, name="in_gates")
        side = iter(side)
        w_o, w_up, w_dn, w_pg, w_hp, w_ap = [
            next(side) if _cast_rows_per_step(a, steps) is not None else a.astype(BF16) for a in later]

        o_h = _hgrn(prep, vec(hgrn_norm[i]), B, S)
        o_a = _attn(qkv, attn_sink[i], B, S, AW, KVW, _pick(S, 1024, 512, 256, 128))
        y = _merge(o_h, o_a, w_hp, w_ap, gates, tm, _pick(D, 1024, 512, 256, 128))
        mix = _mm(y, w_o, 0, D, _epi_identity, F32, tm, _pick(D, 1024, 512, 256, 128), name="w_out")
        xf, h2 = _resid_norm(xf, mix, vec(norm_mix_post[i]), vec(norm_mlp_pre[i]), tr)

        d = _mlp(h2, w_up, w_dn, tm_mlp, _pick(w_mlp_up.shape[2], 1024, 512, 256))
        xf = _resid(xf, d, vec(norm_mlp_post[i]), tr)

        xf = _ple(xf, p[i].reshape(T, -1), w_pg, w_ple[i].astype(BF16), vec(norm_ple[i]),
                  tm_mlp, _pick(D, 1024, 512, 256, 128))
    return xf.reshape(B, S, D)
```

```python
import functools
import math

import jax
import jax.numpy as jnp
from jax import lax
from jax.experimental import pallas as pl
from jax.experimental.pallas import tpu as pltpu

F32 = jnp.float32
BF16 = jnp.bfloat16

EPS = 1e-6
HEAD_DIM = 128
GQA_GROUP = 4
WINDOW = 128
ATTN_BLOCK = 128
CHUNK = 64
HGRN_ROWS = 256
HGRN_UNROLL = 4
ROPE_THETA = 10000.0
MXU_COLS = 256
SUBLANES = 8
BF16_ROWS = 16
RESID_ROWS = 64
VMEM_LIMIT_BYTES = 62 * 1024 * 1024


def _params(*semantics):
    return pltpu.CompilerParams(dimension_semantics=semantics, vmem_limit_bytes=VMEM_LIMIT_BYTES)


def _pick(n, *cands):
    for c in cands:
        if n % c == 0:
            return c
    raise ValueError(f"no tile in {cands} divides {n}")


def _snake(i, j, nj):
    return j + (i % 2) * (nj - 1 - 2 * j)


def _dot(a, b):
    return jnp.dot(a, b, preferred_element_type=F32)


def _dot_nt(a, b):
    return lax.dot_general(a, b, (((1,), (1,)), ((), ())), preferred_element_type=F32)


def _dot_tn(a, b):
    return lax.dot_general(a, b, (((0,), (0,)), ((), ())), preferred_element_type=F32)


def _sigmoid(x):
    return 1.0 / (1.0 + jnp.exp(-x))


def _rms(x, w):
    return x * lax.rsqrt(jnp.mean(x * x, axis=-1, keepdims=True) + EPS) * w


def _norm_cast_kernel(x_ref, w_ref, o_ref):
    o_ref[...] = _rms(x_ref[...], w_ref[...]).astype(o_ref.dtype)


def _norm_cast(x, w, tm):
    T, D = x.shape
    return pl.pallas_call(
        _norm_cast_kernel,
        grid=(T // tm,),
        in_specs=[pl.BlockSpec((tm, D), lambda i: (i, 0)), pl.BlockSpec((1, D), lambda i: (0, 0))],
        out_specs=pl.BlockSpec((tm, D), lambda i: (i, 0)),
        out_shape=jax.ShapeDtypeStruct((T, D), BF16),
        compiler_params=_params("parallel"),
        name="norm_cast",
    )(x, w)


def _resid_norm_kernel(x_ref, d_ref, w_ref, w2_ref, o_ref, h_ref):
    y = x_ref[...] + _rms(d_ref[...], w_ref[...])
    o_ref[...] = y
    h_ref[...] = _rms(y, w2_ref[...]).astype(h_ref.dtype)


def _resid_norm(x, d, w, w2, tm):
    T, D = x.shape
    row = pl.BlockSpec((tm, D), lambda i: (i, 0))
    vec = pl.BlockSpec((1, D), lambda i: (0, 0))
    return pl.pallas_call(
        _resid_norm_kernel,
        grid=(T // tm,),
        in_specs=[row, row, vec, vec],
        out_specs=[row, row],
        out_shape=[jax.ShapeDtypeStruct((T, D), F32), jax.ShapeDtypeStruct((T, D), BF16)],
        compiler_params=_params("parallel"),
        name="resid_norm",
    )(x, d, w, w2)


def _resid_kernel(x_ref, d_ref, w_ref, o_ref):
    o_ref[...] = x_ref[...] + _rms(d_ref[...], w_ref[...])


def _resid(x, d, w, tm):
    T, D = x.shape
    row = pl.BlockSpec((tm, D), lambda i: (i, 0))
    vec = pl.BlockSpec((1, D), lambda i: (0, 0))
    return pl.pallas_call(
        _resid_kernel,
        grid=(T // tm,),
        in_specs=[row, row, vec],
        out_specs=row,
        out_shape=jax.ShapeDtypeStruct((T, D), F32),
        compiler_params=_params("parallel"),
        name="resid",
    )(x, d, w)


def _col_slabs(width):
    step = min(width, MXU_COLS)
    return [slice(c, c + step) for c in range(0, width, step)]


def _mm_kernel(lhs_ref, w_ref, *rest, epilogue, n_casts):
    n_extra = len(rest) - 2 * n_casts - 1
    extra_refs, cast_in = rest[:n_extra], rest[n_extra:n_extra + n_casts]
    o_ref, cast_out = rest[n_extra + n_casts], rest[n_extra + n_casts + 1:]
    lhs = lhs_ref[...]
    for cols in _col_slabs(o_ref.shape[1]):
        o_ref[:, cols] = epilogue(_dot(lhs, w_ref[:, cols]), cols.start, *extra_refs).astype(o_ref.dtype)
    for src, dst in zip(cast_in, cast_out):
        dst[...] = src[...].astype(dst.dtype)


def _cast_rows_per_step(a, steps):
    rows = a.shape[0] // steps
    return rows if rows * steps == a.shape[0] and rows % BF16_ROWS == 0 else None


def _mm(lhs, w, col_off, n_cols, epilogue, out_dtype, tm, tn, extras=(), extra_specs=(), casts=None, name="mm"):
    with_casts, casts = casts is not None, list(casts or ())
    T, K = lhs.shape
    off = col_off // tn
    nj = n_cols // tn
    steps = (T // tm) * nj
    cast_specs = [pl.BlockSpec((_cast_rows_per_step(a, steps), a.shape[1]), lambda i, j: (i * nj + j, 0))
                  for a in casts]
    res = pl.pallas_call(
        functools.partial(_mm_kernel, epilogue=epilogue, n_casts=len(casts)),
        grid=(T // tm, nj),
        in_specs=[pl.BlockSpec((tm, K), lambda i, j: (i, 0)),
                  pl.BlockSpec((K, tn), lambda i, j: (0, _snake(i, j, nj) + off))] + list(extra_specs) + cast_specs,
        out_specs=[pl.BlockSpec((tm, tn), lambda i, j: (i, _snake(i, j, nj)))] + cast_specs,
        out_shape=[jax.ShapeDtypeStruct((T, n_cols), out_dtype)]
        + [jax.ShapeDtypeStruct(a.shape, BF16) for a in casts],
        compiler_params=_params("parallel", "arbitrary"),
        name=name,
    )(lhs, w, *extras, *casts)
    return (res[0], list(res[1:])) if with_casts else res[0]


def _epi_sigmoid(acc, col0):
    return _sigmoid(acc)


def _epi_identity(acc, col0):
    return acc


def _epi_qkv(acc, col0, cos_ref, sin_ref, *, tn, n_q, n_k):
    cos, sin = cos_ref[...], sin_ref[...]
    heads = []
    for h in range(acc.shape[1] // HEAD_DIM):
        tile = _snake(pl.program_id(0), pl.program_id(1), pl.num_programs(1))
        col = tile * tn + col0 + h * HEAD_DIM
        xh = acc[:, h * HEAD_DIM:(h + 1) * HEAD_DIM]
        rot = (xh * cos + pltpu.roll(xh, HEAD_DIM // 2, 1) * sin) * jnp.where(col < n_q, HEAD_DIM ** -0.5, 1.0)
        heads.append(jnp.where(col < n_q + n_k, rot, xh))
    return jnp.concatenate(heads, axis=1)


def _chunk_cumprod(x, reverse):
    n = SUBLANES
    rid = lax.broadcasted_iota(jnp.int32, (n, x.shape[1]), 0)
    slabs = []
    for g in range(x.shape[0] // n):
        y = x[g * n:(g + 1) * n]
        s = 1
        while s < n:
            if reverse:
                y = y * jnp.where(rid < n - s, pltpu.roll(y, n - s, 0), 1.0)
            else:
                y = y * jnp.where(rid >= s, pltpu.roll(y, s, 0), 1.0)
            s *= 2
        slabs.append(y)
    order = range(len(slabs) - 1, -1, -1) if reverse else range(len(slabs))
    total = None
    for g in order:
        if total is not None:
            slabs[g] = slabs[g] * total
        total = slabs[g][0:1] if reverse else slabs[g][n - 1:n]
    return jnp.concatenate(slabs, axis=0)


def _hgrn_prep_kernel(h_ref, wq_ref, wff_ref, wfb_ref, wv_ref, wg_ref, lbf_ref, lbb_ref, *rest, layer, n_pieces):
    pieces = rest[:n_pieces]
    qf_ref, kf_ref, ktf_ref, qb_ref, kb_ref, ktb_ref, v_ref, og_ref, df_ref, db_ref = rest[n_pieces:n_pieces + 10]
    if n_pieces:
        slab = rest[-1]
        cw = pieces[0].shape[1]
        for p, src in enumerate(pieces):
            slab[:, p * cw:(p + 1) * cw] = src[...].astype(slab.dtype)
    h = h_ref[...]
    rows, width = qf_ref.shape
    xf = [_dot(h, wff_ref[...]), _dot(h, wfb_ref[...])]
    xq = _dot(h, wq_ref[...])
    xg = _dot(h, wg_ref[...])
    og_ref[...] = (xg * _sigmoid(xg)).astype(og_ref.dtype)
    v_ref[...] = _dot(h, wv_ref[...]).astype(v_ref.dtype)
    lbs = []
    for lb_ref in (lbf_ref, lbb_ref):
        lbp = lb_ref[...]
        e = jnp.exp(lbp - jnp.max(lbp, axis=0, keepdims=True))
        lbs.append(jnp.sum(e[:layer + 1], axis=0, keepdims=True) / jnp.sum(e, axis=0, keepdims=True))
    for c in range(rows // CHUNK):
        rs = slice(c * CHUNK, (c + 1) * CHUNK)
        q = xq[rs] * _sigmoid(xq[rs]) * (HEAD_DIM ** -0.5)
        for x, lb, reverse, qo, ko, kto, do in (
                (xf[0], lbs[0], False, qf_ref, kf_ref, ktf_ref, df_ref),
                (xf[1], lbs[1], True, qb_ref, kb_ref, ktb_ref, db_ref)):
            f = lb + (1.0 - lb) * _sigmoid(x[rs])
            eb = _chunk_cumprod(f, reverse)
            decay = eb[0:1, :] if reverse else eb[CHUNK - 1:CHUNK, :]
            k_hat = (1.0 - f) / eb
            qo[rs, :] = (q * eb).astype(qo.dtype)
            ko[rs, :] = k_hat.astype(ko.dtype)
            kto[rs, :] = (k_hat * decay).astype(kto.dtype)
            do[c:c + 1, :] = decay


def _hgrn_prep(h, w, lbp, hw, layer, tm, tn, w_full=None):
    T, K = h.shape
    nseg = hw // tn
    steps = (T // tm) * nseg
    wspec = lambda seg: pl.BlockSpec((K, tn), lambda i, j: (0, seg * nseg + j))
    lbspec = lambda seg: pl.BlockSpec((lbp.shape[0], tn), lambda i, j: (0, seg * nseg + j))
    ospec = pl.BlockSpec((tm, tn), lambda i, j: (i, j))
    dspec = pl.BlockSpec((tm // CHUNK, tn), lambda i, j: (i, j))
    act = jax.ShapeDtypeStruct((T, hw), BF16)
    dec = jax.ShapeDtypeStruct((T // CHUNK, hw), F32)
    side_in, side_specs, side_out_spec, side_out_shape = [], [], [], []
    rows = None if w_full is None else _cast_rows_per_step(w_full, steps)
    if rows is not None:
        first, rest_cols = 5 * hw, w_full.shape[1] - 5 * hw
        cw = math.gcd(first, rest_cols)
        if cw % HEAD_DIM == 0:
            for p in range(rest_cols // cw):
                side_in.append(w_full)
                side_specs.append(pl.BlockSpec((rows, cw), lambda i, j, p=p: (i * nseg + j, first // cw + p)))
            side_out_spec = [pl.BlockSpec((rows, rest_cols), lambda i, j: (i * nseg + j, 0))]
            side_out_shape = [jax.ShapeDtypeStruct((K, rest_cols), BF16)]
    res = pl.pallas_call(
        functools.partial(_hgrn_prep_kernel, layer=layer, n_pieces=len(side_in)),
        grid=(T // tm, nseg),
        in_specs=[pl.BlockSpec((tm, K), lambda i, j: (i, 0)), wspec(0), wspec(1), wspec(2), wspec(3), wspec(4),
                  lbspec(0), lbspec(1)] + side_specs,
        out_specs=[ospec] * 8 + [dspec] * 2 + side_out_spec,
        out_shape=[act] * 8 + [dec] * 2 + side_out_shape,
        compiler_params=_params("parallel", "arbitrary"),
        name="in_hgrn",
    )(h, w, w, w, w, w, lbp, lbp, *side_in)
    return res[:10], (res[10] if side_in else None)


def _hgrn_kernel(qf_ref, kf_ref, ktf_ref, qb_ref, kb_ref, ktb_ref, v_ref, og_ref, df_ref, db_ref, nw_ref, o_ref,
                 accf_ref, accb_ref, mf_ref, mb_ref, sf_ref, sb_ref, *, seq):
    R = HGRN_ROWS
    nsb = seq // R
    ncs = R // CHUNK

    row = lax.broadcasted_iota(jnp.int32, (R, R), 0)
    col = lax.broadcasted_iota(jnp.int32, (R, R), 1)
    same = (row // CHUNK) == (col // CHUNK)
    mf_ref[...] = jnp.where(same & (col <= row), 1.0, 0.0).astype(BF16)
    mb_ref[...] = jnp.where(same & (col >= row), 1.0, 0.0).astype(BF16)
    sf_ref[...] = jnp.zeros_like(sf_ref)
    sb_ref[...] = jnp.zeros_like(sb_ref)

    U = HGRN_UNROLL if nsb % HGRN_UNROLL == 0 else 1
    W = U * R
    nw = U * ncs
    blocks = [slice(u * R, (u + 1) * R) for u in range(U)]
    chunks = [slice(c * CHUNK, (c + 1) * CHUNK) for c in range(nw)]

    nt = nsb // U

    def finish_rows(r0, n):
        o = accf_ref[pl.ds(r0, n), :] + accb_ref[pl.ds(r0, n), :]
        y = _rms(o, nw_ref[...]) * og_ref[pl.ds(r0, n), :].astype(F32)
        o_ref[pl.ds(r0, n), :] = y.astype(o_ref.dtype)

    def body(t, carry, finalize):
        dirs = ((t, qf_ref, kf_ref, ktf_ref, df_ref, mf_ref, sf_ref, accf_ref, range(nw)),
                (nt - 1 - t, qb_ref, kb_ref, ktb_ref, db_ref, mb_ref, sb_ref, accb_ref, range(nw - 1, -1, -1)))
        r0s = [pl.multiple_of(d[0] * W, W) for d in dirs]
        qs = [d[1][pl.ds(r0, W), :] for d, r0 in zip(dirs, r0s)]
        vs = [v_ref[pl.ds(r0, W), :] for r0 in r0s]
        scores = []
        for d, r0, q in zip(dirs, r0s, qs):
            k = d[2][pl.ds(r0, W), :]
            scores.append([_dot_nt(q[bl], k[bl]) for bl in blocks])
        updates = []
        for d, r0, v in zip(dirs, r0s, vs):
            kt = d[3][pl.ds(r0, W), :]
            updates.append([_dot_tn(v[sl], kt[sl]) for sl in chunks])
        entering = []
        for d, upd in zip(dirs, updates):
            idx, d_ref, s_ref = d[0], d[4], d[6]
            st = s_ref[...]
            ent = [None] * nw
            for c in d[8]:
                ent[c] = st.astype(BF16)
                st = st * d_ref[pl.ds(idx * nw + c, 1), :] + upd[c]
            s_ref[...] = st
            entering.append(ent)
        inter = [[_dot_nt(q[sl], ent[c]) for c, sl in enumerate(chunks)] for q, ent in zip(qs, entering)]
        for d, r0, sc, v, o_inter in zip(dirs, r0s, scores, vs, inter):
            mask = d[5][...] > 0
            for u, bl in enumerate(blocks):
                o = _dot(jnp.where(mask, sc[u], 0.0).astype(BF16), v[bl])
                d[7][pl.ds(r0 + u * R, R), :] = o + jnp.concatenate(o_inter[u * ncs:(u + 1) * ncs], axis=0)
        if finalize:
            for r0 in r0s:
                finish_rows(r0, W)
        return carry

    if nt % 2 == 0:
        lax.fori_loop(0, nt // 2, functools.partial(body, finalize=False), 0)
        lax.fori_loop(nt // 2, nt, functools.partial(body, finalize=True), 0)
    else:
        lax.fori_loop(0, nt, functools.partial(body, finalize=False), 0)

        def finish(t, carry):
            finish_rows(pl.multiple_of(t * W, W), W)
            return carry

        lax.fori_loop(0, nt, finish, 0)


def _hgrn(prep, nw, batch, seq):
    *acts, df, db = prep
    T, HW = acts[0].shape
    H = HW // HEAD_DIM
    blk = pl.BlockSpec((seq, HEAD_DIM), lambda b, h: (b, h))
    dblk = pl.BlockSpec((seq // CHUNK, HEAD_DIM), lambda b, h: (b, h))
    return pl.pallas_call(
        functools.partial(_hgrn_kernel, seq=seq),
        grid=(batch, H),
        in_specs=[blk] * 8 + [dblk] * 2 + [pl.BlockSpec((1, HEAD_DIM), lambda b, h: (0, 0))],
        out_specs=blk,
        out_shape=jax.ShapeDtypeStruct((T, HW), BF16),
        scratch_shapes=[pltpu.VMEM((seq, HEAD_DIM), F32), pltpu.VMEM((seq, HEAD_DIM), F32),
                        pltpu.VMEM((HGRN_ROWS, HGRN_ROWS), BF16), pltpu.VMEM((HGRN_ROWS, HGRN_ROWS), BF16),
                        pltpu.VMEM((HEAD_DIM, HEAD_DIM), F32), pltpu.VMEM((HEAD_DIM, HEAD_DIM), F32)],
        compiler_params=_params("parallel", "parallel"),
        name="hgrn2",
    )(*acts, df, db, nw)


def _attn_kernel(sink_ref, q_ref, kp_ref, kc_ref, kn_ref, vp_ref, vc_ref, vn_ref, o_ref, *, seq):
    G, D, BLK = GQA_GROUP, HEAD_DIM, ATTN_BLOCK
    rows = q_ref.shape[0]
    kvh = pl.program_id(1)
    n = pl.program_id(2)
    k = jnp.concatenate([kp_ref[...], kc_ref[...], kn_ref[...]], axis=0)
    v = jnp.concatenate([vp_ref[...], vc_ref[...], vn_ref[...]], axis=0)
    v1 = jnp.concatenate([v, jnp.ones_like(v)], axis=1)
    r = lax.broadcasted_iota(jnp.int32, (G * BLK, 3 * BLK), 0) % BLK
    c = lax.broadcasted_iota(jnp.int32, (G * BLK, 3 * BLK), 1)
    band = jnp.where(jnp.abs(c - BLK - r) <= WINDOW, 0.0, -jnp.inf)
    kpos = n * rows - BLK + lax.broadcasted_iota(jnp.int32, (1, rows + 2 * BLK), 1)
    inside = jnp.where((kpos >= 0) & (kpos < seq), 0.0, -jnp.inf)
    sk = jnp.concatenate([jnp.full((BLK, BLK), sink_ref[kvh * G + h], F32) for h in range(G)], axis=0)
    nsub = rows // BLK
    scores = []
    for j in range(nsub):
        q = q_ref[j * BLK:(j + 1) * BLK, :]
        q4 = jnp.concatenate([q[:, h * D:(h + 1) * D] for h in range(G)], axis=0)
        scores.append(_dot_nt(q4, k[j * BLK:(j + 3) * BLK]))
    probs, sink_terms = [], []
    for j in range(nsub):
        s = scores[j] + band
        if j == 0 or j == nsub - 1:
            s = s + inside[:, j * BLK:(j + 3) * BLK]
        m = jnp.maximum(jnp.broadcast_to(jnp.max(s, axis=-1, keepdims=True), (G * BLK, BLK)), sk)
        probs.append(jnp.concatenate(
            [jnp.exp(s[:, i * BLK:(i + 1) * BLK] - m) for i in range(3)], axis=1).astype(BF16))
        sink_terms.append(jnp.exp(sk - m))
    for j in range(nsub):
        pv = _dot(probs[j], v1[j * BLK:(j + 3) * BLK])
        o = pv[:, :D] / (pv[:, D:] + sink_terms[j])
        o_ref[j * BLK:(j + 1) * BLK, :] = jnp.concatenate(
            [o[h * BLK:(h + 1) * BLK] for h in range(G)], axis=1).astype(o_ref.dtype)


def _attn(qkv, sink, batch, seq, n_q_cols, n_kv_cols, rows):
    T = qkv.shape[0]
    G, D, BLK = GQA_GROUP, HEAD_DIM, ATTN_BLOCK
    kvh = n_kv_cols // D
    nb = seq // BLK
    nq = seq // rows
    sub = rows // BLK
    k_off = n_q_cols // D
    v_off = (n_q_cols + n_kv_cols) // D
    qspec = pl.BlockSpec((rows, G * D), lambda b, h, n: (b * nq + n, h))
    prev = lambda b, n: b * nb + jnp.maximum(n * sub - 1, 0)
    nxt = lambda b, n: b * nb + jnp.minimum((n + 1) * sub, nb - 1)
    edge = lambda f, off: pl.BlockSpec((BLK, D), lambda b, h, n: (f(b, n), off + h))
    cur = lambda off: pl.BlockSpec((rows, D), lambda b, h, n: (b * nq + n, off + h))
    return pl.pallas_call(
        functools.partial(_attn_kernel, seq=seq),
        grid=(batch, kvh, nq),
        in_specs=[pl.BlockSpec(memory_space=pltpu.SMEM), qspec,
                  edge(prev, k_off), cur(k_off), edge(nxt, k_off), edge(prev, v_off), cur(v_off), edge(nxt, v_off)],
        out_specs=qspec,
        out_shape=jax.ShapeDtypeStruct((T, n_q_cols), BF16),
        compiler_params=_params("parallel", "parallel", "arbitrary"),
        name="swa_sink",
    )(sink, *([qkv] * 7))


def _merge_kernel(oh_ref, oa_ref, wh_ref, wa_ref, ga_ref, gb_ref, o_ref):
    oh, oa = oh_ref[...], oa_ref[...]
    for cols in _col_slabs(o_ref.shape[1]):
        ya = _dot(oh, wh_ref[:, cols])
        yb = _dot(oa, wa_ref[:, cols])
        o_ref[:, cols] = (ga_ref[:, cols].astype(F32) * ya + gb_ref[:, cols].astype(F32) * yb).astype(o_ref.dtype)


def _merge(oh, oa, wh, wa, gates, tm, tn):
    T, KH = oh.shape
    KA = oa.shape[1]
    D = wh.shape[1]
    nj = D // tn
    return pl.pallas_call(
        _merge_kernel,
        grid=(T // tm, nj),
        in_specs=[pl.BlockSpec((tm, KH), lambda i, j: (i, 0)), pl.BlockSpec((tm, KA), lambda i, j: (i, 0)),
                  pl.BlockSpec((KH, tn), lambda i, j: (0, _snake(i, j, nj))),
                  pl.BlockSpec((KA, tn), lambda i, j: (0, _snake(i, j, nj))),
                  pl.BlockSpec((tm, tn), lambda i, j: (i, _snake(i, j, nj))),
                  pl.BlockSpec((tm, tn), lambda i, j: (i, _snake(i, j, nj) + nj))],
        out_specs=pl.BlockSpec((tm, tn), lambda i, j: (i, _snake(i, j, nj))),
        out_shape=jax.ShapeDtypeStruct((T, D), BF16),
        compiler_params=_params("parallel", "arbitrary"),
        name="gated_merge",
    )(oh, oa, wh, wa, gates, gates)


def _mlp_kernel(h_ref, wu_ref, wd_ref, o_ref):
    @pl.when(pl.program_id(1) == 0)
    def _():
        o_ref[...] = jnp.zeros_like(o_ref)

    u = jnp.square(jnp.maximum(_dot(h_ref[...], wu_ref[...]), 0.0)).astype(BF16)
    o_ref[...] += _dot(u, wd_ref[...])


def _mlp(h, wu, wd, tm, tf):
    T, D = h.shape
    FF = wu.shape[1]
    return pl.pallas_call(
        _mlp_kernel,
        grid=(T // tm, FF // tf),
        in_specs=[pl.BlockSpec((tm, D), lambda i, j: (i, 0)),
                  pl.BlockSpec((D, tf), lambda i, j: (0, _snake(i, j, FF // tf))),
                  pl.BlockSpec((tf, D), lambda i, j: (_snake(i, j, FF // tf), 0))],
        out_specs=pl.BlockSpec((tm, D), lambda i, j: (i, 0)),
        out_shape=jax.ShapeDtypeStruct((T, D), F32),
        compiler_params=_params("parallel", "arbitrary"),
        name="relu2_mlp",
    )(h, wu, wd)


def _ple_kernel(r_ref, p_ref, wg_ref, wp_ref, g_ref, o_ref, xb_ref, *, tn):
    j = pl.program_id(1)
    tile = _snake(pl.program_id(0), j, pl.num_programs(1))

    @pl.when(j == 0)
    def _():
        xb_ref[...] = r_ref[...].astype(xb_ref.dtype)

    x, p = xb_ref[...], p_ref[...].astype(BF16)
    for cols in _col_slabs(tn):
        width = cols.stop - cols.start
        dst = pl.ds(pl.multiple_of(tile * tn + cols.start, width), width)
        o_ref[:, dst] = _dot(p, wp_ref[:, cols]) * _sigmoid(_dot(x, wg_ref[:, cols]))

    @pl.when(j == pl.num_programs(1) - 1)
    def _():
        g = g_ref[...]

        def rows(t, carry):
            rs = pl.ds(pl.multiple_of(t * RESID_ROWS, RESID_ROWS), RESID_ROWS)
            o_ref[rs, :] = r_ref[rs, :] + _rms(o_ref[rs, :], g)
            return carry

        lax.fori_loop(0, o_ref.shape[0] // RESID_ROWS, rows, 0)


def _ple(r, p, wg, wp, g, tm, tn):
    T, D = r.shape
    P = p.shape[1]
    row = lambda i, j: (i, 0)
    return pl.pallas_call(
        functools.partial(_ple_kernel, tn=tn),
        grid=(T // tm, D // tn),
        in_specs=[pl.BlockSpec((tm, D), row), pl.BlockSpec((tm, P), row),
                  pl.BlockSpec((D, tn), lambda i, j: (0, _snake(i, j, D // tn))),
                  pl.BlockSpec((P, tn), lambda i, j: (0, _snake(i, j, D // tn))),
                  pl.BlockSpec((1, D), lambda i, j: (0, 0))],
        out_specs=pl.BlockSpec((tm, D), row),
        out_shape=jax.ShapeDtypeStruct((T, D), F32),
        scratch_shapes=[pltpu.VMEM((tm, D), BF16)],
        compiler_params=_params("parallel", "arbitrary"),
        name="ple_gate",
    )(r, p, wg, wp, g)


def _rope_tables(seq):
    half = HEAD_DIM // 2
    inv_freq = ROPE_THETA ** (-jnp.arange(0, HEAD_DIM, 2, dtype=F32) / HEAD_DIM)
    ang = jnp.arange(seq, dtype=F32)[:, None] * inv_freq[None, :]
    cos, sin = jnp.cos(ang), jnp.sin(ang)
    assert cos.shape == (seq, half)
    return jnp.concatenate([cos, cos], axis=1), jnp.concatenate([-sin, sin], axis=1)


def kernel(x, p, norm_mix_pre, norm_mix_post, w_in, lb_fwd, lb_bwd, hgrn_norm, attn_sink, w_hgrn_proj,
           w_attn_proj, w_out, norm_mlp_pre, norm_mlp_post, w_mlp_up, w_mlp_down, w_ple, w_ple_gate, norm_ple):
    B, S, D = x.shape
    T = B * S
    depth = w_in.shape[0]
    HW = w_hgrn_proj.shape[1]
    AW = w_attn_proj.shape[1]
    KVW = AW // GQA_GROUP
    assert S % HGRN_ROWS == 0 and S % ATTN_BLOCK == 0 and hgrn_norm.shape[-1] == HEAD_DIM
    assert w_in.shape[2] == 5 * HW + AW + 2 * KVW + 2 * D

    tm = _pick(S, 1024, 512, 256, 128)
    tm_mlp = _pick(T, 512, 256, 128)
    tr = _pick(T, 256, 128)
    cos_t, sin_t = _rope_tables(S)
    vec = lambda a: a.reshape(1, -1)

    xf = x.reshape(T, D)
    for i in range(depth):
        off_aq, off_gate = 0, AW + 2 * KVW

        h = _norm_cast(xf, vec(norm_mix_pre[i]), tr)
        tn = lambda off, n: _pick(math.gcd(off, n), 1024, 512, 256, 128)
        prep, w_rest = _hgrn_prep(h, w_in[i][:, :5 * HW].astype(BF16), jnp.concatenate([lb_fwd, lb_bwd], axis=1),
                                  HW, i, tm, 2 * HEAD_DIM, w_full=w_in[i])
        if w_rest is None:
            w_rest = w_in[i][:, 5 * HW:].astype(BF16)
        mm = functools.partial(_mm, h, w_rest, tm=tm)
        tn_a = tn(off_aq, AW + 2 * KVW)
        nrb = S // tm
        rope_spec = pl.BlockSpec((tm, HEAD_DIM), lambda r, j: (r % nrb, 0))
        qkv = mm(off_aq, AW + 2 * KVW, functools.partial(_epi_qkv, tn=tn_a, n_q=AW, n_k=KVW), BF16, tn=tn_a,
                 extras=(cos_t, sin_t), extra_specs=(rope_spec, rope_spec), name="in_qkv")
        tn_g = tn(off_gate, 2 * D)
        later = [w_out[i], w_mlp_up[i], w_mlp_down[i], w_ple_gate[i], w_hgrn_proj[i], w_attn_proj[i]]
        steps = (T // tm) * (2 * D // tn_g)
        on_side = [a for a in later if _cast_rows_per_step(a, steps) is not None]
        gates, side = mm(off_gate, 2 * D, _epi_sigmoid, BF16, tn=tn_g, casts=on_side, name="in_gates")
        side = iter(side)
        w_o, w_up, w_dn, w_pg, w_hp, w_ap = [
            next(side) if _cast_rows_per_step(a, steps) is not None else a.astype(BF16) for a in later]

        o_h = _hgrn(prep, vec(hgrn_norm[i]), B, S)
        o_a = _attn(qkv, attn_sink[i], B, S, AW, KVW, _pick(S, 1024, 512, 256, 128))
        y = _merge(o_h, o_a, w_hp, w_ap, gates, tm, _pick(D, 1024, 512, 256, 128))
        mix = _mm(y, w_o, 0, D, _epi_identity, F32, tm, _pick(D, 1024, 512, 256, 128), name="w_out")
        xf, h2 = _resid_norm(xf, mix, vec(norm_mix_post[i]), vec(norm_mlp_pre[i]), tr)

        d = _mlp(h2, w_up, w_dn, tm_mlp, _pick(w_mlp_up.shape[2], 1024, 512, 256))
        xf = _resid(xf, d, vec(norm_mlp_post[i]), tr)

        xf = _ple(xf, p[i].reshape(T, -1), w_pg, w_ple[i].astype(BF16), vec(norm_ple[i]),
                  tm_mlp, _pick(D, 1024, 512, 256, 128))
    return xf.reshape(B, S, D)
```

```python
import functools
import math

import jax
import jax.numpy as jnp
from jax import lax
from jax.experimental import pallas as pl
from jax.experimental.pallas import tpu as pltpu

F32 = jnp.float32
BF16 = jnp.bfloat16

EPS = 1e-6
HEAD_DIM = 128
GQA_GROUP = 4
WINDOW = 128
ATTN_BLOCK = 128
CHUNK = 64
HGRN_ROWS = 256
HGRN_UNROLL = 8
ROPE_THETA = 10000.0
MXU_COLS = 256
SUBLANES = 8
BF16_ROWS = 16
RESID_ROWS = 64
VMEM_LIMIT_BYTES = 62 * 1024 * 1024


def _params(*semantics):
    return pltpu.CompilerParams(dimension_semantics=semantics, vmem_limit_bytes=VMEM_LIMIT_BYTES)


def _pick(n, *cands):
    for c in cands:
        if n % c == 0:
            return c
    raise ValueError(f"no tile in {cands} divides {n}")


def _dot(a, b):
    return jnp.dot(a, b, preferred_element_type=F32)


def _dot_nt(a, b):
    return lax.dot_general(a, b, (((1,), (1,)), ((), ())), preferred_element_type=F32)


def _dot_tn(a, b):
    return lax.dot_general(a, b, (((0,), (0,)), ((), ())), preferred_element_type=F32)


def _sigmoid(x):
    return 1.0 / (1.0 + jnp.exp(-x))


def _rms(x, w):
    return x * lax.rsqrt(jnp.mean(x * x, axis=-1, keepdims=True) + EPS) * w


def _norm_cast_kernel(x_ref, w_ref, o_ref):
    o_ref[...] = _rms(x_ref[...], w_ref[...]).astype(o_ref.dtype)


def _norm_cast(x, w, tm):
    T, D = x.shape
    return pl.pallas_call(
        _norm_cast_kernel,
        grid=(T // tm,),
        in_specs=[pl.BlockSpec((tm, D), lambda i: (i, 0)), pl.BlockSpec((1, D), lambda i: (0, 0))],
        out_specs=pl.BlockSpec((tm, D), lambda i: (i, 0)),
        out_shape=jax.ShapeDtypeStruct((T, D), BF16),
        compiler_params=_params("parallel"),
        name="norm_cast",
    )(x, w)


def _resid_norm_kernel(x_ref, d_ref, w_ref, w2_ref, o_ref, h_ref):
    y = x_ref[...] + _rms(d_ref[...], w_ref[...])
    o_ref[...] = y
    h_ref[...] = _rms(y, w2_ref[...]).astype(h_ref.dtype)


def _resid_norm(x, d, w, w2, tm):
    T, D = x.shape
    row = pl.BlockSpec((tm, D), lambda i: (i, 0))
    vec = pl.BlockSpec((1, D), lambda i: (0, 0))
    return pl.pallas_call(
        _resid_norm_kernel,
        grid=(T // tm,),
        in_specs=[row, row, vec, vec],
        out_specs=[row, row],
        out_shape=[jax.ShapeDtypeStruct((T, D), F32), jax.ShapeDtypeStruct((T, D), BF16)],
        compiler_params=_params("parallel"),
        name="resid_norm",
    )(x, d, w, w2)


def _resid_kernel(x_ref, d_ref, w_ref, o_ref):
    o_ref[...] = x_ref[...] + _rms(d_ref[...], w_ref[...])


def _resid(x, d, w, tm):
    T, D = x.shape
    row = pl.BlockSpec((tm, D), lambda i: (i, 0))
    vec = pl.BlockSpec((1, D), lambda i: (0, 0))
    return pl.pallas_call(
        _resid_kernel,
        grid=(T // tm,),
        in_specs=[row, row, vec],
        out_specs=row,
        out_shape=jax.ShapeDtypeStruct((T, D), F32),
        compiler_params=_params("parallel"),
        name="resid",
    )(x, d, w)


def _col_slabs(width):
    step = min(width, MXU_COLS)
    return [slice(c, c + step) for c in range(0, width, step)]


def _mm_kernel(lhs_ref, w_ref, *rest, epilogue, n_casts):
    n_extra = len(rest) - 2 * n_casts - 1
    extra_refs, cast_in = rest[:n_extra], rest[n_extra:n_extra + n_casts]
    o_ref, cast_out = rest[n_extra + n_casts], rest[n_extra + n_casts + 1:]
    lhs = lhs_ref[...]
    for cols in _col_slabs(o_ref.shape[1]):
        o_ref[:, cols] = epilogue(_dot(lhs, w_ref[:, cols]), cols.start, *extra_refs).astype(o_ref.dtype)
    for src, dst in zip(cast_in, cast_out):
        dst[...] = src[...].astype(dst.dtype)


def _cast_rows_per_step(a, steps):
    rows = a.shape[0] // steps
    return rows if rows * steps == a.shape[0] and rows % BF16_ROWS == 0 else None


def _mm(lhs, w, col_off, n_cols, epilogue, out_dtype, tm, tn, extras=(), extra_specs=(), casts=None, name="mm"):
    with_casts, casts = casts is not None, list(casts or ())
    T, K = lhs.shape
    off = col_off // tn
    nj = n_cols // tn
    steps = (T // tm) * nj
    cast_specs = [pl.BlockSpec((_cast_rows_per_step(a, steps), a.shape[1]), lambda i, j: (i * nj + j, 0))
                  for a in casts]
    res = pl.pallas_call(
        functools.partial(_mm_kernel, epilogue=epilogue, n_casts=len(casts)),
        grid=(T // tm, nj),
        in_specs=[pl.BlockSpec((tm, K), lambda i, j: (i, 0)),
                  pl.BlockSpec((K, tn), lambda i, j: (0, j + off))] + list(extra_specs) + cast_specs,
        out_specs=[pl.BlockSpec((tm, tn), lambda i, j: (i, j))] + cast_specs,
        out_shape=[jax.ShapeDtypeStruct((T, n_cols), out_dtype)]
        + [jax.ShapeDtypeStruct(a.shape, BF16) for a in casts],
        compiler_params=_params("parallel", "arbitrary"),
        name=name,
    )(lhs, w, *extras, *casts)
    return (res[0], list(res[1:])) if with_casts else res[0]


def _epi_sigmoid(acc, col0):
    return _sigmoid(acc)


def _epi_identity(acc, col0):
    return acc


def _epi_qkv(acc, col0, cos_ref, sin_ref, *, tn, n_q, n_k):
    cos, sin = cos_ref[...], sin_ref[...]
    heads = []
    for h in range(acc.shape[1] // HEAD_DIM):
        col = pl.program_id(1) * tn + col0 + h * HEAD_DIM
        xh = acc[:, h * HEAD_DIM:(h + 1) * HEAD_DIM]
        rot = (xh * cos + pltpu.roll(xh, HEAD_DIM // 2, 1) * sin) * jnp.where(col < n_q, HEAD_DIM ** -0.5, 1.0)
        heads.append(jnp.where(col < n_q + n_k, rot, xh))
    return jnp.concatenate(heads, axis=1)


def _chunk_cumprod(x, reverse):
    n = SUBLANES
    rid = lax.broadcasted_iota(jnp.int32, (n, x.shape[1]), 0)
    slabs = []
    for g in range(x.shape[0] // n):
        y = x[g * n:(g + 1) * n]
        s = 1
        while s < n:
            if reverse:
                y = y * jnp.where(rid < n - s, pltpu.roll(y, n - s, 0), 1.0)
            else:
                y = y * jnp.where(rid >= s, pltpu.roll(y, s, 0), 1.0)
            s *= 2
        slabs.append(y)
    order = range(len(slabs) - 1, -1, -1) if reverse else range(len(slabs))
    total = None
    for g in order:
        if total is not None:
            slabs[g] = slabs[g] * total
        total = slabs[g][0:1] if reverse else slabs[g][n - 1:n]
    return jnp.concatenate(slabs, axis=0)


def _hgrn_prep_kernel(h_ref, wq_ref, wff_ref, wfb_ref, wv_ref, wg_ref, lbf_ref, lbb_ref, *rest, layer, n_pieces):
    pieces = rest[:n_pieces]
    qf_ref, kf_ref, ktf_ref, qb_ref, kb_ref, ktb_ref, v_ref, og_ref, df_ref, db_ref = rest[n_pieces:n_pieces + 10]
    if n_pieces:
        slab = rest[-1]
        cw = pieces[0].shape[1]
        for p, src in enumerate(pieces):
            slab[:, p * cw:(p + 1) * cw] = src[...].astype(slab.dtype)
    h = h_ref[...]
    rows, width = qf_ref.shape
    xf = [_dot(h, wff_ref[...]), _dot(h, wfb_ref[...])]
    xq = _dot(h, wq_ref[...])
    xg = _dot(h, wg_ref[...])
    og_ref[...] = (xg * _sigmoid(xg)).astype(og_ref.dtype)
    v_ref[...] = _dot(h, wv_ref[...]).astype(v_ref.dtype)
    lbs = []
    for lb_ref in (lbf_ref, lbb_ref):
        lbp = lb_ref[...]
        e = jnp.exp(lbp - jnp.max(lbp, axis=0, keepdims=True))
        lbs.append(jnp.sum(e[:layer + 1], axis=0, keepdims=True) / jnp.sum(e, axis=0, keepdims=True))
    for c in range(rows // CHUNK):
        rs = slice(c * CHUNK, (c + 1) * CHUNK)
        q = xq[rs] * _sigmoid(xq[rs]) * (HEAD_DIM ** -0.5)
        for x, lb, reverse, qo, ko, kto, do in (
                (xf[0], lbs[0], False, qf_ref, kf_ref, ktf_ref, df_ref),
                (xf[1], lbs[1], True, qb_ref, kb_ref, ktb_ref, db_ref)):
            f = lb + (1.0 - lb) * _sigmoid(x[rs])
            eb = _chunk_cumprod(f, reverse)
            decay = eb[0:1, :] if reverse else eb[CHUNK - 1:CHUNK, :]
            k_hat = (1.0 - f) / eb
            qo[rs, :] = (q * eb).astype(qo.dtype)
            ko[rs, :] = k_hat.astype(ko.dtype)
            kto[rs, :] = (k_hat * decay).astype(kto.dtype)
            do[c:c + 1, :] = decay


def _hgrn_prep(h, w, lbp, hw, layer, tm, tn, w_full=None):
    T, K = h.shape
    nseg = hw // tn
    steps = (T // tm) * nseg
    wspec = lambda seg: pl.BlockSpec((K, tn), lambda i, j: (0, seg * nseg + j))
    lbspec = lambda seg: pl.BlockSpec((lbp.shape[0], tn), lambda i, j: (0, seg * nseg + j))
    ospec = pl.BlockSpec((tm, tn), lambda i, j: (i, j))
    dspec = pl.BlockSpec((tm // CHUNK, tn), lambda i, j: (i, j))
    act = jax.ShapeDtypeStruct((T, hw), BF16)
    dec = jax.ShapeDtypeStruct((T // CHUNK, hw), F32)
    side_in, side_specs, side_out_spec, side_out_shape = [], [], [], []
    rows = None if w_full is None else _cast_rows_per_step(w_full, steps)
    if rows is not None:
        first, rest_cols = 5 * hw, w_full.shape[1] - 5 * hw
        cw = math.gcd(first, rest_cols)
        if cw % HEAD_DIM == 0:
            for p in range(rest_cols // cw):
                side_in.append(w_full)
                side_specs.append(pl.BlockSpec((rows, cw), lambda i, j, p=p: (i * nseg + j, first // cw + p)))
            side_out_spec = [pl.BlockSpec((rows, rest_cols), lambda i, j: (i * nseg + j, 0))]
            side_out_shape = [jax.ShapeDtypeStruct((K, rest_cols), BF16)]
    res = pl.pallas_call(
        functools.partial(_hgrn_prep_kernel, layer=layer, n_pieces=len(side_in)),
        grid=(T // tm, nseg),
        in_specs=[pl.BlockSpec((tm, K), lambda i, j: (i, 0)), wspec(0), wspec(1), wspec(2), wspec(3), wspec(4),
                  lbspec(0), lbspec(1)] + side_specs,
        out_specs=[ospec] * 8 + [dspec] * 2 + side_out_spec,
        out_shape=[act] * 8 + [dec] * 2 + side_out_shape,
        compiler_params=_params("parallel", "arbitrary"),
        name="in_hgrn",
    )(h, w, w, w, w, w, lbp, lbp, *side_in)
    return res[:10], (res[10] if side_in else None)


def _hgrn_kernel(qf_ref, kf_ref, ktf_ref, qb_ref, kb_ref, ktb_ref, v_ref, og_ref, df_ref, db_ref, nw_ref, o_ref,
                 accf_ref, accb_ref, mf_ref, mb_ref, sf_ref, sb_ref, *, seq):
    R = HGRN_ROWS
    nsb = seq // R
    ncs = R // CHUNK

    row = lax.broadcasted_iota(jnp.int32, (R, R), 0)
    col = lax.broadcasted_iota(jnp.int32, (R, R), 1)
    same = (row // CHUNK) == (col // CHUNK)
    mf_ref[...] = jnp.where(same & (col <= row), 1.0, 0.0).astype(BF16)
    mb_ref[...] = jnp.where(same & (col >= row), 1.0, 0.0).astype(BF16)
    sf_ref[...] = jnp.zeros_like(sf_ref)
    sb_ref[...] = jnp.zeros_like(sb_ref)

    U = HGRN_UNROLL if nsb % HGRN_UNROLL == 0 else 1
    W = U * R
    nw = U * ncs
    blocks = [slice(u * R, (u + 1) * R) for u in range(U)]
    chunks = [slice(c * CHUNK, (c + 1) * CHUNK) for c in range(nw)]

    nt = nsb // U

    def finish_rows(r0, n):
        o = accf_ref[pl.ds(r0, n), :] + accb_ref[pl.ds(r0, n), :]
        y = _rms(o, nw_ref[...]) * og_ref[pl.ds(r0, n), :].astype(F32)
        o_ref[pl.ds(r0, n), :] = y.astype(o_ref.dtype)

    def body(t, carry, finalize):
        dirs = ((t, qf_ref, kf_ref, ktf_ref, df_ref, mf_ref, sf_ref, accf_ref, range(nw)),
                (nt - 1 - t, qb_ref, kb_ref, ktb_ref, db_ref, mb_ref, sb_ref, accb_ref, range(nw - 1, -1, -1)))
        r0s = [pl.multiple_of(d[0] * W, W) for d in dirs]
        qs = [d[1][pl.ds(r0, W), :] for d, r0 in zip(dirs, r0s)]
        vs = [v_ref[pl.ds(r0, W), :] for r0 in r0s]
        scores = []
        for d, r0, q in zip(dirs, r0s, qs):
            k = d[2][pl.ds(r0, W), :]
            scores.append([_dot_nt(q[bl], k[bl]) for bl in blocks])
        updates = []
        for d, r0, v in zip(dirs, r0s, vs):
            kt = d[3][pl.ds(r0, W), :]
            updates.append([_dot_tn(v[sl], kt[sl]) for sl in chunks])
        entering = []
        for d, upd in zip(dirs, updates):
            idx, d_ref, s_ref = d[0], d[4], d[6]
            st = s_ref[...]
            ent = [None] * nw
            for c in d[8]:
                ent[c] = st.astype(BF16)
                st = st * d_ref[pl.ds(idx * nw + c, 1), :] + upd[c]
            s_ref[...] = st
            entering.append(ent)
        inter = [[_dot_nt(q[sl], ent[c]) for c, sl in enumerate(chunks)] for q, ent in zip(qs, entering)]
        for d, r0, sc, v, o_inter in zip(dirs, r0s, scores, vs, inter):
            mask = d[5][...] > 0
            for u, bl in enumerate(blocks):
                o = _dot(jnp.where(mask, sc[u], 0.0).astype(BF16), v[bl])
                d[7][pl.ds(r0 + u * R, R), :] = o + jnp.concatenate(o_inter[u * ncs:(u + 1) * ncs], axis=0)
        if finalize:
            for r0 in r0s:
                finish_rows(r0, W)
        return carry

    if nt % 2 == 0:
        lax.fori_loop(0, nt // 2, functools.partial(body, finalize=False), 0)
        lax.fori_loop(nt // 2, nt, functools.partial(body, finalize=True), 0)
    else:
        lax.fori_loop(0, nt, functools.partial(body, finalize=False), 0)

        def finish(t, carry):
            finish_rows(pl.multiple_of(t * W, W), W)
            return carry

        lax.fori_loop(0, nt, finish, 0)


def _hgrn(prep, nw, batch, seq):
    *acts, df, db = prep
    T, HW = acts[0].shape
    H = HW // HEAD_DIM
    blk = pl.BlockSpec((seq, HEAD_DIM), lambda b, h: (b, h))
    dblk = pl.BlockSpec((seq // CHUNK, HEAD_DIM), lambda b, h: (b, h))
    return pl.pallas_call(
        functools.partial(_hgrn_kernel, seq=seq),
        grid=(batch, H),
        in_specs=[blk] * 8 + [dblk] * 2 + [pl.BlockSpec((1, HEAD_DIM), lambda b, h: (0, 0))],
        out_specs=blk,
        out_shape=jax.ShapeDtypeStruct((T, HW), BF16),
        scratch_shapes=[pltpu.VMEM((seq, HEAD_DIM), F32), pltpu.VMEM((seq, HEAD_DIM), F32),
                        pltpu.VMEM((HGRN_ROWS, HGRN_ROWS), BF16), pltpu.VMEM((HGRN_ROWS, HGRN_ROWS), BF16),
                        pltpu.VMEM((HEAD_DIM, HEAD_DIM), F32), pltpu.VMEM((HEAD_DIM, HEAD_DIM), F32)],
        compiler_params=_params("parallel", "parallel"),
        name="hgrn2",
    )(*acts, df, db, nw)


def _attn_kernel(sink_ref, q_ref, kp_ref, kc_ref, kn_ref, vp_ref, vc_ref, vn_ref, o_ref, *, seq):
    G, D, BLK = GQA_GROUP, HEAD_DIM, ATTN_BLOCK
    rows = q_ref.shape[0]
    kvh = pl.program_id(1)
    n = pl.program_id(2)
    k = jnp.concatenate([kp_ref[...], kc_ref[...], kn_ref[...]], axis=0)
    v = jnp.concatenate([vp_ref[...], vc_ref[...], vn_ref[...]], axis=0)
    v1 = jnp.concatenate([v, jnp.ones_like(v)], axis=1)
    r = lax.broadcasted_iota(jnp.int32, (G * BLK, 3 * BLK), 0) % BLK
    c = lax.broadcasted_iota(jnp.int32, (G * BLK, 3 * BLK), 1)
    band = jnp.where(jnp.abs(c - BLK - r) <= WINDOW, 0.0, -jnp.inf)
    kpos = n * rows - BLK + lax.broadcasted_iota(jnp.int32, (1, rows + 2 * BLK), 1)
    inside = jnp.where((kpos >= 0) & (kpos < seq), 0.0, -jnp.inf)
    sk = jnp.concatenate([jnp.full((BLK, BLK), sink_ref[kvh * G + h], F32) for h in range(G)], axis=0)
    nsub = rows // BLK
    scores = []
    for j in range(nsub):
        q = q_ref[j * BLK:(j + 1) * BLK, :]
        q4 = jnp.concatenate([q[:, h * D:(h + 1) * D] for h in range(G)], axis=0)
        scores.append(_dot_nt(q4, k[j * BLK:(j + 3) * BLK]))
    probs, sink_terms = [], []
    for j in range(nsub):
        s = scores[j] + band
        if j == 0 or j == nsub - 1:
            s = s + inside[:, j * BLK:(j + 3) * BLK]
        m = jnp.maximum(jnp.broadcast_to(jnp.max(s, axis=-1, keepdims=True), (G * BLK, BLK)), sk)
        probs.append(jnp.concatenate(
            [jnp.exp(s[:, i * BLK:(i + 1) * BLK] - m) for i in range(3)], axis=1).astype(BF16))
        sink_terms.append(jnp.exp(sk - m))
    for j in range(nsub):
        pv = _dot(probs[j], v1[j * BLK:(j + 3) * BLK])
        o = pv[:, :D] / (pv[:, D:] + sink_terms[j])
        o_ref[j * BLK:(j + 1) * BLK, :] = jnp.concatenate(
            [o[h * BLK:(h + 1) * BLK] for h in range(G)], axis=1).astype(o_ref.dtype)


def _attn(qkv, sink, batch, seq, n_q_cols, n_kv_cols, rows):
    T = qkv.shape[0]
    G, D, BLK = GQA_GROUP, HEAD_DIM, ATTN_BLOCK
    kvh = n_kv_cols // D
    nb = seq // BLK
    nq = seq // rows
    sub = rows // BLK
    k_off = n_q_cols // D
    v_off = (n_q_cols + n_kv_cols) // D
    qspec = pl.BlockSpec((rows, G * D), lambda b, h, n: (b * nq + n, h))
    prev = lambda b, n: b * nb + jnp.maximum(n * sub - 1, 0)
    nxt = lambda b, n: b * nb + jnp.minimum((n + 1) * sub, nb - 1)
    edge = lambda f, off: pl.BlockSpec((BLK, D), lambda b, h, n: (f(b, n), off + h))
    cur = lambda off: pl.BlockSpec((rows, D), lambda b, h, n: (b * nq + n, off + h))
    return pl.pallas_call(
        functools.partial(_attn_kernel, seq=seq),
        grid=(batch, kvh, nq),
        in_specs=[pl.BlockSpec(memory_space=pltpu.SMEM), qspec,
                  edge(prev, k_off), cur(k_off), edge(nxt, k_off), edge(prev, v_off), cur(v_off), edge(nxt, v_off)],
        out_specs=qspec,
        out_shape=jax.ShapeDtypeStruct((T, n_q_cols), BF16),
        compiler_params=_params("parallel", "parallel", "arbitrary"),
        name="swa_sink",
    )(sink, *([qkv] * 7))


def _merge_kernel(oh_ref, oa_ref, wh_ref, wa_ref, ga_ref, gb_ref, o_ref):
    oh, oa = oh_ref[...], oa_ref[...]
    for cols in _col_slabs(o_ref.shape[1]):
        ya = _dot(oh, wh_ref[:, cols])
        yb = _dot(oa, wa_ref[:, cols])
        o_ref[:, cols] = (ga_ref[:, cols].astype(F32) * ya + gb_ref[:, cols].astype(F32) * yb).astype(o_ref.dtype)


def _merge(oh, oa, wh, wa, gates, tm, tn):
    T, KH = oh.shape
    KA = oa.shape[1]
    D = wh.shape[1]
    nj = D // tn
    return pl.pallas_call(
        _merge_kernel,
        grid=(T // tm, nj),
        in_specs=[pl.BlockSpec((tm, KH), lambda i, j: (i, 0)), pl.BlockSpec((tm, KA), lambda i, j: (i, 0)),
                  pl.BlockSpec((KH, tn), lambda i, j: (0, j)), pl.BlockSpec((KA, tn), lambda i, j: (0, j)),
                  pl.BlockSpec((tm, tn), lambda i, j: (i, j)), pl.BlockSpec((tm, tn), lambda i, j: (i, j + nj))],
        out_specs=pl.BlockSpec((tm, tn), lambda i, j: (i, j)),
        out_shape=jax.ShapeDtypeStruct((T, D), BF16),
        compiler_params=_params("parallel", "arbitrary"),
        name="gated_merge",
    )(oh, oa, wh, wa, gates, gates)


def _mlp_kernel(h_ref, wu_ref, wd_ref, o_ref):
    @pl.when(pl.program_id(1) == 0)
    def _():
        o_ref[...] = jnp.zeros_like(o_ref)

    u = jnp.square(jnp.maximum(_dot(h_ref[...], wu_ref[...]), 0.0)).astype(BF16)
    o_ref[...] += _dot(u, wd_ref[...])


def _mlp(h, wu, wd, tm, tf):
    T, D = h.shape
    FF = wu.shape[1]
    return pl.pallas_call(
        _mlp_kernel,
        grid=(T // tm, FF // tf),
        in_specs=[pl.BlockSpec((tm, D), lambda i, j: (i, 0)),
                  pl.BlockSpec((D, tf), lambda i, j: (0, j)),
                  pl.BlockSpec((tf, D), lambda i, j: (j, 0))],
        out_specs=pl.BlockSpec((tm, D), lambda i, j: (i, 0)),
        out_shape=jax.ShapeDtypeStruct((T, D), F32),
        compiler_params=_params("parallel", "arbitrary"),
        name="relu2_mlp",
    )(h, wu, wd)


def _ple_kernel(r_ref, p_ref, wg_ref, wp_ref, g_ref, o_ref, xb_ref, *, tn):
    j = pl.program_id(1)

    @pl.when(j == 0)
    def _():
        xb_ref[...] = r_ref[...].astype(xb_ref.dtype)

    x, p = xb_ref[...], p_ref[...].astype(BF16)
    for cols in _col_slabs(tn):
        width = cols.stop - cols.start
        dst = pl.ds(pl.multiple_of(j * tn + cols.start, width), width)
        o_ref[:, dst] = _dot(p, wp_ref[:, cols]) * _sigmoid(_dot(x, wg_ref[:, cols]))

    @pl.when(j == pl.num_programs(1) - 1)
    def _():
        g = g_ref[...]

        def rows(t, carry):
            rs = pl.ds(pl.multiple_of(t * RESID_ROWS, RESID_ROWS), RESID_ROWS)
            o_ref[rs, :] = r_ref[rs, :] + _rms(o_ref[rs, :], g)
            return carry

        lax.fori_loop(0, o_ref.shape[0] // RESID_ROWS, rows, 0)


def _ple(r, p, wg, wp, g, tm, tn):
    T, D = r.shape
    P = p.shape[1]
    row = lambda i, j: (i, 0)
    return pl.pallas_call(
        functools.partial(_ple_kernel, tn=tn),
        grid=(T // tm, D // tn),
        in_specs=[pl.BlockSpec((tm, D), row), pl.BlockSpec((tm, P), row),
                  pl.BlockSpec((D, tn), lambda i, j: (0, j)), pl.BlockSpec((P, tn), lambda i, j: (0, j)),
                  pl.BlockSpec((1, D), lambda i, j: (0, 0))],
        out_specs=pl.BlockSpec((tm, D), row),
        out_shape=jax.ShapeDtypeStruct((T, D), F32),
        scratch_shapes=[pltpu.VMEM((tm, D), BF16)],
        compiler_params=_params("parallel", "arbitrary"),
        name="ple_gate",
    )(r, p, wg, wp, g)


def _rope_tables(seq):
    half = HEAD_DIM // 2
    inv_freq = ROPE_THETA ** (-jnp.arange(0, HEAD_DIM, 2, dtype=F32) / HEAD_DIM)
    ang = jnp.arange(seq, dtype=F32)[:, None] * inv_freq[None, :]
    cos, sin = jnp.cos(ang), jnp.sin(ang)
    assert cos.shape == (seq, half)
    return jnp.concatenate([cos, cos], axis=1), jnp.concatenate([-sin, sin], axis=1)


def kernel(x, p, norm_mix_pre, norm_mix_post, w_in, lb_fwd, lb_bwd, hgrn_norm, attn_sink, w_hgrn_proj,
           w_attn_proj, w_out, norm_mlp_pre, norm_mlp_post, w_mlp_up, w_mlp_down, w_ple, w_ple_gate, norm_ple):
    B, S, D = x.shape
    T = B * S
    depth = w_in.shape[0]
    HW = w_hgrn_proj.shape[1]
    AW = w_attn_proj.shape[1]
    KVW = AW // GQA_GROUP
    assert S % HGRN_ROWS == 0 and S % ATTN_BLOCK == 0 and hgrn_norm.shape[-1] == HEAD_DIM
    assert w_in.shape[2] == 5 * HW + AW + 2 * KVW + 2 * D

    tm = _pick(S, 1024, 512, 256, 128)
    tm_mlp = _pick(T, 512, 256, 128)
    tr = _pick(T, 256, 128)
    tr_wide = _pick(T, 512, 256, 128)
    cos_t, sin_t = _rope_tables(S)
    vec = lambda a: a.reshape(1, -1)

    xf = x.reshape(T, D)
    for i in range(depth):
        off_aq, off_gate = 0, AW + 2 * KVW

        h = _norm_cast(xf, vec(norm_mix_pre[i]), tr_wide)
        tn = lambda off, n: _pick(math.gcd(off, n), 1024, 512, 256, 128)
        prep, w_rest = _hgrn_prep(h, w_in[i][:, :5 * HW].astype(BF16), jnp.concatenate([lb_fwd, lb_bwd], axis=1),
                                  HW, i, tm, 2 * HEAD_DIM, w_full=w_in[i])
        if w_rest is None:
            w_rest = w_in[i][:, 5 * HW:].astype(BF16)
        mm = functools.partial(_mm, h, w_rest, tm=tm)
        tn_a = tn(off_aq, AW + 2 * KVW)
        nrb = S // tm
        rope_spec = pl.BlockSpec((tm, HEAD_DIM), lambda r, j: (r % nrb, 0))
        qkv = mm(off_aq, AW + 2 * KVW, functools.partial(_epi_qkv, tn=tn_a, n_q=AW, n_k=KVW), BF16, tn=tn_a,
                 extras=(cos_t, sin_t), extra_specs=(rope_spec, rope_spec), name="in_qkv")
        tn_g = tn(off_gate, 2 * D)
        later = [w_out[i], w_mlp_up[i], w_mlp_down[i], w_ple_gate[i], w_hgrn_proj[i], w_attn_proj[i]]
        steps = (T // tm) * (2 * D // tn_g)
        on_side = [a for a in later if _cast_rows_per_step(a, steps) is not None]
        gates, side = mm(off_gate, 2 * D, _epi_sigmoid, BF16, tn=tn_g, casts=on_side, name="in_gates")
        side = iter(side)
        w_o, w_up, w_dn, w_pg, w_hp, w_ap = [
            next(side) if _cast_rows_per_step(a, steps) is not None else a.astype(BF16) for a in later]

        o_h = _hgrn(prep, vec(hgrn_norm[i]), B, S)
        o_a = _attn(qkv, attn_sink[i], B, S, AW, KVW, _pick(S, 1024, 512, 256, 128))
        y = _merge(o_h, o_a, w_hp, w_ap, gates, tm, _pick(D, 1024, 512, 256, 128))
        mix = _mm(y, w_o, 0, D, _epi_identity, F32, tm, _pick(D, 1024, 512, 256, 128), name="w_out")
        xf, h2 = _resid_norm(xf, mix, vec(norm_mix_post[i]), vec(norm_mlp_pre[i]), tr)

        d = _mlp(h2, w_up, w_dn, tm_mlp, _pick(w_mlp_up.shape[2], 1024, 512, 256))
        xf = _resid(xf, d, vec(norm_mlp_post[i]), tr_wide)

        xf = _ple(xf, p[i].reshape(T, -1), w_pg, w_ple[i].astype(BF16), vec(norm_ple[i]),
                  tm_mlp, _pick(D, 1024, 512, 256, 128))
    return xf.reshape(B, S, D)
```

```python
import functools
import math

import jax
import jax.numpy as jnp
from jax import lax
from jax.experimental import pallas as pl
from jax.experimental.pallas import tpu as pltpu

F32 = jnp.float32
BF16 = jnp.bfloat16

EPS = 1e-6
HEAD_DIM = 128
GQA_GROUP = 4
WINDOW = 128
ATTN_BLOCK = 128
CHUNK = 64
HGRN_ROWS = 256
HGRN_UNROLL = 16
ROPE_THETA = 10000.0
MXU_COLS = 256
SUBLANES = 8
BF16_ROWS = 16
RESID_ROWS = 64
VMEM_LIMIT_BYTES = 62 * 1024 * 1024


def _params(*semantics):
    return pltpu.CompilerParams(dimension_semantics=semantics, vmem_limit_bytes=VMEM_LIMIT_BYTES)


def _pick(n, *cands):
    for c in cands:
        if n % c == 0:
            return c
    raise ValueError(f"no tile in {cands} divides {n}")


def _dot(a, b):
    return jnp.dot(a, b, preferred_element_type=F32)


def _dot_nt(a, b):
    return lax.dot_general(a, b, (((1,), (1,)), ((), ())), preferred_element_type=F32)


def _dot_tn(a, b):
    return lax.dot_general(a, b, (((0,), (0,)), ((), ())), preferred_element_type=F32)


def _sigmoid(x):
    return 1.0 / (1.0 + jnp.exp(-x))


def _rms(x, w):
    return x * lax.rsqrt(jnp.mean(x * x, axis=-1, keepdims=True) + EPS) * w


def _norm_cast_kernel(x_ref, w_ref, o_ref):
    o_ref[...] = _rms(x_ref[...], w_ref[...]).astype(o_ref.dtype)


def _norm_cast(x, w, tm):
    T, D = x.shape
    return pl.pallas_call(
        _norm_cast_kernel,
        grid=(T // tm,),
        in_specs=[pl.BlockSpec((tm, D), lambda i: (i, 0)), pl.BlockSpec((1, D), lambda i: (0, 0))],
        out_specs=pl.BlockSpec((tm, D), lambda i: (i, 0)),
        out_shape=jax.ShapeDtypeStruct((T, D), BF16),
        compiler_params=_params("parallel"),
        name="norm_cast",
    )(x, w)


def _resid_norm_kernel(x_ref, d_ref, w_ref, w2_ref, o_ref, h_ref):
    y = x_ref[...] + _rms(d_ref[...], w_ref[...])
    o_ref[...] = y
    h_ref[...] = _rms(y, w2_ref[...]).astype(h_ref.dtype)


def _resid_norm(x, d, w, w2, tm):
    T, D = x.shape
    row = pl.BlockSpec((tm, D), lambda i: (i, 0))
    vec = pl.BlockSpec((1, D), lambda i: (0, 0))
    return pl.pallas_call(
        _resid_norm_kernel,
        grid=(T // tm,),
        in_specs=[row, row, vec, vec],
        out_specs=[row, row],
        out_shape=[jax.ShapeDtypeStruct((T, D), F32), jax.ShapeDtypeStruct((T, D), BF16)],
        compiler_params=_params("parallel"),
        name="resid_norm",
    )(x, d, w, w2)


def _resid_kernel(x_ref, d_ref, w_ref, o_ref):
    o_ref[...] = x_ref[...] + _rms(d_ref[...], w_ref[...])


def _resid(x, d, w, tm):
    T, D = x.shape
    row = pl.BlockSpec((tm, D), lambda i: (i, 0))
    vec = pl.BlockSpec((1, D), lambda i: (0, 0))
    return pl.pallas_call(
        _resid_kernel,
        grid=(T // tm,),
        in_specs=[row, row, vec],
        out_specs=row,
        out_shape=jax.ShapeDtypeStruct((T, D), F32),
        compiler_params=_params("parallel"),
        name="resid",
    )(x, d, w)


def _col_slabs(width):
    step = min(width, MXU_COLS)
    return [slice(c, c + step) for c in range(0, width, step)]


def _mm_kernel(lhs_ref, w_ref, *rest, epilogue, n_casts):
    n_extra = len(rest) - 2 * n_casts - 1
    extra_refs, cast_in = rest[:n_extra], rest[n_extra:n_extra + n_casts]
    o_ref, cast_out = rest[n_extra + n_casts], rest[n_extra + n_casts + 1:]
    lhs = lhs_ref[...]
    for cols in _col_slabs(o_ref.shape[1]):
        o_ref[:, cols] = epilogue(_dot(lhs, w_ref[:, cols]), cols.start, *extra_refs).astype(o_ref.dtype)
    for src, dst in zip(cast_in, cast_out):
        dst[...] = src[...].astype(dst.dtype)


def _cast_rows_per_step(a, steps):
    rows = a.shape[0] // steps
    return rows if rows * steps == a.shape[0] and rows % BF16_ROWS == 0 else None


def _mm(lhs, w, col_off, n_cols, epilogue, out_dtype, tm, tn, extras=(), extra_specs=(), casts=None, name="mm"):
    with_casts, casts = casts is not None, list(casts or ())
    T, K = lhs.shape
    off = col_off // tn
    nj = n_cols // tn
    steps = (T // tm) * nj
    cast_specs = [pl.BlockSpec((_cast_rows_per_step(a, steps), a.shape[1]), lambda i, j: (i * nj + j, 0))
                  for a in casts]
    res = pl.pallas_call(
        functools.partial(_mm_kernel, epilogue=epilogue, n_casts=len(casts)),
        grid=(T // tm, nj),
        in_specs=[pl.BlockSpec((tm, K), lambda i, j: (i, 0)),
                  pl.BlockSpec((K, tn), lambda i, j: (0, j + off))] + list(extra_specs) + cast_specs,
        out_specs=[pl.BlockSpec((tm, tn), lambda i, j: (i, j))] + cast_specs,
        out_shape=[jax.ShapeDtypeStruct((T, n_cols), out_dtype)]
        + [jax.ShapeDtypeStruct(a.shape, BF16) for a in casts],
        compiler_params=_params("parallel", "arbitrary"),
        name=name,
    )(lhs, w, *extras, *casts)
    return (res[0], list(res[1:])) if with_casts else res[0]


def _epi_sigmoid(acc, col0):
    return _sigmoid(acc)


def _epi_identity(acc, col0):
    return acc


def _epi_qkv(acc, col0, cos_ref, sin_ref, *, tn, n_q, n_k):
    cos, sin = cos_ref[...], sin_ref[...]
    heads = []
    for h in range(acc.shape[1] // HEAD_DIM):
        col = pl.program_id(1) * tn + col0 + h * HEAD_DIM
        xh = acc[:, h * HEAD_DIM:(h + 1) * HEAD_DIM]
        rot = (xh * cos + pltpu.roll(xh, HEAD_DIM // 2, 1) * sin) * jnp.where(col < n_q, HEAD_DIM ** -0.5, 1.0)
        heads.append(jnp.where(col < n_q + n_k, rot, xh))
    return jnp.concatenate(heads, axis=1)


def _chunk_cumprod(x, reverse):
    n = SUBLANES
    rid = lax.broadcasted_iota(jnp.int32, (n, x.shape[1]), 0)
    slabs = []
    for g in range(x.shape[0] // n):
        y = x[g * n:(g + 1) * n]
        s = 1
        while s < n:
            if reverse:
                y = y * jnp.where(rid < n - s, pltpu.roll(y, n - s, 0), 1.0)
            else:
                y = y * jnp.where(rid >= s, pltpu.roll(y, s, 0), 1.0)
            s *= 2
        slabs.append(y)
    order = range(len(slabs) - 1, -1, -1) if reverse else range(len(slabs))
    total = None
    for g in order:
        if total is not None:
            slabs[g] = slabs[g] * total
        total = slabs[g][0:1] if reverse else slabs[g][n - 1:n]
    return jnp.concatenate(slabs, axis=0)


def _hgrn_prep_kernel(h_ref, wq_ref, wff_ref, wfb_ref, wv_ref, wg_ref, lbf_ref, lbb_ref, *rest, layer, n_pieces):
    pieces = rest[:n_pieces]
    qf_ref, kf_ref, ktf_ref, qb_ref, kb_ref, ktb_ref, v_ref, og_ref, df_ref, db_ref = rest[n_pieces:n_pieces + 10]
    if n_pieces:
        slab = rest[-1]
        cw = pieces[0].shape[1]
        for p, src in enumerate(pieces):
            slab[:, p * cw:(p + 1) * cw] = src[...].astype(slab.dtype)
    h = h_ref[...]
    rows, width = qf_ref.shape
    xf = [_dot(h, wff_ref[...]), _dot(h, wfb_ref[...])]
    xq = _dot(h, wq_ref[...])
    xg = _dot(h, wg_ref[...])
    og_ref[...] = (xg * _sigmoid(xg)).astype(og_ref.dtype)
    v_ref[...] = _dot(h, wv_ref[...]).astype(v_ref.dtype)
    lbs = []
    for lb_ref in (lbf_ref, lbb_ref):
        lbp = lb_ref[...]
        e = jnp.exp(lbp - jnp.max(lbp, axis=0, keepdims=True))
        lbs.append(jnp.sum(e[:layer + 1], axis=0, keepdims=True) / jnp.sum(e, axis=0, keepdims=True))
    for c in range(rows // CHUNK):
        rs = slice(c * CHUNK, (c + 1) * CHUNK)
        q = xq[rs] * _sigmoid(xq[rs]) * (HEAD_DIM ** -0.5)
        for x, lb, reverse, qo, ko, kto, do in (
                (xf[0], lbs[0], False, qf_ref, kf_ref, ktf_ref, df_ref),
                (xf[1], lbs[1], True, qb_ref, kb_ref, ktb_ref, db_ref)):
            f = lb + (1.0 - lb) * _sigmoid(x[rs])
            eb = _chunk_cumprod(f, reverse)
            decay = eb[0:1, :] if reverse else eb[CHUNK - 1:CHUNK, :]
            k_hat = (1.0 - f) / eb
            qo[rs, :] = (q * eb).astype(qo.dtype)
            ko[rs, :] = k_hat.astype(ko.dtype)
            kto[rs, :] = (k_hat * decay).astype(kto.dtype)
            do[c:c + 1, :] = decay


def _hgrn_prep(h, w, lbp, hw, layer, tm, tn, w_full=None):
    T, K = h.shape
    nseg = hw // tn
    steps = (T // tm) * nseg
    wspec = lambda seg: pl.BlockSpec((K, tn), lambda i, j: (0, seg * nseg + j))
    lbspec = lambda seg: pl.BlockSpec((lbp.shape[0], tn), lambda i, j: (0, seg * nseg + j))
    ospec = pl.BlockSpec((tm, tn), lambda i, j: (i, j))
    dspec = pl.BlockSpec((tm // CHUNK, tn), lambda i, j: (i, j))
    act = jax.ShapeDtypeStruct((T, hw), BF16)
    dec = jax.ShapeDtypeStruct((T // CHUNK, hw), F32)
    side_in, side_specs, side_out_spec, side_out_shape = [], [], [], []
    rows = None if w_full is None else _cast_rows_per_step(w_full, steps)
    if rows is not None:
        first, rest_cols = 5 * hw, w_full.shape[1] - 5 * hw
        cw = math.gcd(first, rest_cols)
        if cw % HEAD_DIM == 0:
            for p in range(rest_cols // cw):
                side_in.append(w_full)
                side_specs.append(pl.BlockSpec((rows, cw), lambda i, j, p=p: (i * nseg + j, first // cw + p)))
            side_out_spec = [pl.BlockSpec((rows, rest_cols), lambda i, j: (i * nseg + j, 0))]
            side_out_shape = [jax.ShapeDtypeStruct((K, rest_cols), BF16)]
    res = pl.pallas_call(
        functools.partial(_hgrn_prep_kernel, layer=layer, n_pieces=len(side_in)),
        grid=(T // tm, nseg),
        in_specs=[pl.BlockSpec((tm, K), lambda i, j: (i, 0)), wspec(0), wspec(1), wspec(2), wspec(3), wspec(4),
                  lbspec(0), lbspec(1)] + side_specs,
        out_specs=[ospec] * 8 + [dspec] * 2 + side_out_spec,
        out_shape=[act] * 8 + [dec] * 2 + side_out_shape,
        compiler_params=_params("parallel", "arbitrary"),
        name="in_hgrn",
    )(h, w, w, w, w, w, lbp, lbp, *side_in)
    return res[:10], (res[10] if side_in else None)


def _hgrn_kernel(qf_ref, kf_ref, ktf_ref, qb_ref, kb_ref, ktb_ref, v_ref, og_ref, df_ref, db_ref, nw_ref, o_ref,
                 accf_ref, accb_ref, mf_ref, mb_ref, sf_ref, sb_ref, *, seq):
    R = HGRN_ROWS
    nsb = seq // R
    ncs = R // CHUNK

    row = lax.broadcasted_iota(jnp.int32, (R, R), 0)
    col = lax.broadcasted_iota(jnp.int32, (R, R), 1)
    same = (row // CHUNK) == (col // CHUNK)
    mf_ref[...] = jnp.where(same & (col <= row), 1.0, 0.0).astype(BF16)
    mb_ref[...] = jnp.where(same & (col >= row), 1.0, 0.0).astype(BF16)
    sf_ref[...] = jnp.zeros_like(sf_ref)
    sb_ref[...] = jnp.zeros_like(sb_ref)

    U = HGRN_UNROLL if nsb % HGRN_UNROLL == 0 else 1
    W = U * R
    nw = U * ncs
    blocks = [slice(u * R, (u + 1) * R) for u in range(U)]
    chunks = [slice(c * CHUNK, (c + 1) * CHUNK) for c in range(nw)]

    nt = nsb // U

    def finish_rows(r0, n):
        o = accf_ref[pl.ds(r0, n), :] + accb_ref[pl.ds(r0, n), :]
        y = _rms(o, nw_ref[...]) * og_ref[pl.ds(r0, n), :].astype(F32)
        o_ref[pl.ds(r0, n), :] = y.astype(o_ref.dtype)

    def body(t, carry, finalize):
        dirs = ((t, qf_ref, kf_ref, ktf_ref, df_ref, mf_ref, sf_ref, accf_ref, range(nw)),
                (nt - 1 - t, qb_ref, kb_ref, ktb_ref, db_ref, mb_ref, sb_ref, accb_ref, range(nw - 1, -1, -1)))
        r0s = [pl.multiple_of(d[0] * W, W) for d in dirs]
        qs = [d[1][pl.ds(r0, W), :] for d, r0 in zip(dirs, r0s)]
        vs = [v_ref[pl.ds(r0, W), :] for r0 in r0s]
        scores = []
        for d, r0, q in zip(dirs, r0s, qs):
            k = d[2][pl.ds(r0, W), :]
            scores.append([_dot_nt(q[bl], k[bl]) for bl in blocks])
        updates = []
        for d, r0, v in zip(dirs, r0s, vs):
            kt = d[3][pl.ds(r0, W), :]
            updates.append([_dot_tn(v[sl], kt[sl]) for sl in chunks])
        entering = []
        for d, upd in zip(dirs, updates):
            idx, d_ref, s_ref = d[0], d[4], d[6]
            st = s_ref[...]
            ent = [None] * nw
            for c in d[8]:
                ent[c] = st.astype(BF16)
                st = st * d_ref[pl.ds(idx * nw + c, 1), :] + upd[c]
            s_ref[...] = st
            entering.append(ent)
        inter = [[_dot_nt(q[sl], ent[c]) for c, sl in enumerate(chunks)] for q, ent in zip(qs, entering)]
        for d, r0, sc, v, o_inter in zip(dirs, r0s, scores, vs, inter):
            mask = d[5][...] > 0
            for u, bl in enumerate(blocks):
                o = _dot(jnp.where(mask, sc[u], 0.0).astype(BF16), v[bl])
                d[7][pl.ds(r0 + u * R, R), :] = o + jnp.concatenate(o_inter[u * ncs:(u + 1) * ncs], axis=0)
        if finalize:
            for r0 in r0s:
                finish_rows(r0, W)
        return carry

    if nt % 2 == 0:
        lax.fori_loop(0, nt // 2, functools.partial(body, finalize=False), 0)
        lax.fori_loop(nt // 2, nt, functools.partial(body, finalize=True), 0)
    else:
        lax.fori_loop(0, nt, functools.partial(body, finalize=False), 0)

        def finish(t, carry):
            finish_rows(pl.multiple_of(t * W, W), W)
            return carry

        lax.fori_loop(0, nt, finish, 0)


def _hgrn(prep, nw, batch, seq):
    *acts, df, db = prep
    T, HW = acts[0].shape
    H = HW // HEAD_DIM
    blk = pl.BlockSpec((seq, HEAD_DIM), lambda b, h: (b, h))
    dblk = pl.BlockSpec((seq // CHUNK, HEAD_DIM), lambda b, h: (b, h))
    return pl.pallas_call(
        functools.partial(_hgrn_kernel, seq=seq),
        grid=(batch, H),
        in_specs=[blk] * 8 + [dblk] * 2 + [pl.BlockSpec((1, HEAD_DIM), lambda b, h: (0, 0))],
        out_specs=blk,
        out_shape=jax.ShapeDtypeStruct((T, HW), BF16),
        scratch_shapes=[pltpu.VMEM((seq, HEAD_DIM), F32), pltpu.VMEM((seq, HEAD_DIM), F32),
                        pltpu.VMEM((HGRN_ROWS, HGRN_ROWS), BF16), pltpu.VMEM((HGRN_ROWS, HGRN_ROWS), BF16),
                        pltpu.VMEM((HEAD_DIM, HEAD_DIM), F32), pltpu.VMEM((HEAD_DIM, HEAD_DIM), F32)],
        compiler_params=_params("parallel", "parallel"),
        name="hgrn2",
    )(*acts, df, db, nw)


def _attn_kernel(sink_ref, q_ref, kp_ref, kc_ref, kn_ref, vp_ref, vc_ref, vn_ref, o_ref, *, seq):
    G, D, BLK = GQA_GROUP, HEAD_DIM, ATTN_BLOCK
    rows = q_ref.shape[0]
    kvh = pl.program_id(1)
    n = pl.program_id(2)
    k = jnp.concatenate([kp_ref[...], kc_ref[...], kn_ref[...]], axis=0)
    v = jnp.concatenate([vp_ref[...], vc_ref[...], vn_ref[...]], axis=0)
    v1 = jnp.concatenate([v, jnp.ones_like(v)], axis=1)
    r = lax.broadcasted_iota(jnp.int32, (G * BLK, 3 * BLK), 0) % BLK
    c = lax.broadcasted_iota(jnp.int32, (G * BLK, 3 * BLK), 1)
    band = jnp.where(jnp.abs(c - BLK - r) <= WINDOW, 0.0, -jnp.inf)
    kpos = n * rows - BLK + lax.broadcasted_iota(jnp.int32, (1, rows + 2 * BLK), 1)
    inside = jnp.where((kpos >= 0) & (kpos < seq), 0.0, -jnp.inf)
    sk = jnp.concatenate([jnp.full((BLK, BLK), sink_ref[kvh * G + h], F32) for h in range(G)], axis=0)
    nsub = rows // BLK
    scores = []
    for j in range(nsub):
        q = q_ref[j * BLK:(j + 1) * BLK, :]
        q4 = jnp.concatenate([q[:, h * D:(h + 1) * D] for h in range(G)], axis=0)
        scores.append(_dot_nt(q4, k[j * BLK:(j + 3) * BLK]))
    probs, sink_terms = [], []
    for j in range(nsub):
        s = scores[j] + band
        if j == 0 or j == nsub - 1:
            s = s + inside[:, j * BLK:(j + 3) * BLK]
        m = jnp.maximum(jnp.broadcast_to(jnp.max(s, axis=-1, keepdims=True), (G * BLK, BLK)), sk)
        probs.append(jnp.concatenate(
            [jnp.exp(s[:, i * BLK:(i + 1) * BLK] - m) for i in range(3)], axis=1).astype(BF16))
        sink_terms.append(jnp.exp(sk - m))
    for j in range(nsub):
        pv = _dot(probs[j], v1[j * BLK:(j + 3) * BLK])
        o = pv[:, :D] / (pv[:, D:] + sink_terms[j])
        o_ref[j * BLK:(j + 1) * BLK, :] = jnp.concatenate(
            [o[h * BLK:(h + 1) * BLK] for h in range(G)], axis=1).astype(o_ref.dtype)


def _attn(qkv, sink, batch, seq, n_q_cols, n_kv_cols, rows):
    T = qkv.shape[0]
    G, D, BLK = GQA_GROUP, HEAD_DIM, ATTN_BLOCK
    kvh = n_kv_cols // D
    nb = seq // BLK
    nq = seq // rows
    sub = rows // BLK
    k_off = n_q_cols // D
    v_off = (n_q_cols + n_kv_cols) // D
    qspec = pl.BlockSpec((rows, G * D), lambda b, h, n: (b * nq + n, h))
    prev = lambda b, n: b * nb + jnp.maximum(n * sub - 1, 0)
    nxt = lambda b, n: b * nb + jnp.minimum((n + 1) * sub, nb - 1)
    edge = lambda f, off: pl.BlockSpec((BLK, D), lambda b, h, n: (f(b, n), off + h))
    cur = lambda off: pl.BlockSpec((rows, D), lambda b, h, n: (b * nq + n, off + h))
    return pl.pallas_call(
        functools.partial(_attn_kernel, seq=seq),
        grid=(batch, kvh, nq),
        in_specs=[pl.BlockSpec(memory_space=pltpu.SMEM), qspec,
                  edge(prev, k_off), cur(k_off), edge(nxt, k_off), edge(prev, v_off), cur(v_off), edge(nxt, v_off)],
        out_specs=qspec,
        out_shape=jax.ShapeDtypeStruct((T, n_q_cols), BF16),
        compiler_params=_params("parallel", "parallel", "arbitrary"),
        name="swa_sink",
    )(sink, *([qkv] * 7))


def _merge_kernel(oh_ref, oa_ref, wh_ref, wa_ref, ga_ref, gb_ref, o_ref):
    oh, oa = oh_ref[...], oa_ref[...]
    for cols in _col_slabs(o_ref.shape[1]):
        ya = _dot(oh, wh_ref[:, cols])
        yb = _dot(oa, wa_ref[:, cols])
        o_ref[:, cols] = (ga_ref[:, cols].astype(F32) * ya + gb_ref[:, cols].astype(F32) * yb).astype(o_ref.dtype)


def _merge(oh, oa, wh, wa, gates, tm, tn):
    T, KH = oh.shape
    KA = oa.shape[1]
    D = wh.shape[1]
    nj = D // tn
    return pl.pallas_call(
        _merge_kernel,
        grid=(T // tm, nj),
        in_specs=[pl.BlockSpec((tm, KH), lambda i, j: (i, 0)), pl.BlockSpec((tm, KA), lambda i, j: (i, 0)),
                  pl.BlockSpec((KH, tn), lambda i, j: (0, j)), pl.BlockSpec((KA, tn), lambda i, j: (0, j)),
                  pl.BlockSpec((tm, tn), lambda i, j: (i, j)), pl.BlockSpec((tm, tn), lambda i, j: (i, j + nj))],
        out_specs=pl.BlockSpec((tm, tn), lambda i, j: (i, j)),
        out_shape=jax.ShapeDtypeStruct((T, D), BF16),
        compiler_params=_params("parallel", "arbitrary"),
        name="gated_merge",
    )(oh, oa, wh, wa, gates, gates)


def _mlp_kernel(h_ref, wu_ref, wd_ref, o_ref):
    @pl.when(pl.program_id(1) == 0)
    def _():
        o_ref[...] = jnp.zeros_like(o_ref)

    u = jnp.square(jnp.maximum(_dot(h_ref[...], wu_ref[...]), 0.0)).astype(BF16)
    o_ref[...] += _dot(u, wd_ref[...])


def _mlp(h, wu, wd, tm, tf):
    T, D = h.shape
    FF = wu.shape[1]
    return pl.pallas_call(
        _mlp_kernel,
        grid=(T // tm, FF // tf),
        in_specs=[pl.BlockSpec((tm, D), lambda i, j: (i, 0)),
                  pl.BlockSpec((D, tf), lambda i, j: (0, j)),
                  pl.BlockSpec((tf, D), lambda i, j: (j, 0))],
        out_specs=pl.BlockSpec((tm, D), lambda i, j: (i, 0)),
        out_shape=jax.ShapeDtypeStruct((T, D), F32),
        compiler_params=_params("parallel", "arbitrary"),
        name="relu2_mlp",
    )(h, wu, wd)


def _ple_kernel(r_ref, p_ref, wg_ref, wp_ref, g_ref, o_ref, xb_ref, *, tn):
    j = pl.program_id(1)

    @pl.when(j == 0)
    def _():
        xb_ref[...] = r_ref[...].astype(xb_ref.dtype)

    x, p = xb_ref[...], p_ref[...].astype(BF16)
    for cols in _col_slabs(tn):
        width = cols.stop - cols.start
        dst = pl.ds(pl.multiple_of(j * tn + cols.start, width), width)
        o_ref[:, dst] = _dot(p, wp_ref[:, cols]) * _sigmoid(_dot(x, wg_ref[:, cols]))

    @pl.when(j == pl.num_programs(1) - 1)
    def _():
        g = g_ref[...]

        def rows(t, carry):
            rs = pl.ds(pl.multiple_of(t * RESID_ROWS, RESID_ROWS), RESID_ROWS)
            o_ref[rs, :] = r_ref[rs, :] + _rms(o_ref[rs, :], g)
            return carry

        lax.fori_loop(0, o_ref.shape[0] // RESID_ROWS, rows, 0)


def _ple(r, p, wg, wp, g, tm, tn):
    T, D = r.shape
    P = p.shape[1]
    row = lambda i, j: (i, 0)
    return pl.pallas_call(
        functools.partial(_ple_kernel, tn=tn),
        grid=(T // tm, D // tn),
        in_specs=[pl.BlockSpec((tm, D), row), pl.BlockSpec((tm, P), row),
                  pl.BlockSpec((D, tn), lambda i, j: (0, j)), pl.BlockSpec((P, tn), lambda i, j: (0, j)),
                  pl.BlockSpec((1, D), lambda i, j: (0, 0))],
        out_specs=pl.BlockSpec((tm, D), row),
        out_shape=jax.ShapeDtypeStruct((T, D), F32),
        scratch_shapes=[pltpu.VMEM((tm, D), BF16)],
        compiler_params=_params("parallel", "arbitrary"),
        name="ple_gate",
    )(r, p, wg, wp, g)


def _rope_tables(seq):
    half = HEAD_DIM // 2
    inv_freq = ROPE_THETA ** (-jnp.arange(0, HEAD_DIM, 2, dtype=F32) / HEAD_DIM)
    ang = jnp.arange(seq, dtype=F32)[:, None] * inv_freq[None, :]
    cos, sin = jnp.cos(ang), jnp.sin(ang)
    assert cos.shape == (seq, half)
    return jnp.concatenate([cos, cos], axis=1), jnp.concatenate([-sin, sin], axis=1)


def kernel(x, p, norm_mix_pre, norm_mix_post, w_in, lb_fwd, lb_bwd, hgrn_norm, attn_sink, w_hgrn_proj,
           w_attn_proj, w_out, norm_mlp_pre, norm_mlp_post, w_mlp_up, w_mlp_down, w_ple, w_ple_gate, norm_ple):
    B, S, D = x.shape
    T = B * S
    depth = w_in.shape[0]
    HW = w_hgrn_proj.shape[1]
    AW = w_attn_proj.shape[1]
    KVW = AW // GQA_GROUP
    assert S % HGRN_ROWS == 0 and S % ATTN_BLOCK == 0 and hgrn_norm.shape[-1] == HEAD_DIM
    assert w_in.shape[2] == 5 * HW + AW + 2 * KVW + 2 * D

    tm = _pick(S, 1024, 512, 256, 128)
    tm_mlp = _pick(T, 512, 256, 128)
    tr = _pick(T, 256, 128)
    tr_wide = _pick(T, 512, 256, 128)
    cos_t, sin_t = _rope_tables(S)
    vec = lambda a: a.reshape(1, -1)

    xf = x.reshape(T, D)
    for i in range(depth):
        off_aq, off_gate = 0, AW + 2 * KVW

        h = _norm_cast(xf, vec(norm_mix_pre[i]), tr_wide)
        tn = lambda off, n: _pick(math.gcd(off, n), 1024, 512, 256, 128)
        prep, w_rest = _hgrn_prep(h, w_in[i][:, :5 * HW].astype(BF16), jnp.concatenate([lb_fwd, lb_bwd], axis=1),
                                  HW, i, tm, 2 * HEAD_DIM, w_full=w_in[i])
        if w_rest is None:
            w_rest = w_in[i][:, 5 * HW:].astype(BF16)
        mm = functools.partial(_mm, h, w_rest, tm=tm)
        tn_a = tn(off_aq, AW + 2 * KVW)
        nrb = S // tm
        rope_spec = pl.BlockSpec((tm, HEAD_DIM), lambda r, j: (r % nrb, 0))
        qkv = mm(off_aq, AW + 2 * KVW, functools.partial(_epi_qkv, tn=tn_a, n_q=AW, n_k=KVW), BF16, tn=tn_a,
                 extras=(cos_t, sin_t), extra_specs=(rope_spec, rope_spec), name="in_qkv")
        tn_g = tn(off_gate, 2 * D)
        later = [w_out[i], w_mlp_up[i], w_mlp_down[i], w_ple_gate[i], w_hgrn_proj[i], w_attn_proj[i]]
        steps = (T // tm) * (2 * D // tn_g)
        on_side = [a for a in later if _cast_rows_per_step(a, steps) is not None]
        gates, side = mm(off_gate, 2 * D, _epi_sigmoid, BF16, tn=tn_g, casts=on_side, name="in_gates")
        side = iter(side)
        w_o, w_up, w_dn, w_pg, w_hp, w_ap = [
            next(side) if _cast_rows_per_step(a, steps) is not None else a.astype(BF16) for a in later]

        o_h = _hgrn(prep, vec(hgrn_norm[i]), B, S)
        o_a = _attn(qkv, attn_sink[i], B, S, AW, KVW, _pick(S, 1024, 512, 256, 128))
        y = _merge(o_h, o_a, w_hp, w_ap, gates, tm, _pick(D, 1024, 512, 256, 128))
        mix = _mm(y, w_o, 0, D, _epi_identity, F32, tm, _pick(D, 1024, 512, 256, 128), name="w_out")
        xf, h2 = _resid_norm(xf, mix, vec(norm_mix_post[i]), vec(norm_mlp_pre[i]), tr)

        d = _mlp(h2, w_up, w_dn, tm_mlp, _pick(w_mlp_up.shape[2], 1024, 512, 256))
        xf = _resid(xf, d, vec(norm_mlp_post[i]), tr_wide)

        xf = _ple(xf, p[i].reshape(T, -1), w_pg, w_ple[i].astype(BF16), vec(norm_ple[i]),
                  tm_mlp, _pick(D, 1024, 512, 256, 128))
    return xf.reshape(B, S, D)
```

```python
import functools
import math

import jax
import jax.numpy as jnp
from jax import lax
from jax.experimental import pallas as pl
from jax.experimental.pallas import tpu as pltpu

F32 = jnp.float32
BF16 = jnp.bfloat16

EPS = 1e-6
HEAD_DIM = 128
GQA_GROUP = 4
WINDOW = 128
ATTN_BLOCK = 128
CHUNK = 64
HGRN_ROWS = 256
HGRN_UNROLL = 16
ROPE_THETA = 10000.0
MXU_COLS = 256
SUBLANES = 8
BF16_ROWS = 16
RESID_ROWS = 64
VMEM_LIMIT_BYTES = 62 * 1024 * 1024


def _params(*semantics):
    return pltpu.CompilerParams(dimension_semantics=semantics, vmem_limit_bytes=VMEM_LIMIT_BYTES)


def _pick(n, *cands):
    for c in cands:
        if n % c == 0:
            return c
    raise ValueError(f"no tile in {cands} divides {n}")


def _dot(a, b):
    return jnp.dot(a, b, preferred_element_type=F32)


def _dot_nt(a, b):
    return lax.dot_general(a, b, (((1,), (1,)), ((), ())), preferred_element_type=F32)


def _dot_tn(a, b):
    return lax.dot_general(a, b, (((0,), (0,)), ((), ())), preferred_element_type=F32)


def _sigmoid(x):
    return 1.0 / (1.0 + jnp.exp(-x))


def _rms(x, w):
    return x * lax.rsqrt(jnp.mean(x * x, axis=-1, keepdims=True) + EPS) * w


def _norm_cast_kernel(x_ref, w_ref, o_ref):
    o_ref[...] = _rms(x_ref[...], w_ref[...]).astype(o_ref.dtype)


def _norm_cast(x, w, tm):
    T, D = x.shape
    return pl.pallas_call(
        _norm_cast_kernel,
        grid=(T // tm,),
        in_specs=[pl.BlockSpec((tm, D), lambda i: (i, 0)), pl.BlockSpec((1, D), lambda i: (0, 0))],
        out_specs=pl.BlockSpec((tm, D), lambda i: (i, 0)),
        out_shape=jax.ShapeDtypeStruct((T, D), BF16),
        compiler_params=_params("parallel"),
        name="norm_cast",
    )(x, w)


def _resid_norm_kernel(x_ref, d_ref, w_ref, w2_ref, o_ref, h_ref):
    y = x_ref[...] + _rms(d_ref[...], w_ref[...])
    o_ref[...] = y
    h_ref[...] = _rms(y, w2_ref[...]).astype(h_ref.dtype)


def _resid_norm(x, d, w, w2, tm):
    T, D = x.shape
    row = pl.BlockSpec((tm, D), lambda i: (i, 0))
    vec = pl.BlockSpec((1, D), lambda i: (0, 0))
    return pl.pallas_call(
        _resid_norm_kernel,
        grid=(T // tm,),
        in_specs=[row, row, vec, vec],
        out_specs=[row, row],
        out_shape=[jax.ShapeDtypeStruct((T, D), F32), jax.ShapeDtypeStruct((T, D), BF16)],
        compiler_params=_params("parallel"),
        name="resid_norm",
    )(x, d, w, w2)


def _resid_kernel(x_ref, d_ref, w_ref, o_ref):
    o_ref[...] = x_ref[...] + _rms(d_ref[...], w_ref[...])


def _resid(x, d, w, tm):
    T, D = x.shape
    row = pl.BlockSpec((tm, D), lambda i: (i, 0))
    vec = pl.BlockSpec((1, D), lambda i: (0, 0))
    return pl.pallas_call(
        _resid_kernel,
        grid=(T // tm,),
        in_specs=[row, row, vec],
        out_specs=row,
        out_shape=jax.ShapeDtypeStruct((T, D), F32),
        compiler_params=_params("parallel"),
        name="resid",
    )(x, d, w)


def _col_slabs(width):
    step = min(width, MXU_COLS)
    return [slice(c, c + step) for c in range(0, width, step)]


def _mm_kernel(lhs_ref, w_ref, *rest, epilogue, n_casts):
    n_extra = len(rest) - 2 * n_casts - 1
    extra_refs, cast_in = rest[:n_extra], rest[n_extra:n_extra + n_casts]
    o_ref, cast_out = rest[n_extra + n_casts], rest[n_extra + n_casts + 1:]
    lhs = lhs_ref[...]
    for cols in _col_slabs(o_ref.shape[1]):
        o_ref[:, cols] = epilogue(_dot(lhs, w_ref[:, cols]), cols.start, *extra_refs).astype(o_ref.dtype)
    for src, dst in zip(cast_in, cast_out):
        dst[...] = src[...].astype(dst.dtype)


def _cast_rows_per_step(a, steps):
    rows = a.shape[0] // steps
    return rows if rows * steps == a.shape[0] and rows % BF16_ROWS == 0 else None


def _mm(lhs, w, col_off, n_cols, epilogue, out_dtype, tm, tn, extras=(), extra_specs=(), casts=None, name="mm"):
    with_casts, casts = casts is not None, list(casts or ())
    T, K = lhs.shape
    off = col_off // tn
    nj = n_cols // tn
    steps = (T // tm) * nj
    cast_specs = [pl.BlockSpec((_cast_rows_per_step(a, steps), a.shape[1]), lambda i, j: (i * nj + j, 0))
                  for a in casts]
    res = pl.pallas_call(
        functools.partial(_mm_kernel, epilogue=epilogue, n_casts=len(casts)),
        grid=(T // tm, nj),
        in_specs=[pl.BlockSpec((tm, K), lambda i, j: (i, 0)),
                  pl.BlockSpec((K, tn), lambda i, j: (0, j + off))] + list(extra_specs) + cast_specs,
        out_specs=[pl.BlockSpec((tm, tn), lambda i, j: (i, j))] + cast_specs,
        out_shape=[jax.ShapeDtypeStruct((T, n_cols), out_dtype)]
        + [jax.ShapeDtypeStruct(a.shape, BF16) for a in casts],
        compiler_params=_params("parallel", "arbitrary"),
        name=name,
    )(lhs, w, *extras, *casts)
    return (res[0], list(res[1:])) if with_casts else res[0]


def _epi_sigmoid(acc, col0):
    return _sigmoid(acc)


def _epi_identity(acc, col0):
    return acc


def _epi_qkv(acc, col0, cos_ref, sin_ref, *, tn, n_q, n_k):
    cos, sin = cos_ref[...], sin_ref[...]
    heads = []
    for h in range(acc.shape[1] // HEAD_DIM):
        col = pl.program_id(1) * tn + col0 + h * HEAD_DIM
        xh = acc[:, h * HEAD_DIM:(h + 1) * HEAD_DIM]
        rot = (xh * cos + pltpu.roll(xh, HEAD_DIM // 2, 1) * sin) * jnp.where(col < n_q, HEAD_DIM ** -0.5, 1.0)
        heads.append(jnp.where(col < n_q + n_k, rot, xh))
    return jnp.concatenate(heads, axis=1)


def _chunk_cumprod(x, reverse):
    n = SUBLANES
    rid = lax.broadcasted_iota(jnp.int32, (n, x.shape[1]), 0)
    slabs = []
    for g in range(x.shape[0] // n):
        y = x[g * n:(g + 1) * n]
        s = 1
        while s < n:
            if reverse:
                y = y * jnp.where(rid < n - s, pltpu.roll(y, n - s, 0), 1.0)
            else:
                y = y * jnp.where(rid >= s, pltpu.roll(y, s, 0), 1.0)
            s *= 2
        slabs.append(y)
    order = range(len(slabs) - 1, -1, -1) if reverse else range(len(slabs))
    total = None
    for g in order:
        if total is not None:
            slabs[g] = slabs[g] * total
        total = slabs[g][0:1] if reverse else slabs[g][n - 1:n]
    return jnp.concatenate(slabs, axis=0)


def _hgrn_prep_kernel(h_ref, wq_ref, wff_ref, wfb_ref, wv_ref, wg_ref, lbf_ref, lbb_ref, *rest, layer, n_pieces):
    pieces = rest[:n_pieces]
    qf_ref, kf_ref, ktf_ref, qb_ref, kb_ref, ktb_ref, v_ref, og_ref, df_ref, db_ref = rest[n_pieces:n_pieces + 10]
    if n_pieces:
        slab = rest[-1]
        cw = pieces[0].shape[1]
        for p, src in enumerate(pieces):
            slab[:, p * cw:(p + 1) * cw] = src[...].astype(slab.dtype)
    h = h_ref[...]
    rows, width = qf_ref.shape
    xf = [_dot(h, wff_ref[...]), _dot(h, wfb_ref[...])]
    xq = _dot(h, wq_ref[...])
    xg = _dot(h, wg_ref[...])
    og_ref[...] = (xg * _sigmoid(xg)).astype(og_ref.dtype)
    v_ref[...] = _dot(h, wv_ref[...]).astype(v_ref.dtype)
    lbs = []
    for lb_ref in (lbf_ref, lbb_ref):
        lbp = lb_ref[...]
        e = jnp.exp(lbp - jnp.max(lbp, axis=0, keepdims=True))
        lbs.append(jnp.sum(e[:layer + 1], axis=0, keepdims=True) / jnp.sum(e, axis=0, keepdims=True))
    for c in range(rows // CHUNK):
        rs = slice(c * CHUNK, (c + 1) * CHUNK)
        q = xq[rs] * _sigmoid(xq[rs]) * (HEAD_DIM ** -0.5)
        for x, lb, reverse, qo, ko, kto, do in (
                (xf[0], lbs[0], False, qf_ref, kf_ref, ktf_ref, df_ref),
                (xf[1], lbs[1], True, qb_ref, kb_ref, ktb_ref, db_ref)):
            f = lb + (1.0 - lb) * _sigmoid(x[rs])
            eb = _chunk_cumprod(f, reverse)
            decay = eb[0:1, :] if reverse else eb[CHUNK - 1:CHUNK, :]
            k_hat = (1.0 - f) / eb
            qo[rs, :] = (q * eb).astype(qo.dtype)
            ko[rs, :] = k_hat.astype(ko.dtype)
            kto[rs, :] = (k_hat * decay).astype(kto.dtype)
            do[c:c + 1, :] = decay


def _hgrn_prep(h, w, lbp, hw, layer, tm, tn, w_full=None):
    T, K = h.shape
    nseg = hw // tn
    steps = (T // tm) * nseg
    wspec = lambda seg: pl.BlockSpec((K, tn), lambda i, j: (0, seg * nseg + j))
    lbspec = lambda seg: pl.BlockSpec((lbp.shape[0], tn), lambda i, j: (0, seg * nseg + j))
    ospec = pl.BlockSpec((tm, tn), lambda i, j: (i, j))
    dspec = pl.BlockSpec((tm // CHUNK, tn), lambda i, j: (i, j))
    act = jax.ShapeDtypeStruct((T, hw), BF16)
    dec = jax.ShapeDtypeStruct((T // CHUNK, hw), F32)
    side_in, side_specs, side_out_spec, side_out_shape = [], [], [], []
    rows = None if w_full is None else _cast_rows_per_step(w_full, steps)
    if rows is not None:
        first, rest_cols = 5 * hw, w_full.shape[1] - 5 * hw
        cw = math.gcd(first, rest_cols)
        if cw % HEAD_DIM == 0:
            for p in range(rest_cols // cw):
                side_in.append(w_full)
                side_specs.append(pl.BlockSpec((rows, cw), lambda i, j, p=p: (i * nseg + j, first // cw + p)))
            side_out_spec = [pl.BlockSpec((rows, rest_cols), lambda i, j: (i * nseg + j, 0))]
            side_out_shape = [jax.ShapeDtypeStruct((K, rest_cols), BF16)]
    res = pl.pallas_call(
        functools.partial(_hgrn_prep_kernel, layer=layer, n_pieces=len(side_in)),
        grid=(T // tm, nseg),
        in_specs=[pl.BlockSpec((tm, K), lambda i, j: (i, 0)), wspec(0), wspec(1), wspec(2), wspec(3), wspec(4),
                  lbspec(0), lbspec(1)] + side_specs,
        out_specs=[ospec] * 8 + [dspec] * 2 + side_out_spec,
        out_shape=[act] * 8 + [dec] * 2 + side_out_shape,
        compiler_params=_params("parallel", "arbitrary"),
        name="in_hgrn",
    )(h, w, w, w, w, w, lbp, lbp, *side_in)
    return res[:10], (res[10] if side_in else None)


def _hgrn_kernel(qf_ref, kf_ref, ktf_ref, qb_ref, kb_ref, ktb_ref, v_ref, og_ref, df_ref, db_ref, nw_ref, o_ref,
                 accf_ref, accb_ref, mf_ref, mb_ref, sf_ref, sb_ref, *, seq):
    R = HGRN_ROWS
    nsb = seq // R
    ncs = R // CHUNK

    row = lax.broadcasted_iota(jnp.int32, (R, R), 0)
    col = lax.broadcasted_iota(jnp.int32, (R, R), 1)
    same = (row // CHUNK) == (col // CHUNK)
    mf_ref[...] = jnp.where(same & (col <= row), 1.0, 0.0).astype(BF16)
    mb_ref[...] = jnp.where(same & (col >= row), 1.0, 0.0).astype(BF16)
    sf_ref[...] = jnp.zeros_like(sf_ref)
    sb_ref[...] = jnp.zeros_like(sb_ref)

    U = HGRN_UNROLL if nsb % HGRN_UNROLL == 0 else 1
    W = U * R
    nw = U * ncs
    blocks = [slice(u * R, (u + 1) * R) for u in range(U)]
    chunks = [slice(c * CHUNK, (c + 1) * CHUNK) for c in range(nw)]

    nt = nsb // U

    def finish_rows(r0, n):
        o = accf_ref[pl.ds(r0, n), :] + accb_ref[pl.ds(r0, n), :]
        y = _rms(o, nw_ref[...]) * og_ref[pl.ds(r0, n), :].astype(F32)
        o_ref[pl.ds(r0, n), :] = y.astype(o_ref.dtype)

    def body(t, carry, finalize):
        dirs = ((t, qf_ref, kf_ref, ktf_ref, df_ref, mf_ref, sf_ref, accf_ref, range(nw)),
                (nt - 1 - t, qb_ref, kb_ref, ktb_ref, db_ref, mb_ref, sb_ref, accb_ref, range(nw - 1, -1, -1)))
        r0s = [pl.multiple_of(d[0] * W, W) for d in dirs]
        qs = [d[1][pl.ds(r0, W), :] for d, r0 in zip(dirs, r0s)]
        vs = [v_ref[pl.ds(r0, W), :] for r0 in r0s]
        scores = []
        for d, r0, q in zip(dirs, r0s, qs):
            k = d[2][pl.ds(r0, W), :]
            scores.append([_dot_nt(q[bl], k[bl]) for bl in blocks])
        updates = []
        for d, r0, v in zip(dirs, r0s, vs):
            kt = d[3][pl.ds(r0, W), :]
            updates.append([_dot_tn(v[sl], kt[sl]) for sl in chunks])
        entering = []
        for d, upd in zip(dirs, updates):
            idx, d_ref, s_ref = d[0], d[4], d[6]
            st = s_ref[...]
            ent = [None] * nw
            for c in d[8]:
                ent[c] = st.astype(BF16)
                st = st * d_ref[pl.ds(idx * nw + c, 1), :] + upd[c]
            s_ref[...] = st
            entering.append(ent)
        inter = [[_dot_nt(q[sl], ent[c]) for c, sl in enumerate(chunks)] for q, ent in zip(qs, entering)]
        for d, r0, sc, v, o_inter in zip(dirs, r0s, scores, vs, inter):
            mask = d[5][...] > 0
            for u, bl in enumerate(blocks):
                o = _dot(jnp.where(mask, sc[u], 0.0).astype(BF16), v[bl])
                d[7][pl.ds(r0 + u * R, R), :] = o + jnp.concatenate(o_inter[u * ncs:(u + 1) * ncs], axis=0)
        if finalize:
            for r0 in r0s:
                finish_rows(r0, W)
        return carry

    if nt % 2 == 0:
        lax.fori_loop(0, nt // 2, functools.partial(body, finalize=False), 0)
        lax.fori_loop(nt // 2, nt, functools.partial(body, finalize=True), 0)
    else:
        lax.fori_loop(0, nt, functools.partial(body, finalize=False), 0)

        def finish(t, carry):
            finish_rows(pl.multiple_of(t * W, W), W)
            return carry

        lax.fori_loop(0, nt, finish, 0)


def _hgrn(prep, nw, batch, seq):
    *acts, df, db = prep
    T, HW = acts[0].shape
    H = HW // HEAD_DIM
    blk = pl.BlockSpec((seq, HEAD_DIM), lambda b, h: (b, h))
    dblk = pl.BlockSpec((seq // CHUNK, HEAD_DIM), lambda b, h: (b, h))
    return pl.pallas_call(
        functools.partial(_hgrn_kernel, seq=seq),
        grid=(batch, H),
        in_specs=[blk] * 8 + [dblk] * 2 + [pl.BlockSpec((1, HEAD_DIM), lambda b, h: (0, 0))],
        out_specs=blk,
        out_shape=jax.ShapeDtypeStruct((T, HW), BF16),
        scratch_shapes=[pltpu.VMEM((seq, HEAD_DIM), F32), pltpu.VMEM((seq, HEAD_DIM), F32),
                        pltpu.VMEM((HGRN_ROWS, HGRN_ROWS), BF16), pltpu.VMEM((HGRN_ROWS, HGRN_ROWS), BF16),
                        pltpu.VMEM((HEAD_DIM, HEAD_DIM), F32), pltpu.VMEM((HEAD_DIM, HEAD_DIM), F32)],
        compiler_params=_params("parallel", "parallel"),
        name="hgrn2",
    )(*acts, df, db, nw)


def _attn_kernel(sink_ref, q_ref, kp_ref, kc_ref, kn_ref, vp_ref, vc_ref, vn_ref, o_ref, *, seq):
    G, D, BLK = GQA_GROUP, HEAD_DIM, ATTN_BLOCK
    rows = q_ref.shape[0]
    kvh = pl.program_id(1)
    n = pl.program_id(2)
    k = jnp.concatenate([kp_ref[...], kc_ref[...], kn_ref[...]], axis=0)
    v = jnp.concatenate([vp_ref[...], vc_ref[...], vn_ref[...]], axis=0)
    v1 = jnp.concatenate([v, jnp.ones_like(v)], axis=1)
    r = lax.broadcasted_iota(jnp.int32, (G * BLK, 3 * BLK), 0) % BLK
    c = lax.broadcasted_iota(jnp.int32, (G * BLK, 3 * BLK), 1)
    band = jnp.where(jnp.abs(c - BLK - r) <= WINDOW, 0.0, -jnp.inf)
    kpos = n * rows - BLK + lax.broadcasted_iota(jnp.int32, (1, rows + 2 * BLK), 1)
    inside = jnp.where((kpos >= 0) & (kpos < seq), 0.0, -jnp.inf)
    sk = jnp.concatenate([jnp.full((BLK, BLK), sink_ref[kvh * G + h], F32) for h in range(G)], axis=0)
    nsub = rows // BLK
    scores = []
    for j in range(nsub):
        q = q_ref[j * BLK:(j + 1) * BLK, :]
        q4 = jnp.concatenate([q[:, h * D:(h + 1) * D] for h in range(G)], axis=0)
        scores.append(_dot_nt(q4, k[j * BLK:(j + 3) * BLK]))
    probs, sink_terms = [], []
    for j in range(nsub):
        s = scores[j] + band
        if j == 0 or j == nsub - 1:
            s = s + inside[:, j * BLK:(j + 3) * BLK]
        m = jnp.maximum(jnp.broadcast_to(jnp.max(s, axis=-1, keepdims=True), (G * BLK, BLK)), sk)
        probs.append(jnp.concatenate(
            [jnp.exp(s[:, i * BLK:(i + 1) * BLK] - m) for i in range(3)], axis=1).astype(BF16))
        sink_terms.append(jnp.exp(sk - m))
    for j in range(nsub):
        pv = _dot(probs[j], v1[j * BLK:(j + 3) * BLK])
        o = pv[:, :D] / (pv[:, D:] + sink_terms[j])
        o_ref[j * BLK:(j + 1) * BLK, :] = jnp.concatenate(
            [o[h * BLK:(h + 1) * BLK] for h in range(G)], axis=1).astype(o_ref.dtype)


def _attn(qkv, sink, batch, seq, n_q_cols, n_kv_cols, rows):
    T = qkv.shape[0]
    G, D, BLK = GQA_GROUP, HEAD_DIM, ATTN_BLOCK
    kvh = n_kv_cols // D
    nb = seq // BLK
    nq = seq // rows
    sub = rows // BLK
    k_off = n_q_cols // D
    v_off = (n_q_cols + n_kv_cols) // D
    qspec = pl.BlockSpec((rows, G * D), lambda b, h, n: (b * nq + n, h))
    prev = lambda b, n: b * nb + jnp.maximum(n * sub - 1, 0)
    nxt = lambda b, n: b * nb + jnp.minimum((n + 1) * sub, nb - 1)
    edge = lambda f, off: pl.BlockSpec((BLK, D), lambda b, h, n: (f(b, n), off + h))
    cur = lambda off: pl.BlockSpec((rows, D), lambda b, h, n: (b * nq + n, off + h))
    return pl.pallas_call(
        functools.partial(_attn_kernel, seq=seq),
        grid=(batch, kvh, nq),
        in_specs=[pl.BlockSpec(memory_space=pltpu.SMEM), qspec,
                  edge(prev, k_off), cur(k_off), edge(nxt, k_off), edge(prev, v_off), cur(v_off), edge(nxt, v_off)],
        out_specs=qspec,
        out_shape=jax.ShapeDtypeStruct((T, n_q_cols), BF16),
        compiler_params=_params("parallel", "parallel", "arbitrary"),
        name="swa_sink",
    )(sink, *([qkv] * 7))


def _merge_kernel(oh_ref, oa_ref, wh_ref, wa_ref, ga_ref, gb_ref, o_ref):
    oh, oa = oh_ref[...], oa_ref[...]
    for cols in _col_slabs(o_ref.shape[1]):
        ya = _dot(oh, wh_ref[:, cols])
        yb = _dot(oa, wa_ref[:, cols])
        o_ref[:, cols] = (ga_ref[:, cols].astype(F32) * ya + gb_ref[:, cols].astype(F32) * yb).astype(o_ref.dtype)


def _merge(oh, oa, wh, wa, gates, tm, tn):
    T, KH = oh.shape
    KA = oa.shape[1]
    D = wh.shape[1]
    nj = D // tn
    return pl.pallas_call(
        _merge_kernel,
        grid=(T // tm, nj),
        in_specs=[pl.BlockSpec((tm, KH), lambda i, j: (i, 0)), pl.BlockSpec((tm, KA), lambda i, j: (i, 0)),
                  pl.BlockSpec((KH, tn), lambda i, j: (0, j)), pl.BlockSpec((KA, tn), lambda i, j: (0, j)),
                  pl.BlockSpec((tm, tn), lambda i, j: (i, j)), pl.BlockSpec((tm, tn), lambda i, j: (i, j + nj))],
        out_specs=pl.BlockSpec((tm, tn), lambda i, j: (i, j)),
        out_shape=jax.ShapeDtypeStruct((T, D), BF16),
        compiler_params=_params("parallel", "arbitrary"),
        name="gated_merge",
    )(oh, oa, wh, wa, gates, gates)


def _mlp_kernel(h_ref, wu_ref, wd_ref, o_ref):
    @pl.when(pl.program_id(1) == 0)
    def _():
        o_ref[...] = jnp.zeros_like(o_ref)

    u = jnp.square(jnp.maximum(_dot(h_ref[...], wu_ref[...]), 0.0)).astype(BF16)
    o_ref[...] += _dot(u, wd_ref[...])


def _mlp(h, wu, wd, tm, tf):
    T, D = h.shape
    FF = wu.shape[1]
    return pl.pallas_call(
        _mlp_kernel,
        grid=(T // tm, FF // tf),
        in_specs=[pl.BlockSpec((tm, D), lambda i, j: (i, 0)),
                  pl.BlockSpec((D, tf), lambda i, j: (0, j)),
                  pl.BlockSpec((tf, D), lambda i, j: (j, 0))],
        out_specs=pl.BlockSpec((tm, D), lambda i, j: (i, 0)),
        out_shape=jax.ShapeDtypeStruct((T, D), F32),
        compiler_params=_params("parallel", "arbitrary"),
        name="relu2_mlp",
    )(h, wu, wd)


def _ple_kernel(r_ref, p_ref, wg_ref, wp_ref, g_ref, o_ref, xb_ref, *, tn):
    j = pl.program_id(1)

    @pl.when(j == 0)
    def _():
        xb_ref[...] = r_ref[...].astype(xb_ref.dtype)

    x, p = xb_ref[...], p_ref[...].astype(BF16)
    for cols in _col_slabs(tn):
        width = cols.stop - cols.start
        dst = pl.ds(pl.multiple_of(j * tn + cols.start, width), width)
        o_ref[:, dst] = _dot(p, wp_ref[:, cols]) * _sigmoid(_dot(x, wg_ref[:, cols]))

    @pl.when(j == pl.num_programs(1) - 1)
    def _():
        g = g_ref[...]

        def rows(t, carry):
            rs = pl.ds(pl.multiple_of(t * RESID_ROWS, RESID_ROWS), RESID_ROWS)
            o_ref[rs, :] = r_ref[rs, :] + _rms(o_ref[rs, :], g)
            return carry

        lax.fori_loop(0, o_ref.shape[0] // RESID_ROWS, rows, 0)


def _ple(r, p, wg, wp, g, tm, tn):
    T, D = r.shape
    P = p.shape[1]
    row = lambda i, j: (i, 0)
    return pl.pallas_call(
        functools.partial(_ple_kernel, tn=tn),
        grid=(T // tm, D // tn),
        in_specs=[pl.BlockSpec((tm, D), row), pl.BlockSpec((tm, P), row),
                  pl.BlockSpec((D, tn), lambda i, j: (0, j)), pl.BlockSpec((P, tn), lambda i, j: (0, j)),
                  pl.BlockSpec((1, D), lambda i, j: (0, 0))],
        out_specs=pl.BlockSpec((tm, D), row),
        out_shape=jax.ShapeDtypeStruct((T, D), F32),
        scratch_shapes=[pltpu.VMEM((tm, D), BF16)],
        compiler_params=_params("parallel", "arbitrary"),
        name="ple_gate",
    )(r, p, wg, wp, g)


def _rope_tables(seq):
    half = HEAD_DIM // 2
    inv_freq = ROPE_THETA ** (-jnp.arange(0, HEAD_DIM, 2, dtype=F32) / HEAD_DIM)
    ang = jnp.arange(seq, dtype=F32)[:, None] * inv_freq[None, :]
    cos, sin = jnp.cos(ang), jnp.sin(ang)
    assert cos.shape == (seq, half)
    return jnp.concatenate([cos, cos], axis=1), jnp.concatenate([-sin, sin], axis=1)


def kernel(x, p, norm_mix_pre, norm_mix_post, w_in, lb_fwd, lb_bwd, hgrn_norm, attn_sink, w_hgrn_proj,
           w_attn_proj, w_out, norm_mlp_pre, norm_mlp_post, w_mlp_up, w_mlp_down, w_ple, w_ple_gate, norm_ple):
    B, S, D = x.shape
    T = B * S
    depth = w_in.shape[0]
    HW = w_hgrn_proj.shape[1]
    AW = w_attn_proj.shape[1]
    KVW = AW // GQA_GROUP
    assert S % HGRN_ROWS == 0 and S % ATTN_BLOCK == 0 and hgrn_norm.shape[-1] == HEAD_DIM
    assert w_in.shape[2] == 5 * HW + AW + 2 * KVW + 2 * D

    tm = _pick(S, 1024, 512, 256, 128)
    tm_mlp = _pick(T, 512, 256, 128)
    tr = _pick(T, 256, 128)
    tr_wide = _pick(T, 512, 256, 128)
    cos_t, sin_t = _rope_tables(S)
    vec = lambda a: a.reshape(1, -1)

    xf = x.reshape(T, D)
    for i in range(depth):
        off_aq, off_gate = 0, AW + 2 * KVW

        h = _norm_cast(xf, vec(norm_mix_pre[i]), tr_wide)
        tn = lambda off, n: _pick(math.gcd(off, n), 1024, 512, 256, 128)
        prep, w_rest = _hgrn_prep(h, w_in[i][:, :5 * HW].astype(BF16), jnp.concatenate([lb_fwd, lb_bwd], axis=1),
                                  HW, i, tm, 2 * HEAD_DIM, w_full=w_in[i])
        if w_rest is None:
            w_rest = w_in[i][:, 5 * HW:].astype(BF16)
        mm = functools.partial(_mm, h, w_rest, tm=tm)
        tn_a = tn(off_aq, AW + 2 * KVW)
        nrb = S // tm
        rope_spec = pl.BlockSpec((tm, HEAD_DIM), lambda r, j: (r % nrb, 0))
        qkv = mm(off_aq, AW + 2 * KVW, functools.partial(_epi_qkv, tn=tn_a, n_q=AW, n_k=KVW), BF16, tn=tn_a,
                 extras=(cos_t, sin_t), extra_specs=(rope_spec, rope_spec), name="in_qkv")
        tn_g = tn(off_gate, 2 * D)
        later = [w_out[i], w_mlp_up[i], w_mlp_down[i], w_ple_gate[i], w_hgrn_proj[i], w_attn_proj[i]]
        steps = (T // tm) * (2 * D // tn_g)
        on_side = [a for a in later if _cast_rows_per_step(a, steps) is not None]
        gates, side = mm(off_gate, 2 * D, _epi_sigmoid, BF16, tn=tn_g, casts=on_side, name="in_gates")
        side = iter(side)
        w_o, w_up, w_dn, w_pg, w_hp, w_ap = [
            next(side) if _cast_rows_per_step(a, steps) is not None else a.astype(BF16) for a in later]

        o_h = _hgrn(prep, vec(hgrn_norm[i]), B, S)
        o_a = _attn(qkv, attn_sink[i], B, S, AW, KVW, _pick(S, 2048, 1024, 512, 256, 128))
        y = _merge(o_h, o_a, w_hp, w_ap, gates, tm, _pick(D, 1024, 512, 256, 128))
        mix = _mm(y, w_o, 0, D, _epi_identity, F32, tm, _pick(D, 1024, 512, 256, 128), name="w_out")
        xf, h2 = _resid_norm(xf, mix, vec(norm_mix_post[i]), vec(norm_mlp_pre[i]), tr)

        d = _mlp(h2, w_up, w_dn, tm_mlp, _pick(w_mlp_up.shape[2], 1024, 512, 256))
        xf = _resid(xf, d, vec(norm_mlp_post[i]), tr_wide)

        xf = _ple(xf, p[i].reshape(T, -1), w_pg, w_ple[i].astype(BF16), vec(norm_ple[i]),
                  tm_mlp, _pick(D, 1024, 512, 256, 128))
    return xf.reshape(B, S, D)
```
